```python
import math
import jax
import jax.numpy as jnp
from jax import lax
import numpy as np

D_MODEL = 2048
BATCH = 16
SEQ = 2048
DEPTH = 4

N_MIXERS = 3
CHUNK = 128
NORM_EPS = 1e-5
D_FF = 4 * D_MODEL

SSD_EXPAND = 2
SSD_INNER = SSD_EXPAND * D_MODEL
SSD_HEAD_DIM = 64
SSD_HEADS = SSD_INNER // SSD_HEAD_DIM
SSD_GROUPS = 8
SSD_HPG = SSD_HEADS // SSD_GROUPS
SSD_STATE = 128
SSD_CONV = 4
SSD_CONV_DIM = SSD_INNER + 2 * SSD_GROUPS * SSD_STATE
SSD_PROJ = SSD_INNER + SSD_CONV_DIM + SSD_HEADS
N_SSD_LAYERS = (DEPTH + 2) // 3

MLSTM_HEADS = 4
MLSTM_QK = D_MODEL // 2
MLSTM_V = D_MODEL
MLSTM_DQK = MLSTM_QK // MLSTM_HEADS
MLSTM_DV = MLSTM_V // MLSTM_HEADS
MLSTM_PROJ = 2 * MLSTM_QK + 2 * MLSTM_V + 2 * MLSTM_HEADS
N_MLSTM_LAYERS = (DEPTH + 1) // 3

S5_WIDTH = D_MODEL
S5_GROUP = 16
S5_GROUPS = S5_WIDTH // S5_GROUP
S5_STATE = 64
N_S5_LAYERS = DEPTH // 3

DT_MIN = 1e-3
DT_MAX = 1e-1

kernel_name = 'hybrid_ssd_mlstm_s5_trunk'


def rmsnorm(x, g):
    xf = x.astype(jnp.float32)
    y = xf * lax.rsqrt(jnp.mean(xf * xf, axis=-1, keepdims=True) + NORM_EPS)
    return y * g.astype(jnp.float32)


def to_chunks(t):
    b, s = t.shape[0], t.shape[1]
    return jnp.moveaxis(t.reshape((b, s // CHUNK, CHUNK) + t.shape[2:]), 1, 0)


def from_chunks(t):
    nc, b, l = t.shape[0], t.shape[1], t.shape[2]
    return jnp.moveaxis(t, 0, 1).reshape((b, nc * l) + t.shape[3:])


def causal_depthwise_conv(x, w, bias):
    k, c = w.shape
    y = lax.conv_general_dilated(x, w[:, None, :].astype(x.dtype), window_strides=(1,),
                                 padding=[(k - 1, 0)], dimension_numbers=('NWC', 'WIO', 'NWC'),
                                 feature_group_count=c)
    return y + bias.astype(x.dtype)


def ssd_chunked_scan(xdt, loga, bmat, cmat):
    b = xdt.shape[0]
    causal = jnp.tril(jnp.ones((CHUNK, CHUNK), dtype=bool))

    def step(state, inp):
        x_c, la_c, b_c, c_c = inp
        acs = jnp.cumsum(la_c, axis=1)
        seg = acs[:, :, None] - acs[:, None, :]
        decay = jnp.exp(jnp.where(causal[None, :, :, None, None], seg, -jnp.inf))
        cb = jnp.einsum('blgn,bsgn->blsg', c_c, b_c)
        y_diag = jnp.einsum('blsgj,bsgjp->blgjp', cb[..., None] * decay, x_c)
        y_off = jnp.einsum('blgn,bgjpn->blgjp', c_c, state) * jnp.exp(acs)[..., None]
        tail = jnp.exp(acs[:, -1:] - acs)
        new_state = (state * jnp.exp(acs[:, -1])[..., None, None]
                     + jnp.einsum('bsgn,bsgj,bsgjp->bgjpn', b_c, tail, x_c))
        return new_state, y_diag + y_off

    state0 = jnp.zeros((b, SSD_GROUPS, SSD_HPG, SSD_HEAD_DIM, SSD_STATE), jnp.float32)
    _, ys = lax.scan(step, state0, (to_chunks(xdt), to_chunks(loga), to_chunks(bmat), to_chunks(cmat)))
    return from_chunks(ys)


def ssd_mixer(h, w_in, conv_w, conv_b, dt_bias, a_log, d_skip, norm_g, w_out):
    b, s, _ = h.shape
    f32 = jnp.float32
    proj = h @ w_in
    z = proj[..., :SSD_INNER]
    xbc = proj[..., SSD_INNER:SSD_INNER + SSD_CONV_DIM]
    dt_raw = proj[..., SSD_INNER + SSD_CONV_DIM:]
    xbc = jax.nn.silu(causal_depthwise_conv(xbc, conv_w, conv_b)).astype(f32)
    nb = SSD_GROUPS * SSD_STATE
    xs = xbc[..., :SSD_INNER].reshape(b, s, SSD_GROUPS, SSD_HPG, SSD_HEAD_DIM)
    bmat = xbc[..., SSD_INNER:SSD_INNER + nb].reshape(b, s, SSD_GROUPS, SSD_STATE)
    cmat = xbc[..., SSD_INNER + nb:].reshape(b, s, SSD_GROUPS, SSD_STATE)
    dt = jax.nn.softplus(dt_raw.astype(f32) + dt_bias.astype(f32)).reshape(b, s, SSD_GROUPS, SSD_HPG)
    a = -jnp.exp(a_log.astype(f32)).reshape(SSD_GROUPS, SSD_HPG)
    y = ssd_chunked_scan(xs * dt[..., None], dt * a, bmat, cmat)
    y = y + d_skip.astype(f32).reshape(SSD_GROUPS, SSD_HPG, 1) * xs
    y = y.reshape(b, s, SSD_INNER) * jax.nn.silu(z.astype(f32))
    return rmsnorm(y, norm_g).astype(h.dtype) @ w_out


def mlstm_chunked(q, k, v, i_pre, log_f):
    b = q.shape[0]
    causal = jnp.tril(jnp.ones((CHUNK, CHUNK), dtype=bool))

    def step(carry, inp):
        c_st, n_st, m_st = carry
        q_c, k_c, v_c, i_c, lf_c = inp
        bcum = jnp.cumsum(lf_c, axis=1)
        intra = bcum[:, :, None] - bcum[:, None, :] + i_c[:, None, :]
        intra = jnp.where(causal[None, :, :, None], intra, -jnp.inf)
        inter = bcum + m_st[:, None]
        m_t = jnp.maximum(inter, jnp.max(intra, axis=2))
        w = jnp.exp(intra - m_t[:, :, None])
        scale_inter = jnp.exp(inter - m_t)
        qk = jnp.einsum('blhd,bshd->blsh', q_c, k_c) * w
        num = (jnp.einsum('blsh,bshe->blhe', qk, v_c)
               + scale_inter[..., None] * jnp.einsum('blhd,bhde->blhe', q_c, c_st))
        den = jnp.sum(qk, axis=2) + scale_inter * jnp.einsum('blhd,bhd->blh', q_c, n_st)
        h_c = num / jnp.maximum(jnp.abs(den), jnp.exp(-m_t))[..., None]
        b_last = bcum[:, -1]
        tail = b_last[:, None] - bcum + i_c
        m_new = jnp.maximum(b_last + m_st, jnp.max(tail, axis=1))
        carry_scale = jnp.exp(b_last + m_st - m_new)
        wk = jnp.exp(tail - m_new[:, None])
        c_new = carry_scale[..., None, None] * c_st + jnp.einsum('bsh,bshd,bshe->bhde', wk, k_c, v_c)
        n_new = carry_scale[..., None] * n_st + jnp.einsum('bsh,bshd->bhd', wk, k_c)
        return (c_new, n_new, m_new), h_c

    carry0 = (jnp.zeros((b, MLSTM_HEADS, MLSTM_DQK, MLSTM_DV), jnp.float32),
              jnp.zeros((b, MLSTM_HEADS, MLSTM_DQK), jnp.float32),
              jnp.zeros((b, MLSTM_HEADS), jnp.float32))
    _, hs = lax.scan(step, carry0, (to_chunks(q), to_chunks(k), to_chunks(v),
                                    to_chunks(i_pre), to_chunks(log_f)))
    return from_chunks(hs)


def mlstm_mixer(h, w_in, gate_b, head_g, w_out):
    b, s, _ = h.shape
    f32 = jnp.float32
    proj = h @ w_in
    q = proj[..., :MLSTM_QK].astype(f32).reshape(b, s, MLSTM_HEADS, MLSTM_DQK) * (MLSTM_DQK ** -0.5)
    k = proj[..., MLSTM_QK:2 * MLSTM_QK].astype(f32).reshape(b, s, MLSTM_HEADS, MLSTM_DQK)
    v = proj[..., 2 * MLSTM_QK:2 * MLSTM_QK + MLSTM_V].astype(f32).reshape(b, s, MLSTM_HEADS, MLSTM_DV)
    o = proj[..., 2 * MLSTM_QK + MLSTM_V:2 * MLSTM_QK + 2 * MLSTM_V].astype(f32)
    gates = proj[..., 2 * MLSTM_QK + 2 * MLSTM_V:].astype(f32) + gate_b.astype(f32)
    i_pre = gates[..., :MLSTM_HEADS]
    log_f = jax.nn.log_sigmoid(gates[..., MLSTM_HEADS:])
    hs = mlstm_chunked(q, k, v, i_pre, log_f)
    hs = rmsnorm(hs, head_g.reshape(MLSTM_HEADS, MLSTM_DV))
    out = jax.nn.sigmoid(o) * hs.reshape(b, s, MLSTM_V)
    return out.astype(h.dtype) @ w_out


def complex_scan_combine(e1, e2):
    a1r, a1i, b1r, b1i = e1
    a2r, a2i, b2r, b2i = e2
    return (a2r * a1r - a2i * a1i, a2r * a1i + a2i * a1r,
            a2r * b1r - a2i * b1i + b2r, a2r * b1i + a2i * b1r + b2i)


def s5_mixer(h, w_in, b_re, b_im, c_re, c_im, d_skip, log_dt, a_re, a_im, w_out):
    b, s, _ = h.shape
    f32 = jnp.float32
    u = (h @ w_in).astype(f32).reshape(b, s, S5_GROUPS, S5_GROUP)
    ar, ai = a_re.astype(f32), a_im.astype(f32)
    dt = jnp.exp(log_dt.astype(f32))[:, None]
    mag = jnp.exp(ar * dt)
    abar_re, abar_im = mag * jnp.cos(ai * dt), mag * jnp.sin(ai * dt)
    den = ar * ar + ai * ai
    zoh_re = ((abar_re - 1.0) * ar + abar_im * ai) / den
    zoh_im = (abar_im * ar - (abar_re - 1.0) * ai) / den
    br, bi = b_re.astype(f32), b_im.astype(f32)
    bbar_re = zoh_re[..., None] * br - zoh_im[..., None] * bi
    bbar_im = zoh_re[..., None] * bi + zoh_im[..., None] * br
    cr, ci = c_re.astype(f32), c_im.astype(f32)

    def step(carry, u_c):
        s_re, s_im = carry
        bu_re = jnp.einsum('blgc,gpc->blgp', u_c, bbar_re)
        bu_im = jnp.einsum('blgc,gpc->blgp', u_c, bbar_im)
        bu_re = bu_re.at[:, 0].add(abar_re * s_re - abar_im * s_im)
        bu_im = bu_im.at[:, 0].add(abar_re * s_im + abar_im * s_re)
        a_b_re = jnp.broadcast_to(abar_re, bu_re.shape)
        a_b_im = jnp.broadcast_to(abar_im, bu_im.shape)
        _, _, st_re, st_im = lax.associative_scan(complex_scan_combine, (a_b_re, a_b_im, bu_re, bu_im), axis=1)
        y = jnp.einsum('blgp,gcp->blgc', st_re, cr) - jnp.einsum('blgp,gcp->blgc', st_im, ci)
        return (st_re[:, -1], st_im[:, -1]), y

    carry0 = (jnp.zeros((b, S5_GROUPS, S5_STATE), f32), jnp.zeros((b, S5_GROUPS, S5_STATE), f32))
    _, ys = lax.scan(step, carry0, to_chunks(u))
    y = from_chunks(ys) + d_skip.astype(f32).reshape(S5_GROUPS, S5_GROUP) * u
    y = jax.nn.gelu(y.reshape(b, s, S5_WIDTH)).astype(h.dtype)
    glu = y @ w_out
    return glu[..., :D_MODEL] * jax.nn.sigmoid(glu[..., D_MODEL:])


def sq_relu_mlp(h, w1, w2):
    return jnp.square(jax.nn.relu(h @ w1)) @ w2


def setup_inputs(seed: int = 0) -> dict:
    key = jax.random.key(seed)
    ks = iter(jax.random.split(key, 40))
    f32 = jnp.float32

    def nrm(shape, scale):
        return jax.random.normal(next(ks), shape, f32) * scale

    def log_uniform_dt(shape):
        return jax.random.uniform(next(ks), shape, f32, minval=math.log(DT_MIN), maxval=math.log(DT_MAX))

    x = nrm((BATCH, SEQ, D_MODEL), 1.0)
    norm_mix_g = 1.0 + nrm((DEPTH, D_MODEL), 0.02)
    norm_mlp_g = 1.0 + nrm((DEPTH, D_MODEL), 0.02)
    ssd_w_in = nrm((N_SSD_LAYERS, D_MODEL, SSD_PROJ), D_MODEL ** -0.5)
    ssd_conv_w = nrm((N_SSD_LAYERS, SSD_CONV, SSD_CONV_DIM), SSD_CONV ** -0.5)
    ssd_conv_b = nrm((N_SSD_LAYERS, SSD_CONV_DIM), 0.02)
    dt0 = jnp.exp(log_uniform_dt((N_SSD_LAYERS, SSD_HEADS)))
    ssd_dt_bias = dt0 + jnp.log(-jnp.expm1(-dt0))
    ssd_a_log = jnp.log(jax.random.uniform(next(ks), (N_SSD_LAYERS, SSD_HEADS), f32, minval=1.0, maxval=16.0))
    ssd_d = 1.0 + nrm((N_SSD_LAYERS, SSD_HEADS), 0.02)
    ssd_norm_g = 1.0 + nrm((N_SSD_LAYERS, SSD_INNER), 0.02)
    ssd_w_out = nrm((N_SSD_LAYERS, SSD_INNER, D_MODEL), SSD_INNER ** -0.5)
    mlstm_w_in = nrm((N_MLSTM_LAYERS, D_MODEL, MLSTM_PROJ), D_MODEL ** -0.5)
    ig_b = nrm((N_MLSTM_LAYERS, MLSTM_HEADS), 0.1)
    fg_b = jnp.linspace(3.0, 6.0, MLSTM_HEADS, dtype=f32)[None] + nrm((N_MLSTM_LAYERS, MLSTM_HEADS), 0.1)
    mlstm_gate_b = jnp.concatenate([ig_b, fg_b], axis=-1)
    mlstm_head_g = 1.0 + nrm((N_MLSTM_LAYERS, MLSTM_V), 0.02)
    mlstm_w_out = nrm((N_MLSTM_LAYERS, MLSTM_V, D_MODEL), MLSTM_V ** -0.5)
    s5_w_in = nrm((N_S5_LAYERS, D_MODEL, S5_WIDTH), D_MODEL ** -0.5)
    s5_b_re = nrm((N_S5_LAYERS, S5_GROUPS, S5_STATE, S5_GROUP), (2 * S5_GROUP) ** -0.5)
    s5_b_im = nrm((N_S5_LAYERS, S5_GROUPS, S5_STATE, S5_GROUP), (2 * S5_GROUP) ** -0.5)
    s5_c_re = nrm((N_S5_LAYERS, S5_GROUPS, S5_GROUP, S5_STATE), 0.5)
    s5_c_im = nrm((N_S5_LAYERS, S5_GROUPS, S5_GROUP, S5_STATE), 0.5)
    s5_d = nrm((N_S5_LAYERS, S5_WIDTH), 1.0)
    s5_log_dt = log_uniform_dt((N_S5_LAYERS, S5_GROUPS))
    s5_a_re = -0.5 + nrm((N_S5_LAYERS, S5_GROUPS, S5_STATE), 0.01)
    s5_a_im = (math.pi * jnp.arange(S5_STATE, dtype=f32))[None, None] + nrm((N_S5_LAYERS, S5_GROUPS, S5_STATE), 0.01)
    s5_w_out = nrm((N_S5_LAYERS, S5_WIDTH, 2 * D_MODEL), S5_WIDTH ** -0.5)
    mlp_w1 = nrm((DEPTH, D_MODEL, D_FF), D_MODEL ** -0.5)
    mlp_w2 = nrm((DEPTH, D_FF, D_MODEL), D_FF ** -0.5)
    final_norm_g = 1.0 + nrm((D_MODEL,), 0.02)
    return {'x': x, 'norm_mix_g': norm_mix_g, 'norm_mlp_g': norm_mlp_g,
            'ssd_w_in': ssd_w_in, 'ssd_conv_w': ssd_conv_w, 'ssd_conv_b': ssd_conv_b,
            'ssd_dt_bias': ssd_dt_bias, 'ssd_a_log': ssd_a_log, 'ssd_d': ssd_d,
            'ssd_norm_g': ssd_norm_g, 'ssd_w_out': ssd_w_out,
            'mlstm_w_in': mlstm_w_in, 'mlstm_gate_b': mlstm_gate_b, 'mlstm_head_g': mlstm_head_g,
            'mlstm_w_out': mlstm_w_out,
            's5_w_in': s5_w_in, 's5_b_re': s5_b_re, 's5_b_im': s5_b_im, 's5_c_re': s5_c_re,
            's5_c_im': s5_c_im, 's5_d': s5_d, 's5_log_dt': s5_log_dt, 's5_a_re': s5_a_re,
            's5_a_im': s5_a_im, 's5_w_out': s5_w_out,
            'mlp_w1': mlp_w1, 'mlp_w2': mlp_w2, 'final_norm_g': final_norm_g}


def reference(x, norm_mix_g, norm_mlp_g,
              ssd_w_in, ssd_conv_w, ssd_conv_b, ssd_dt_bias, ssd_a_log, ssd_d, ssd_norm_g, ssd_w_out,
              mlstm_w_in, mlstm_gate_b, mlstm_head_g, mlstm_w_out,
              s5_w_in, s5_b_re, s5_b_im, s5_c_re, s5_c_im, s5_d, s5_log_dt, s5_a_re, s5_a_im, s5_w_out,
              mlp_w1, mlp_w2, final_norm_g):
    h = x
    for layer in range(DEPTH):
        kind, idx = layer % N_MIXERS, layer // N_MIXERS
        hn = rmsnorm(h, norm_mix_g[layer]).astype(h.dtype)
        if kind == 0:
            mix = ssd_mixer(hn, ssd_w_in[idx], ssd_conv_w[idx], ssd_conv_b[idx], ssd_dt_bias[idx],
                            ssd_a_log[idx], ssd_d[idx], ssd_norm_g[idx], ssd_w_out[idx])
        elif kind == 1:
            mix = mlstm_mixer(hn, mlstm_w_in[idx], mlstm_gate_b[idx], mlstm_head_g[idx], mlstm_w_out[idx])
        else:
            mix = s5_mixer(hn, s5_w_in[idx], s5_b_re[idx], s5_b_im[idx], s5_c_re[idx], s5_c_im[idx],
                           s5_d[idx], s5_log_dt[idx], s5_a_re[idx], s5_a_im[idx], s5_w_out[idx])
        h = h + mix.astype(h.dtype)
        hn = rmsnorm(h, norm_mlp_g[layer]).astype(h.dtype)
        h = h + sq_relu_mlp(hn, mlp_w1[layer], mlp_w2[layer]).astype(h.dtype)
    return rmsnorm(h, final_norm_g).astype(x.dtype)
```

```python
import functools
import math

import jax
import jax.numpy as jnp
from jax import lax
from jax.experimental import pallas as pl
from jax.experimental.pallas import tpu as pltpu

F32 = jnp.float32
BF16 = jnp.bfloat16

NORM_EPS = 1e-5
LANES = 128
VMEM_LIMIT_BYTES = 56 * 1024 * 1024

SSD_HEAD_DIM = 64
SSD_GROUPS = 8
SSD_HPG = 8
SSD_STATE = 128
SSD_CONV = 4
SSD_GROUP_W = SSD_HPG * SSD_HEAD_DIM
SSD_CHUNK = 128
CONV_HALO = 8

MLSTM_HEADS = 4
MLSTM_DQK = 256
MLSTM_DV = 512
MLSTM_CHUNK = 128

S5_GROUP = 16
S5_STATE = 64
S5_GROUPS_PER_BLOCK = LANES // S5_GROUP
S5_BLOCK_STATE = S5_GROUPS_PER_BLOCK * S5_STATE
S5_TIME_BLOCK = 64


def _params(*semantics):
    return pltpu.CompilerParams(dimension_semantics=semantics, vmem_limit_bytes=VMEM_LIMIT_BYTES)


def _dot(a, b):
    return jnp.dot(a, b, preferred_element_type=F32)


def _dot_nt(a, b):
    return lax.dot_general(a, b, (((1,), (1,)), ((), ())), preferred_element_type=F32)


def _dot_tn(a, b):
    return lax.dot_general(a, b, (((0,), (0,)), ((), ())), preferred_element_type=F32)


def _split3(x):
    hi = x.astype(BF16)
    r1 = x - hi.astype(F32)
    mid = r1.astype(BF16)
    lo = (r1 - mid.astype(F32)).astype(BF16)
    return hi, mid, lo


def _tri(n, upper):
    r = lax.broadcasted_iota(jnp.int32, (n, n), 0)
    c = lax.broadcasted_iota(jnp.int32, (n, n), 1)
    keep = (r <= c) if upper else (c <= r)
    return jnp.where(keep, 1.0, 0.0).astype(BF16)


def _cumsum_rows(x):
    t = _tri(x.shape[0], upper=False)
    hi, mid, lo = _split3(x)
    return _dot(t, hi) + _dot(t, mid) + _dot(t, lo)


def _cumsum_cols(x):
    t = _tri(x.shape[1], upper=True)
    hi, mid, lo = _split3(x)
    return _dot(hi, t) + _dot(mid, t) + _dot(lo, t)


def _causal(n):
    r = lax.broadcasted_iota(jnp.int32, (n, n), 0)
    c = lax.broadcasted_iota(jnp.int32, (n, n), 1)
    return c <= r


def _softplus(x):
    return jnp.maximum(x, 0.0) + jnp.log1p(jnp.exp(-jnp.abs(x)))


def _sigmoid(x):
    return 1.0 / (1.0 + jnp.exp(-x))


def _silu(x):
    return x * _sigmoid(x)


def _log_sigmoid(x):
    return -_softplus(-x)


def _rms_scale(x, g):
    ms = jnp.mean(x * x, axis=-1, keepdims=True)
    return x * lax.rsqrt(ms + NORM_EPS) * g


def _norm_matmul_kernel(*refs, has_small, has_res):
    it = iter(refs)
    x_ref, g_ref, w_ref = next(it), next(it), next(it)
    ws_ref = next(it) if has_small else None
    res_ref = next(it) if has_res else None
    o_ref = next(it)
    os_ref = next(it) if has_small else None
    xn_ref = next(it)

    @pl.when(pl.program_id(1) == 0)
    def _():
        xn = _rms_scale(x_ref[...].astype(F32), g_ref[...]).astype(BF16)
        xn_ref[...] = xn
        if has_small:
            os_ref[...] = _dot(xn, ws_ref[...])

    y = _dot(xn_ref[...], w_ref[...])
    if has_res:
        y = res_ref[...] + y
    o_ref[...] = y


def norm_matmul(x, g, w, *, w_small=None, res=None, tm, tn, out_index=None, out_shape=None, name):
    n_tok, k = x.shape
    n = w.shape[1]
    assert n_tok % tm == 0 and n % tn == 0
    grid = (n_tok // tm, n // tn)
    out_index = out_index or (lambda i, j: (i, j))
    out_shape = out_shape or (n_tok, n)
    in_specs = [pl.BlockSpec((tm, k), lambda i, j: (i, 0)),
                pl.BlockSpec((1, k), lambda i, j: (0, 0)),
                pl.BlockSpec((k, tn), lambda i, j: (0, j))]
    args = [x, g.reshape(1, k).astype(F32), w]
    out_shapes = [jax.ShapeDtypeStruct(out_shape, F32)]
    out_specs = [pl.BlockSpec((tm, tn), out_index)]
    if w_small is not None:
        in_specs.append(pl.BlockSpec((k, LANES), lambda i, j: (0, 0)))
        args.append(w_small)
        out_shapes.append(jax.ShapeDtypeStruct((n_tok, LANES), F32))
        out_specs.append(pl.BlockSpec((tm, LANES), lambda i, j: (i, 0)))
    if res is not None:
        in_specs.append(pl.BlockSpec((tm, tn), lambda i, j: (i, j)))
        args.append(res)
    outs = pl.pallas_call(
        functools.partial(_norm_matmul_kernel, has_small=w_small is not None, has_res=res is not None),
        out_shape=out_shapes, grid=grid, in_specs=in_specs, out_specs=out_specs,
        scratch_shapes=[pltpu.VMEM((tm, k), BF16)],
        compiler_params=_params("parallel", "arbitrary"), name=name)(*args)
    return outs if w_small is not None else outs[0]


def _matmul_res_kernel(a_ref, w_ref, res_ref, o_ref):
    o_ref[...] = res_ref[...] + _dot(a_ref[...], w_ref[...])


def matmul_residual(a, w, res, *, tm, tn, name):
    t, k = a.shape
    n = w.shape[1]
    return pl.pallas_call(
        _matmul_res_kernel, out_shape=jax.ShapeDtypeStruct((t, n), F32),
        grid=(t // tm, n // tn),
        in_specs=[pl.BlockSpec((tm, k), lambda i, j: (i, 0)),
                  pl.BlockSpec((k, tn), lambda i, j: (0, j)),
                  pl.BlockSpec((tm, tn), lambda i, j: (i, j))],
        out_specs=pl.BlockSpec((tm, tn), lambda i, j: (i, j)),
        compiler_params=_params("parallel", "arbitrary"), name=name)(a, w, res)


def _glu_res_kernel(a_ref, wv_ref, wg_ref, res_ref, o_ref):
    a = a_ref[...]
    val = _dot(a, wv_ref[...])
    gate = _dot(a, wg_ref[...])
    o_ref[...] = res_ref[...] + val * _sigmoid(gate)


def glu_matmul_residual(a_tm, w, res, *, batch, tm, tn, name):
    s = a_tm.shape[0]
    k = w.shape[0]
    n = w.shape[1] // 2
    nt = s // tm
    return pl.pallas_call(
        _glu_res_kernel, out_shape=jax.ShapeDtypeStruct((batch * s, n), F32),
        grid=(batch * nt, n // tn),
        in_specs=[pl.BlockSpec((tm, k), lambda i, j: (i % nt, i // nt)),
                  pl.BlockSpec((k, tn), lambda i, j: (0, j)),
                  pl.BlockSpec((k, tn), lambda i, j: (0, j + n // tn)),
                  pl.BlockSpec((tm, tn), lambda i, j: (i, j))],
        out_specs=pl.BlockSpec((tm, tn), lambda i, j: (i, j)),
        compiler_params=_params("parallel", "arbitrary"), name=name)(a_tm, w, w, res)


def _mlp_kernel(x_ref, g_ref, w1_ref, w2_ref, gf_ref, o_ref, xn_ref, acc_ref, *, final_norm):
    f = pl.program_id(1)

    @pl.when(f == 0)
    def _():
        xn_ref[...] = _rms_scale(x_ref[...], g_ref[...]).astype(BF16)
        acc_ref[...] = jnp.zeros_like(acc_ref)

    h1 = jnp.maximum(_dot(xn_ref[...], w1_ref[...]), 0.0)
    acc_ref[...] += _dot((h1 * h1).astype(BF16), w2_ref[...])

    @pl.when(f == pl.num_programs(1) - 1)
    def _():
        y = x_ref[...] + acc_ref[...]
        if final_norm:
            y = _rms_scale(y, gf_ref[...])
        o_ref[...] = y


def mlp_block(x, g, w1, w2, g_final, *, final_norm, tm, tf, name):
    t, d = x.shape
    dff = w1.shape[1]
    return pl.pallas_call(
        functools.partial(_mlp_kernel, final_norm=final_norm),
        out_shape=jax.ShapeDtypeStruct((t, d), F32),
        grid=(t // tm, dff // tf),
        in_specs=[pl.BlockSpec((tm, d), lambda i, f: (i, 0)),
                  pl.BlockSpec((1, d), lambda i, f: (0, 0)),
                  pl.BlockSpec((d, tf), lambda i, f: (0, f)),
                  pl.BlockSpec((tf, d), lambda i, f: (f, 0)),
                  pl.BlockSpec((1, d), lambda i, f: (0, 0))],
        out_specs=pl.BlockSpec((tm, d), lambda i, f: (i, 0)),
        scratch_shapes=[pltpu.VMEM((tm, d), BF16), pltpu.VMEM((tm, d), F32)],
        compiler_params=_params("parallel", "arbitrary"), name=name,
    )(x, g.reshape(1, d), w1, w2, g_final.reshape(1, d))


def _causal_conv_silu(buf_ref, cur, w_ref, b_ref, first):
    l = cur.shape[0]

    @pl.when(first)
    def _():
        buf_ref[0:CONV_HALO, :] = jnp.zeros((CONV_HALO, cur.shape[1]), F32)

    buf_ref[CONV_HALO:CONV_HALO + l, :] = cur
    w = w_ref[...]
    acc = b_ref[...] + w[SSD_CONV - 1:SSD_CONV, :] * cur
    for k in range(SSD_CONV - 1):
        start = CONV_HALO - (SSD_CONV - 1) + k
        acc = acc + w[k:k + 1, :] * buf_ref[start:start + l, :]
    buf_ref[0:CONV_HALO, :] = cur[l - CONV_HALO:, :]
    return _silu(acc)


def _ssd_kernel(z_ref, x_ref, b_ref, c_ref, dt_ref, dtt_ref,
                wx_ref, wb_ref, wc_ref, bx_ref, bb_ref, bc_ref,
                dtb_ref, dtbt_ref, alog_ref, alogt_ref, dskip_ref,
                o_ref, xbuf, bbuf, cbuf, state_ref):
    first = pl.program_id(2) == 0
    l = x_ref.shape[1]

    @pl.when(first)
    def _():
        state_ref[...] = jnp.zeros_like(state_ref)

    xs = _causal_conv_silu(xbuf, x_ref[0], wx_ref, bx_ref, first)
    bm = _causal_conv_silu(bbuf, b_ref[0], wb_ref, bb_ref, first)
    cm = _causal_conv_silu(cbuf, c_ref[0], wc_ref, bc_ref, first)
    bm16 = bm.astype(BF16)
    cm16 = cm.astype(BF16)

    dt = _softplus(dt_ref[0, 0] + dtb_ref[0])
    dtt = _softplus(dtt_ref[0, 0] + dtbt_ref[0])
    la = dt * (-jnp.exp(alog_ref[0]))
    lat = dtt * (-jnp.exp(alogt_ref[0]))
    acs = _cumsum_rows(la)
    acst = _cumsum_cols(lat)
    acs_last = acs[l - 1:l, :]
    tail = jnp.exp(acs_last - acs)
    e_acs = jnp.exp(acs)
    e_last = jnp.exp(acs_last)

    cb = _dot_nt(cm16, bm16)
    causal = _causal(l)
    z = z_ref[0]
    dskip = dskip_ref[0]
    outs = []
    for j in range(SSD_HPG):
        sl = slice(j * SSD_HEAD_DIM, (j + 1) * SSD_HEAD_DIM)
        x_j = xs[:, sl]
        xdt = x_j * dt[:, j:j + 1]
        seg = acs[:, j:j + 1] - acst[j:j + 1, :]
        decay = jnp.where(causal, jnp.exp(seg), 0.0)
        y = _dot((cb * decay).astype(BF16), xdt.astype(BF16))
        st = state_ref[j]
        y = y + _dot(cm16, st.astype(BF16)) * e_acs[:, j:j + 1]
        state_ref[j] = st * e_last[:, j:j + 1] + _dot_tn(bm16, (xdt * tail[:, j:j + 1]).astype(BF16))
        y = y + dskip[:, j:j + 1] * x_j
        outs.append(y * _silu(z[:, sl]))
    o_ref[0] = jnp.concatenate(outs, axis=-1)


def ssd_core(proj, dt_raw, conv_w, conv_b, dt_bias, a_log, d_skip, *, batch, seq):
    l = SSD_CHUNK
    g_, j_ = SSD_GROUPS, SSD_HPG
    inner = g_ * SSD_GROUP_W
    nx = inner // SSD_GROUP_W
    nb = 2 * inner // SSD_STATE
    dtg = dt_raw.reshape(batch, seq, g_, j_).transpose(0, 2, 1, 3)
    dtgt = dt_raw.reshape(batch, seq, g_, j_).transpose(0, 2, 3, 1)
    cw = conv_w.astype(F32)
    cbias = conv_b.reshape(1, -1).astype(F32)

    def head_row(p):
        return p.reshape(g_, 1, j_).astype(F32)

    def head_col(p):
        return p.reshape(g_, j_, 1).astype(F32)

    head_row_spec = pl.BlockSpec((1, 1, j_), lambda b, g, c: (g, 0, 0))
    head_col_spec = pl.BlockSpec((1, j_, 1), lambda b, g, c: (g, 0, 0))
    in_specs = [
        pl.BlockSpec((1, l, SSD_GROUP_W), lambda b, g, c: (b, c, g)),
        pl.BlockSpec((1, l, SSD_GROUP_W), lambda b, g, c: (b, c, nx + g)),
        pl.BlockSpec((1, l, SSD_STATE), lambda b, g, c: (b, c, nb + g)),
        pl.BlockSpec((1, l, SSD_STATE), lambda b, g, c: (b, c, nb + g_ + g)),
        pl.BlockSpec((1, 1, l, j_), lambda b, g, c: (b, g, c, 0)),
        pl.BlockSpec((1, 1, j_, l), lambda b, g, c: (b, g, 0, c)),
        pl.BlockSpec((SSD_CONV, SSD_GROUP_W), lambda b, g, c: (0, g)),
        pl.BlockSpec((SSD_CONV, SSD_STATE), lambda b, g, c: (0, inner // SSD_STATE + g)),
        pl.BlockSpec((SSD_CONV, SSD_STATE), lambda b, g, c: (0, inner // SSD_STATE + g_ + g)),
        pl.BlockSpec((1, SSD_GROUP_W), lambda b, g, c: (0, g)),
        pl.BlockSpec((1, SSD_STATE), lambda b, g, c: (0, inner // SSD_STATE + g)),
        pl.BlockSpec((1, SSD_STATE), lambda b, g, c: (0, inner // SSD_STATE + g_ + g)),
        head_row_spec, head_col_spec, head_row_spec, head_col_spec, head_row_spec,
    ]
    return pl.pallas_call(
        _ssd_kernel, out_shape=jax.ShapeDtypeStruct((batch, seq, inner), F32),
        grid=(batch, g_, seq // l), in_specs=in_specs,
        out_specs=pl.BlockSpec((1, l, SSD_GROUP_W), lambda b, g, c: (b, c, g)),
        scratch_shapes=[pltpu.VMEM((CONV_HALO + l, SSD_GROUP_W), F32),
                        pltpu.VMEM((CONV_HALO + l, SSD_STATE), F32),
                        pltpu.VMEM((CONV_HALO + l, SSD_STATE), F32),
                        pltpu.VMEM((j_, SSD_STATE, SSD_HEAD_DIM), F32)],
        compiler_params=_params("parallel", "parallel", "arbitrary"), name="ssd_core",
    )(proj, proj, proj, proj, dtg, dtgt, cw, cw, cw, cbias, cbias, cbias,
      head_row(dt_bias), head_col(dt_bias), head_row(a_log), head_col(a_log), head_row(d_skip))


def _mlstm_kernel(q_ref, k_ref, v_ref, o_ref, gc_ref, gr_ref, gbc_ref, gbr_ref, hg_ref,
                  out_ref, c_ref, n_ref, m_ref):
    l = q_ref.shape[1]

    @pl.when(pl.program_id(2) == 0)
    def _():
        c_ref[...] = jnp.zeros_like(c_ref)
        n_ref[...] = jnp.zeros_like(n_ref)
        m_ref[...] = jnp.zeros_like(m_ref)

    q = q_ref[0] * (MLSTM_DQK ** -0.5)
    k = k_ref[0]
    v16 = v_ref[0].astype(BF16)
    q16 = q.astype(BF16)
    gcol = gc_ref[0, 0] + gbc_ref[0]
    grow = gr_ref[0, 0] + gbr_ref[0]
    i_col, lf_col = gcol[:, 0:1], _log_sigmoid(gcol[:, 1:2])
    i_row, lf_row = grow[0:1, :], _log_sigmoid(grow[1:2, :])
    bcum = _cumsum_rows(lf_col)
    bcum_row = _cumsum_cols(lf_row)
    m_st = m_ref[0:1, 0:1]

    causal = _causal(l)
    intra = jnp.where(causal, bcum - bcum_row + i_row, -jnp.inf)
    inter = bcum + m_st
    m_t = jnp.maximum(inter, jnp.max(intra, axis=1, keepdims=True))
    w = jnp.exp(intra - m_t)
    scale_inter = jnp.exp(inter - m_t)
    qk = _dot_nt(q16, k.astype(BF16)) * w
    num = _dot(qk.astype(BF16), v16) + scale_inter * _dot(q16, c_ref[...].astype(BF16))
    qn = jnp.sum(q * n_ref[...], axis=1, keepdims=True)
    den = jnp.sum(qk, axis=1, keepdims=True) + scale_inter * qn
    h = num / jnp.maximum(jnp.abs(den), jnp.exp(-m_t))

    b_last = bcum[l - 1:l, :]
    tail = b_last - bcum + i_col
    m_new = jnp.maximum(b_last + m_st, jnp.max(tail, axis=0, keepdims=True))
    carry_scale = jnp.exp(b_last + m_st - m_new)
    wk = k * jnp.exp(tail - m_new)
    c_ref[...] = carry_scale * c_ref[...] + _dot_tn(wk.astype(BF16), v16)
    n_ref[...] = carry_scale * n_ref[...] + jnp.sum(wk, axis=0, keepdims=True)
    m_ref[...] = jnp.broadcast_to(m_new, m_ref.shape)

    hn = _rms_scale(h, hg_ref[...])
    out_ref[0] = (_sigmoid(o_ref[0]) * hn).astype(out_ref.dtype)


def mlstm_core(proj, gates, gate_b, head_g, *, batch, seq):
    l = MLSTM_CHUNK
    h_ = MLSTM_HEADS
    qk_w = h_ * MLSTM_DQK
    v_w = h_ * MLSTM_DV
    gsplit = jnp.stack([gates[..., :h_], gates[..., h_:2 * h_]], axis=-1)
    gcol = gsplit.transpose(0, 2, 1, 3)
    grow = gsplit.transpose(0, 2, 3, 1)
    gb = jnp.stack([gate_b[:h_], gate_b[h_:]], axis=-1).astype(F32)
    vo = 2 * qk_w // MLSTM_DV
    return pl.pallas_call(
        _mlstm_kernel, out_shape=jax.ShapeDtypeStruct((batch, seq, v_w), BF16),
        grid=(batch, h_, seq // l),
        in_specs=[pl.BlockSpec((1, l, MLSTM_DQK), lambda b, h, c: (b, c, h)),
                  pl.BlockSpec((1, l, MLSTM_DQK), lambda b, h, c: (b, c, h_ + h)),
                  pl.BlockSpec((1, l, MLSTM_DV), lambda b, h, c: (b, c, vo + h)),
                  pl.BlockSpec((1, l, MLSTM_DV), lambda b, h, c: (b, c, vo + h_ + h)),
                  pl.BlockSpec((1, 1, l, 2), lambda b, h, c: (b, h, c, 0)),
                  pl.BlockSpec((1, 1, 2, l), lambda b, h, c: (b, h, 0, c)),
                  pl.BlockSpec((1, 1, 2), lambda b, h, c: (h, 0, 0)),
                  pl.BlockSpec((1, 2, 1), lambda b, h, c: (h, 0, 0)),
                  pl.BlockSpec((1, MLSTM_DV), lambda b, h, c: (0, h))],
        out_specs=pl.BlockSpec((1, l, MLSTM_DV), lambda b, h, c: (b, c, h)),
        scratch_shapes=[pltpu.VMEM((MLSTM_DQK, MLSTM_DV), F32),
                        pltpu.VMEM((1, MLSTM_DQK), F32),
                        pltpu.VMEM((8, LANES), F32)],
        compiler_params=_params("parallel", "parallel", "arbitrary"), name="mlstm_core",
    )(proj, proj, proj, proj, gcol, grow, gb.reshape(h_, 1, 2), gb.reshape(h_, 2, 1),
      head_g.reshape(1, v_w).astype(F32))


def _s5_discretize_kernel(logdt_ref, are_ref, aim_ref, bre_ref, bim_ref,
                          abre_ref, abim_ref, bbre_ref, bbim_ref):
    ar, ai = are_ref[...], aim_ref[...]
    dt = jnp.exp(logdt_ref[...])
    mag = jnp.exp(ar * dt)
    abar_re, abar_im = mag * jnp.cos(ai * dt), mag * jnp.sin(ai * dt)
    den = ar * ar + ai * ai
    zoh_re = ((abar_re - 1.0) * ar + abar_im * ai) / den
    zoh_im = (abar_im * ar - (abar_re - 1.0) * ai) / den
    abre_ref[...] = abar_re
    abim_ref[...] = abar_im
    for c in range(S5_GROUP):
        br, bi = bre_ref[c], bim_ref[c]
        bbre_ref[c] = zoh_re * br - zoh_im * bi
        bbim_ref[c] = zoh_re * bi + zoh_im * br


def s5_discretize(log_dt, a_re, a_im, b_re, b_im):
    g_, p_ = a_re.shape
    sds = jax.ShapeDtypeStruct
    return pl.pallas_call(
        _s5_discretize_kernel,
        out_shape=[sds((g_, p_), F32), sds((g_, p_), F32),
                   sds((S5_GROUP, g_, p_), F32), sds((S5_GROUP, g_, p_), F32)],
        name="s5_discretize",
    )(log_dt.reshape(g_, 1).astype(F32), a_re.astype(F32), a_im.astype(F32),
      b_re.astype(F32).transpose(2, 0, 1), b_im.astype(F32).transpose(2, 0, 1))


def _s5_kernel(u_ref, bw_ref, cw_ref, a_ref, d_ref, y_ref, st_ref, bu_ref, *, batch):
    ts = u_ref.shape[0]
    half = S5_BLOCK_STATE

    @pl.when(pl.program_id(1) == 0)
    def _():
        st_ref[...] = jnp.zeros_like(st_ref)

    u = u_ref[...].reshape(ts * batch, LANES)
    bu_ref[...] = _dot(u.astype(BF16), bw_ref[0])
    a = a_ref[0]
    a_re = jnp.broadcast_to(a[:, :half], (batch, half))
    a_im = jnp.broadcast_to(a[:, half:], (batch, half))

    def step(t, carry):
        s_re, s_im = carry
        rows = pl.ds(pl.multiple_of(t * batch, batch), batch)
        n_re = a_re * s_re - a_im * s_im + bu_ref[rows, 0:half]
        n_im = a_re * s_im + a_im * s_re + bu_ref[rows, half:2 * half]
        bu_ref[rows, 0:half] = n_re
        bu_ref[rows, half:2 * half] = n_im
        return n_re, n_im

    s_re, s_im = lax.fori_loop(0, ts, step, (st_ref[:, 0:half], st_ref[:, half:2 * half]))
    st_ref[:, 0:half] = s_re
    st_ref[:, half:2 * half] = s_im

    y = _dot(bu_ref[...].astype(BF16), cw_ref[0]) + d_ref[...] * u
    y_ref[...] = jax.nn.gelu(y).astype(y_ref.dtype).reshape(ts, batch, LANES)


def s5_core(u_tm, bw, cw, a_vec, d_skip, *, batch, seq):
    width = u_tm.shape[2]
    nblk = width // LANES
    ts = S5_TIME_BLOCK
    return pl.pallas_call(
        functools.partial(_s5_kernel, batch=batch),
        out_shape=jax.ShapeDtypeStruct((seq, batch, width), BF16),
        grid=(nblk, seq // ts),
        in_specs=[pl.BlockSpec((ts, batch, LANES), lambda j, i: (i, 0, j)),
                  pl.BlockSpec((1, LANES, 2 * S5_BLOCK_STATE), lambda j, i: (j, 0, 0)),
                  pl.BlockSpec((1, 2 * S5_BLOCK_STATE, LANES), lambda j, i: (j, 0, 0)),
                  pl.BlockSpec((1, 1, 2 * S5_BLOCK_STATE), lambda j, i: (j, 0, 0)),
                  pl.BlockSpec((1, LANES), lambda j, i: (0, j))],
        out_specs=pl.BlockSpec((ts, batch, LANES), lambda j, i: (i, 0, j)),
        scratch_shapes=[pltpu.VMEM((batch, 2 * S5_BLOCK_STATE), F32),
                        pltpu.VMEM((ts * batch, 2 * S5_BLOCK_STATE), F32)],
        compiler_params=_params("parallel", "arbitrary"), name="s5_core",
    )(u_tm, bw, cw, a_vec, d_skip.reshape(1, width).astype(F32))


def _s5_block_weights(abar_re, abar_im, bbar_re, bbar_im, c_re, c_im):
    g_, p_ = abar_re.shape
    gb = S5_GROUPS_PER_BLOCK
    nblk = g_ // gb
    eye = jnp.eye(gb, dtype=F32)

    def in_blocks(bbar):
        b = bbar.reshape(S5_GROUP, nblk, gb, p_).transpose(1, 2, 0, 3)
        return jnp.einsum("ngcp,gh->ngchp", b, eye).reshape(nblk, gb * S5_GROUP, gb * p_)

    def out_blocks(cmat):
        c = cmat.reshape(nblk, gb, S5_GROUP, p_).transpose(0, 1, 3, 2)
        return jnp.einsum("ngpc,gh->ngphc", c, eye).reshape(nblk, gb * p_, gb * S5_GROUP)

    bw = jnp.concatenate([in_blocks(bbar_re), in_blocks(bbar_im)], axis=2).astype(BF16)
    cw = jnp.concatenate([out_blocks(c_re.astype(F32)), out_blocks(-c_im.astype(F32))], axis=1).astype(BF16)
    a_vec = jnp.concatenate([abar_re.reshape(nblk, 1, gb * p_), abar_im.reshape(nblk, 1, gb * p_)], axis=2)
    return bw, cw, a_vec


TOKEN_TILE = 1024
COL_TILE = 1024
MLP_TOKEN_TILE = 512
MLP_FF_TILE = 512


def _pad_cols(w, width):
    return jnp.pad(w, ((0, 0), (0, width - w.shape[1])))


def _ssd_layer(h, g, w_in, conv_w, conv_b, dt_bias, a_log, d_skip, norm_g, w_out, *, batch, seq):
    inner = SSD_GROUPS * SSD_GROUP_W
    main = 2 * inner + 2 * SSD_GROUPS * SSD_STATE
    n_heads = SSD_GROUPS * SSD_HPG
    w16 = w_in.astype(BF16)
    proj, dtp = norm_matmul(h, g, w16[:, :main], w_small=_pad_cols(w16[:, main:], LANES),
                            tm=TOKEN_TILE, tn=COL_TILE, name="ssd_in_proj")
    y = ssd_core(proj.reshape(batch, seq, main), dtp[:, :n_heads].reshape(batch, seq, n_heads),
                 conv_w, conv_b, dt_bias, a_log, d_skip, batch=batch, seq=seq)
    return norm_matmul(y.reshape(batch * seq, inner), norm_g, w_out.astype(BF16), res=h,
                       tm=TOKEN_TILE // 2, tn=COL_TILE, name="ssd_out_proj")


def _mlstm_layer(h, g, w_in, gate_b, head_g, w_out, *, batch, seq):
    main = 2 * MLSTM_HEADS * (MLSTM_DQK + MLSTM_DV)
    w16 = w_in.astype(BF16)
    proj, gates = norm_matmul(h, g, w16[:, :main], w_small=_pad_cols(w16[:, main:], LANES),
                              tm=TOKEN_TILE, tn=COL_TILE, name="mlstm_in_proj")
    hs = mlstm_core(proj.reshape(batch, seq, main),
                    gates[:, :2 * MLSTM_HEADS].reshape(batch, seq, 2 * MLSTM_HEADS),
                    gate_b, head_g, batch=batch, seq=seq)
    return matmul_residual(hs.reshape(batch * seq, -1), w_out.astype(BF16), h,
                           tm=TOKEN_TILE, tn=COL_TILE, name="mlstm_out_proj")


def _s5_layer(h, g, w_in, b_re, b_im, c_re, c_im, d_skip, log_dt, a_re, a_im, w_out, *, batch, seq):
    d = h.shape[1]
    width = w_in.shape[1]
    tm = min(TOKEN_TILE, seq)
    nt = seq // tm
    ncol = width // COL_TILE
    u_tm = norm_matmul(h, g, w_in.astype(BF16), tm=tm, tn=COL_TILE,
                       out_index=lambda i, j: (i % nt, (i // nt) * ncol + j),
                       out_shape=(seq, batch * width), name="s5_in_proj")
    abar_re, abar_im, bbar_re, bbar_im = s5_discretize(log_dt, a_re, a_im, b_re, b_im)
    bw, cw, a_vec = _s5_block_weights(abar_re, abar_im, bbar_re, bbar_im, c_re, c_im)
    y_tm = s5_core(u_tm.reshape(seq, batch, width), bw, cw, a_vec, d_skip, batch=batch, seq=seq)
    return glu_matmul_residual(y_tm.reshape(seq, batch * width), w_out.astype(BF16), h,
                               batch=batch, tm=tm, tn=COL_TILE, name="s5_out_proj")


def kernel(x, norm_mix_g, norm_mlp_g, ssd_w_in, ssd_conv_w, ssd_conv_b, ssd_dt_bias, ssd_a_log, ssd_d, ssd_norm_g, ssd_w_out, mlstm_w_in, mlstm_gate_b, mlstm_head_g, mlstm_w_out, s5_w_in, s5_b_re, s5_b_im, s5_c_re, s5_c_im, s5_d, s5_log_dt, s5_a_re, s5_a_im, s5_w_out, mlp_w1, mlp_w2, final_norm_g):
    batch, seq, d = x.shape
    depth = norm_mix_g.shape[0]
    h = x.reshape(batch * seq, d)
    for layer in range(depth):
        kind, idx = layer % 3, layer // 3
        g = norm_mix_g[layer]
        if kind == 0:
            h = _ssd_layer(h, g, ssd_w_in[idx], ssd_conv_w[idx], ssd_conv_b[idx], ssd_dt_bias[idx],
                           ssd_a_log[idx], ssd_d[idx], ssd_norm_g[idx], ssd_w_out[idx], batch=batch, seq=seq)
        elif kind == 1:
            h = _mlstm_layer(h, g, mlstm_w_in[idx], mlstm_gate_b[idx], mlstm_head_g[idx], mlstm_w_out[idx],
                             batch=batch, seq=seq)
        else:
            h = _s5_layer(h, g, s5_w_in[idx], s5_b_re[idx], s5_b_im[idx], s5_c_re[idx], s5_c_im[idx],
                          s5_d[idx], s5_log_dt[idx], s5_a_re[idx], s5_a_im[idx], s5_w_out[idx],
                          batch=batch, seq=seq)
        h = mlp_block(h, norm_mlp_g[layer], mlp_w1[layer].astype(BF16), mlp_w2[layer].astype(BF16),
                      final_norm_g, final_norm=(layer == depth - 1),
                      tm=min(MLP_TOKEN_TILE, batch * seq), tf=MLP_FF_TILE, name=f"mlp_{layer}")
    return h.reshape(batch, seq, d)
```

```python
import functools
import math

import jax
import jax.numpy as jnp
from jax import lax
from jax.experimental import pallas as pl
from jax.experimental.pallas import tpu as pltpu

F32 = jnp.float32
BF16 = jnp.bfloat16

NORM_EPS = 1e-5
LOG2E = math.log2(math.e)
LANES = 128
VMEM_LIMIT_BYTES = 56 * 1024 * 1024

SSD_HEAD_DIM = 64
SSD_GROUPS = 8
SSD_HPG = 8
SSD_STATE = 128
SSD_CONV = 4
SSD_GROUP_W = SSD_HPG * SSD_HEAD_DIM
SSD_CHUNK = 256
CONV_HALO = 8

MLSTM_HEADS = 4
MLSTM_DQK = 256
MLSTM_DV = 512
MLSTM_CHUNK = 256

S5_GROUP = 16
S5_STATE = 64
S5_GROUPS_PER_BLOCK = LANES // S5_GROUP
S5_BLOCK_STATE = S5_GROUPS_PER_BLOCK * S5_STATE
S5_TIME_BLOCK = 128


def _params(*semantics):
    return pltpu.CompilerParams(dimension_semantics=semantics, vmem_limit_bytes=VMEM_LIMIT_BYTES)


def _dot(a, b):
    return jnp.dot(a, b, preferred_element_type=F32)


def _dot_nt(a, b):
    return lax.dot_general(a, b, (((1,), (1,)), ((), ())), preferred_element_type=F32)


def _dot_tn(a, b):
    return lax.dot_general(a, b, (((0,), (0,)), ((), ())), preferred_element_type=F32)


def _split3(x):
    hi = x.astype(BF16)
    r1 = x - hi.astype(F32)
    mid = r1.astype(BF16)
    lo = (r1 - mid.astype(F32)).astype(BF16)
    return hi, mid, lo


def _tri(n, upper):
    r = lax.broadcasted_iota(jnp.int32, (n, n), 0)
    c = lax.broadcasted_iota(jnp.int32, (n, n), 1)
    keep = (r <= c) if upper else (c <= r)
    return jnp.where(keep, 1.0, 0.0).astype(BF16)


def _cumsum_rows(x):
    t = _tri(x.shape[0], upper=False)
    hi, mid, lo = _split3(x)
    return _dot(t, hi) + _dot(t, mid) + _dot(t, lo)


def _cumsum_cols(x):
    t = _tri(x.shape[1], upper=True)
    hi, mid, lo = _split3(x)
    return _dot(hi, t) + _dot(mid, t) + _dot(lo, t)


def _causal(n):
    r = lax.broadcasted_iota(jnp.int32, (n, n), 0)
    c = lax.broadcasted_iota(jnp.int32, (n, n), 1)
    return c <= r


def _softplus(x):
    return jnp.maximum(x, 0.0) + jnp.log1p(jnp.exp(-jnp.abs(x)))


def _sigmoid(x):
    return 1.0 / (1.0 + jnp.exp(-x))


def _silu(x):
    return x * _sigmoid(x)


def _log_sigmoid(x):
    return -_softplus(-x)


def _rms_scale(x, g):
    ms = jnp.mean(x * x, axis=-1, keepdims=True)
    return x * lax.rsqrt(ms + NORM_EPS) * g


def _norm_matmul_kernel(*refs, has_small, has_res):
    it = iter(refs)
    x_ref, g_ref, w_ref = next(it), next(it), next(it)
    ws_ref = next(it) if has_small else None
    res_ref = next(it) if has_res else None
    o_ref = next(it)
    os_ref = next(it) if has_small else None
    xn_ref = next(it)

    @pl.when(pl.program_id(1) == 0)
    def _():
        xn = _rms_scale(x_ref[...].astype(F32), g_ref[...]).astype(BF16)
        xn_ref[...] = xn
        if has_small:
            os_ref[...] = _dot(xn, ws_ref[...])

    y = _dot(xn_ref[...], w_ref[...])
    if has_res:
        y = res_ref[...] + y
    o_ref[...] = y


def norm_matmul(x, g, w, *, w_small=None, res=None, tm, tn, out_index=None, out_shape=None, name):
    n_tok, k = x.shape
    n = w.shape[1]
    assert n_tok % tm == 0 and n % tn == 0
    grid = (n_tok // tm, n // tn)
    out_index = out_index or (lambda i, j: (i, j))
    out_shape = out_shape or (n_tok, n)
    in_specs = [pl.BlockSpec((tm, k), lambda i, j: (i, 0)),
                pl.BlockSpec((1, k), lambda i, j: (0, 0)),
                pl.BlockSpec((k, tn), lambda i, j: (0, j))]
    args = [x, g.reshape(1, k).astype(F32), w]
    out_shapes = [jax.ShapeDtypeStruct(out_shape, F32)]
    out_specs = [pl.BlockSpec((tm, tn), out_index)]
    if w_small is not None:
        in_specs.append(pl.BlockSpec((k, LANES), lambda i, j: (0, 0)))
        args.append(w_small)
        out_shapes.append(jax.ShapeDtypeStruct((n_tok, LANES), F32))
        out_specs.append(pl.BlockSpec((tm, LANES), lambda i, j: (i, 0)))
    if res is not None:
        in_specs.append(pl.BlockSpec((tm, tn), lambda i, j: (i, j)))
        args.append(res)
    outs = pl.pallas_call(
        functools.partial(_norm_matmul_kernel, has_small=w_small is not None, has_res=res is not None),
        out_shape=out_shapes, grid=grid, in_specs=in_specs, out_specs=out_specs,
        scratch_shapes=[pltpu.VMEM((tm, k), BF16)],
        compiler_params=_params("parallel", "arbitrary"), name=name)(*args)
    return outs if w_small is not None else outs[0]


def _matmul_res_kernel(a_ref, w_ref, res_ref, o_ref):
    o_ref[...] = res_ref[...] + _dot(a_ref[...], w_ref[...])


def matmul_residual(a, w, res, *, tm, tn, name):
    t, k = a.shape
    n = w.shape[1]
    return pl.pallas_call(
        _matmul_res_kernel, out_shape=jax.ShapeDtypeStruct((t, n), F32),
        grid=(t // tm, n // tn),
        in_specs=[pl.BlockSpec((tm, k), lambda i, j: (i, 0)),
                  pl.BlockSpec((k, tn), lambda i, j: (0, j)),
                  pl.BlockSpec((tm, tn), lambda i, j: (i, j))],
        out_specs=pl.BlockSpec((tm, tn), lambda i, j: (i, j)),
        compiler_params=_params("parallel", "arbitrary"), name=name)(a, w, res)


def _glu_res_kernel(a_ref, wv_ref, wg_ref, res_ref, o_ref):
    a = a_ref[...]
    val = _dot(a, wv_ref[...])
    gate = _dot(a, wg_ref[...])
    o_ref[...] = res_ref[...] + val * _sigmoid(gate)


def glu_matmul_residual(a_tm, w, res, *, batch, tm, tn, name):
    s = a_tm.shape[0]
    k = w.shape[0]
    n = w.shape[1] // 2
    nt = s // tm
    return pl.pallas_call(
        _glu_res_kernel, out_shape=jax.ShapeDtypeStruct((batch * s, n), F32),
        grid=(batch * nt, n // tn),
        in_specs=[pl.BlockSpec((tm, k), lambda i, j: (i % nt, i // nt)),
                  pl.BlockSpec((k, tn), lambda i, j: (0, j)),
                  pl.BlockSpec((k, tn), lambda i, j: (0, j + n // tn)),
                  pl.BlockSpec((tm, tn), lambda i, j: (i, j))],
        out_specs=pl.BlockSpec((tm, tn), lambda i, j: (i, j)),
        compiler_params=_params("parallel", "arbitrary"), name=name)(a_tm, w, w, res)


def _mlp_kernel(x_ref, g_ref, w1_ref, w2_ref, gf_ref, o_ref, xn_ref, acc_ref, *, final_norm):
    f = pl.program_id(1)

    @pl.when(f == 0)
    def _():
        xn_ref[...] = _rms_scale(x_ref[...], g_ref[...]).astype(BF16)
        acc_ref[...] = jnp.zeros_like(acc_ref)

    h1 = jnp.maximum(_dot(xn_ref[...], w1_ref[...]), 0.0)
    acc_ref[...] += _dot((h1 * h1).astype(BF16), w2_ref[...])

    @pl.when(f == pl.num_programs(1) - 1)
    def _():
        y = x_ref[...] + acc_ref[...]
        if final_norm:
            y = _rms_scale(y, gf_ref[...])
        o_ref[...] = y


def mlp_block(x, g, w1, w2, g_final, *, final_norm, tm, tf, name):
    t, d = x.shape
    dff = w1.shape[1]
    return pl.pallas_call(
        functools.partial(_mlp_kernel, final_norm=final_norm),
        out_shape=jax.ShapeDtypeStruct((t, d), F32),
        grid=(t // tm, dff // tf),
        in_specs=[pl.BlockSpec((tm, d), lambda i, f: (i, 0)),
                  pl.BlockSpec((1, d), lambda i, f: (0, 0)),
                  pl.BlockSpec((d, tf), lambda i, f: (0, f)),
                  pl.BlockSpec((tf, d), lambda i, f: (f, 0)),
                  pl.BlockSpec((1, d), lambda i, f: (0, 0))],
        out_specs=pl.BlockSpec((tm, d), lambda i, f: (i, 0)),
        scratch_shapes=[pltpu.VMEM((tm, d), BF16), pltpu.VMEM((tm, d), F32)],
        compiler_params=_params("parallel", "arbitrary"), name=name,
    )(x, g.reshape(1, d), w1, w2, g_final.reshape(1, d))


def _causal_conv_silu(buf_ref, cur, w_ref, b_ref, first):
    l = cur.shape[0]

    @pl.when(first)
    def _():
        buf_ref[0:CONV_HALO, :] = jnp.zeros((CONV_HALO, cur.shape[1]), F32)

    buf_ref[CONV_HALO:CONV_HALO + l, :] = cur
    w = w_ref[...]
    acc = b_ref[...] + w[SSD_CONV - 1:SSD_CONV, :] * cur
    for k in range(SSD_CONV - 1):
        start = CONV_HALO - (SSD_CONV - 1) + k
        acc = acc + w[k:k + 1, :] * buf_ref[start:start + l, :]
    buf_ref[0:CONV_HALO, :] = cur[l - CONV_HALO:, :]
    return _silu(acc)


def _dot3(x, r):
    hi, mid, lo = _split3(x)
    return _dot(hi, r) + _dot(mid, r) + _dot(lo, r)


def _ssd_kernel(z_ref, x_ref, b_ref, c_ref, dt_ref, dtt_ref,
                wx_ref, wb_ref, wc_ref, bx_ref, bb_ref, bc_ref,
                dtb_ref, dtbt_ref, alog_ref, alogt_ref, dskip_ref, ra_ref, rd_ref,
                o_ref, xbuf, bbuf, cbuf, state_ref):
    first = pl.program_id(2) == 0
    l = x_ref.shape[1]
    sub = LANES
    nsub = l // sub
    hw = SSD_HEAD_DIM

    @pl.when(first)
    def _():
        state_ref[...] = jnp.zeros_like(state_ref)

    xs = _causal_conv_silu(xbuf, x_ref[0], wx_ref, bx_ref, first)
    bm16 = _causal_conv_silu(bbuf, b_ref[0], wb_ref, bb_ref, first).astype(BF16)
    cm16 = _causal_conv_silu(cbuf, c_ref[0], wc_ref, bc_ref, first).astype(BF16)

    dt = _softplus(dt_ref[0, 0] + dtb_ref[0])
    dtt = _softplus(dtt_ref[0, 0] + dtbt_ref[0])
    acs = _cumsum_rows(dt * (-LOG2E * jnp.exp(alog_ref[0])))
    acst = _cumsum_cols(dtt * (-LOG2E * jnp.exp(alogt_ref[0])))
    acs_wide = _dot3(acs, ra_ref[...])
    acs_blk = acs_wide[:, :SSD_HPG * sub]
    acs64 = acs_wide[:, SSD_HPG * sub:]
    dt64 = _dot3(dt, rd_ref[...])
    last64 = acs64[l - 1:l, :]

    xdt = xs * dt64
    xdt16 = xdt.astype(BF16)
    st = state_ref[...]
    y = _dot(cm16, st.astype(BF16)) * jnp.exp2(acs64)
    state_ref[...] = st * jnp.exp2(last64) + _dot_tn(bm16, (xdt * jnp.exp2(last64 - acs64)).astype(BF16))
    y = y + dskip_ref[0] * xs

    cb = _dot_nt(cm16, bm16)
    diag_mask = _causal(sub)
    cb_blk = [[cb[r * sub:(r + 1) * sub, c * sub:(c + 1) * sub] if c < r else
               jnp.where(diag_mask, cb[r * sub:(r + 1) * sub, c * sub:(c + 1) * sub], 0.0)
               for c in range(r + 1)] for r in range(nsub)]
    lane = lax.broadcasted_iota(jnp.int32, (l, sub), 1)
    pair_out = []
    for p in range(SSD_HPG // 2):
        xp = xdt16[:, p * sub:(p + 1) * sub]
        rhs = [jnp.where(lane < hw, xp, jnp.zeros_like(xp)), jnp.where(lane >= hw, xp, jnp.zeros_like(xp))]
        rows = []
        for r in range(nsub):
            lhs_parts, rhs_parts = [], []
            for h in range(2):
                j = 2 * p + h
                col = acs_blk[r * sub:(r + 1) * sub, j * sub:(j + 1) * sub]
                for c in range(r + 1):
                    seg = col - acst[j:j + 1, c * sub:(c + 1) * sub]
                    if c == r:
                        seg = jnp.minimum(seg, 0.0)
                    lhs_parts.append((cb_blk[r][c] * jnp.exp2(seg)).astype(BF16))
                rhs_parts.append(rhs[h][:(r + 1) * sub, :])
            rows.append(_dot(jnp.concatenate(lhs_parts, axis=1), jnp.concatenate(rhs_parts, axis=0)))
        pair_out.append(jnp.concatenate(rows, axis=0))
    y = y + jnp.concatenate(pair_out, axis=1)
    o_ref[0] = y * _silu(z_ref[0])


def _head_expand(lanes_per_head, n_heads):
    return jnp.repeat(jnp.eye(n_heads, dtype=F32), lanes_per_head, axis=1)


def ssd_core(proj, dt_raw, conv_w, conv_b, dt_bias, a_log, d_skip, *, batch, seq):
    l = min(SSD_CHUNK, seq)
    g_, j_ = SSD_GROUPS, SSD_HPG
    inner = g_ * SSD_GROUP_W
    nx = inner // SSD_GROUP_W
    nb = 2 * inner // SSD_STATE
    dtg = dt_raw.reshape(batch, seq, g_, j_).transpose(0, 2, 1, 3)
    dtgt = dt_raw.reshape(batch, seq, g_, j_).transpose(0, 2, 3, 1)
    cw = conv_w.astype(F32)
    cbias = conv_b.reshape(1, -1).astype(F32)
    expand64 = _head_expand(SSD_HEAD_DIM, j_)
    ra = jnp.concatenate([_head_expand(LANES, j_), expand64], axis=1).astype(BF16)
    rd = expand64.astype(BF16)
    dskip64 = jnp.repeat(d_skip.reshape(g_, 1, j_).astype(F32), SSD_HEAD_DIM, axis=2)

    def head_row(p):
        return p.reshape(g_, 1, j_).astype(F32)

    def head_col(p):
        return p.reshape(g_, j_, 1).astype(F32)

    head_row_spec = pl.BlockSpec((1, 1, j_), lambda b, g, c: (g, 0, 0))
    head_col_spec = pl.BlockSpec((1, j_, 1), lambda b, g, c: (g, 0, 0))
    in_specs = [
        pl.BlockSpec((1, l, SSD_GROUP_W), lambda b, g, c: (b, c, g)),
        pl.BlockSpec((1, l, SSD_GROUP_W), lambda b, g, c: (b, c, nx + g)),
        pl.BlockSpec((1, l, SSD_STATE), lambda b, g, c: (b, c, nb + g)),
        pl.BlockSpec((1, l, SSD_STATE), lambda b, g, c: (b, c, nb + g_ + g)),
        pl.BlockSpec((1, 1, l, j_), lambda b, g, c: (b, g, c, 0)),
        pl.BlockSpec((1, 1, j_, l), lambda b, g, c: (b, g, 0, c)),
        pl.BlockSpec((SSD_CONV, SSD_GROUP_W), lambda b, g, c: (0, g)),
        pl.BlockSpec((SSD_CONV, SSD_STATE), lambda b, g, c: (0, inner // SSD_STATE + g)),
        pl.BlockSpec((SSD_CONV, SSD_STATE), lambda b, g, c: (0, inner // SSD_STATE + g_ + g)),
        pl.BlockSpec((1, SSD_GROUP_W), lambda b, g, c: (0, g)),
        pl.BlockSpec((1, SSD_STATE), lambda b, g, c: (0, inner // SSD_STATE + g)),
        pl.BlockSpec((1, SSD_STATE), lambda b, g, c: (0, inner // SSD_STATE + g_ + g)),
        head_row_spec, head_col_spec, head_row_spec, head_col_spec,
        pl.BlockSpec((1, 1, SSD_GROUP_W), lambda b, g, c: (g, 0, 0)),
        pl.BlockSpec(ra.shape, lambda b, g, c: (0, 0)),
        pl.BlockSpec(rd.shape, lambda b, g, c: (0, 0)),
    ]
    return pl.pallas_call(
        _ssd_kernel, out_shape=jax.ShapeDtypeStruct((batch, seq, inner), F32),
        grid=(batch, g_, seq // l), in_specs=in_specs,
        out_specs=pl.BlockSpec((1, l, SSD_GROUP_W), lambda b, g, c: (b, c, g)),
        scratch_shapes=[pltpu.VMEM((CONV_HALO + l, SSD_GROUP_W), F32),
                        pltpu.VMEM((CONV_HALO + l, SSD_STATE), F32),
                        pltpu.VMEM((CONV_HALO + l, SSD_STATE), F32),
                        pltpu.VMEM((SSD_STATE, SSD_GROUP_W), F32)],
        compiler_params=_params("parallel", "parallel", "arbitrary"), name="ssd_core",
    )(proj, proj, proj, proj, dtg, dtgt, cw, cw, cw, cbias, cbias, cbias,
      head_row(dt_bias), head_col(dt_bias), head_row(a_log), head_col(a_log), dskip64, ra, rd)


def _mlstm_kernel(q_ref, k_ref, v_ref, o_ref, gc_ref, gr_ref, gbc_ref, gbr_ref, hg_ref,
                  out_ref, c_ref, n_ref, m_ref):
    l = q_ref.shape[1]

    @pl.when(pl.program_id(2) == 0)
    def _():
        c_ref[...] = jnp.zeros_like(c_ref)
        n_ref[...] = jnp.zeros_like(n_ref)
        m_ref[...] = jnp.zeros_like(m_ref)

    q = q_ref[0] * (MLSTM_DQK ** -0.5)
    k = k_ref[0]
    v16 = v_ref[0].astype(BF16)
    q16 = q.astype(BF16)
    gcol = gc_ref[0, 0] + gbc_ref[0]
    grow = gr_ref[0, 0] + gbr_ref[0]
    i_col, lf_col = gcol[:, 0:1], _log_sigmoid(gcol[:, 1:2])
    i_row, lf_row = grow[0:1, :], _log_sigmoid(grow[1:2, :])
    bcum = _cumsum_rows(lf_col)
    bcum_row = _cumsum_cols(lf_row)
    m_st = m_ref[0:1, 0:1]

    causal = _causal(l)
    intra = jnp.where(causal, bcum - bcum_row + i_row, -jnp.inf)
    inter = bcum + m_st
    m_t = jnp.maximum(inter, jnp.max(intra, axis=1, keepdims=True))
    w = jnp.exp(intra - m_t)
    scale_inter = jnp.exp(inter - m_t)
    qk = _dot_nt(q16, k.astype(BF16)) * w
    num = _dot(qk.astype(BF16), v16) + scale_inter * _dot(q16, c_ref[...].astype(BF16))
    qn = jnp.sum(q * n_ref[...], axis=1, keepdims=True)
    den = jnp.sum(qk, axis=1, keepdims=True) + scale_inter * qn
    h = num / jnp.maximum(jnp.abs(den), jnp.exp(-m_t))

    b_last = bcum[l - 1:l, :]
    tail = b_last - bcum + i_col
    m_new = jnp.maximum(b_last + m_st, jnp.max(tail, axis=0, keepdims=True))
    carry_scale = jnp.exp(b_last + m_st - m_new)
    wk = k * jnp.exp(tail - m_new)
    c_ref[...] = carry_scale * c_ref[...] + _dot_tn(wk.astype(BF16), v16)
    n_ref[...] = carry_scale * n_ref[...] + jnp.sum(wk, axis=0, keepdims=True)
    m_ref[...] = jnp.broadcast_to(m_new, m_ref.shape)

    hn = _rms_scale(h, hg_ref[...])
    out_ref[0] = (_sigmoid(o_ref[0]) * hn).astype(out_ref.dtype)


def mlstm_core(proj, gates, gate_b, head_g, *, batch, seq):
    l = MLSTM_CHUNK
    h_ = MLSTM_HEADS
    qk_w = h_ * MLSTM_DQK
    v_w = h_ * MLSTM_DV
    gsplit = jnp.stack([gates[..., :h_], gates[..., h_:2 * h_]], axis=-1)
    gcol = gsplit.transpose(0, 2, 1, 3)
    grow = gsplit.transpose(0, 2, 3, 1)
    gb = jnp.stack([gate_b[:h_], gate_b[h_:]], axis=-1).astype(F32)
    vo = 2 * qk_w // MLSTM_DV
    return pl.pallas_call(
        _mlstm_kernel, out_shape=jax.ShapeDtypeStruct((batch, seq, v_w), BF16),
        grid=(batch, h_, seq // l),
        in_specs=[pl.BlockSpec((1, l, MLSTM_DQK), lambda b, h, c: (b, c, h)),
                  pl.BlockSpec((1, l, MLSTM_DQK), lambda b, h, c: (b, c, h_ + h)),
                  pl.BlockSpec((1, l, MLSTM_DV), lambda b, h, c: (b, c, vo + h)),
                  pl.BlockSpec((1, l, MLSTM_DV), lambda b, h, c: (b, c, vo + h_ + h)),
                  pl.BlockSpec((1, 1, l, 2), lambda b, h, c: (b, h, c, 0)),
                  pl.BlockSpec((1, 1, 2, l), lambda b, h, c: (b, h, 0, c)),
                  pl.BlockSpec((1, 1, 2), lambda b, h, c: (h, 0, 0)),
                  pl.BlockSpec((1, 2, 1), lambda b, h, c: (h, 0, 0)),
                  pl.BlockSpec((1, MLSTM_DV), lambda b, h, c: (0, h))],
        out_specs=pl.BlockSpec((1, l, MLSTM_DV), lambda b, h, c: (b, c, h)),
        scratch_shapes=[pltpu.VMEM((MLSTM_DQK, MLSTM_DV), F32),
                        pltpu.VMEM((1, MLSTM_DQK), F32),
                        pltpu.VMEM((8, LANES), F32)],
        compiler_params=_params("parallel", "parallel", "arbitrary"), name="mlstm_core",
    )(proj, proj, proj, proj, gcol, grow, gb.reshape(h_, 1, 2), gb.reshape(h_, 2, 1),
      head_g.reshape(1, v_w).astype(F32))


def _s5_discretize_kernel(logdt_ref, are_ref, aim_ref, bre_ref, bim_ref,
                          abre_ref, abim_ref, bbre_ref, bbim_ref):
    ar, ai = are_ref[...], aim_ref[...]
    dt = jnp.exp(logdt_ref[...])
    mag = jnp.exp(ar * dt)
    abar_re, abar_im = mag * jnp.cos(ai * dt), mag * jnp.sin(ai * dt)
    den = ar * ar + ai * ai
    zoh_re = ((abar_re - 1.0) * ar + abar_im * ai) / den
    zoh_im = (abar_im * ar - (abar_re - 1.0) * ai) / den
    abre_ref[...] = abar_re
    abim_ref[...] = abar_im
    for c in range(S5_GROUP):
        br, bi = bre_ref[c], bim_ref[c]
        bbre_ref[c] = zoh_re * br - zoh_im * bi
        bbim_ref[c] = zoh_re * bi + zoh_im * br


def s5_discretize(log_dt, a_re, a_im, b_re, b_im):
    g_, p_ = a_re.shape
    sds = jax.ShapeDtypeStruct
    return pl.pallas_call(
        _s5_discretize_kernel,
        out_shape=[sds((g_, p_), F32), sds((g_, p_), F32),
                   sds((S5_GROUP, g_, p_), F32), sds((S5_GROUP, g_, p_), F32)],
        name="s5_discretize",
    )(log_dt.reshape(g_, 1).astype(F32), a_re.astype(F32), a_im.astype(F32),
      b_re.astype(F32).transpose(2, 0, 1), b_im.astype(F32).transpose(2, 0, 1))


def _s5_kernel(u_ref, bw_ref, cw_ref, a_ref, d_ref, y_ref, st_ref, bu_ref, *, batch):
    ts = u_ref.shape[0]
    half = S5_BLOCK_STATE

    @pl.when(pl.program_id(1) == 0)
    def _():
        st_ref[...] = jnp.zeros_like(st_ref)

    u = u_ref[...].reshape(ts * batch, LANES)
    bu_ref[...] = _dot(u.astype(BF16), bw_ref[0])
    a = a_ref[0]
    a_re = jnp.broadcast_to(a[:, :half], (batch, half))
    a_im = jnp.broadcast_to(a[:, half:], (batch, half))

    def step(t, carry):
        s_re, s_im = carry
        rows = pl.ds(pl.multiple_of(t * batch, batch), batch)
        n_re = a_re * s_re - a_im * s_im + bu_ref[rows, 0:half]
        n_im = a_re * s_im + a_im * s_re + bu_ref[rows, half:2 * half]
        bu_ref[rows, 0:half] = n_re
        bu_ref[rows, half:2 * half] = n_im
        return n_re, n_im

    s_re, s_im = lax.fori_loop(0, ts, step, (st_ref[:, 0:half], st_ref[:, half:2 * half]))
    st_ref[:, 0:half] = s_re
    st_ref[:, half:2 * half] = s_im

    y = _dot(bu_ref[...].astype(BF16), cw_ref[0]) + d_ref[...] * u
    y_ref[...] = jax.nn.gelu(y).astype(y_ref.dtype).reshape(ts, batch, LANES)


def s5_core(u_tm, bw, cw, a_vec, d_skip, *, batch, seq):
    width = u_tm.shape[2]
    nblk = width // LANES
    ts = S5_TIME_BLOCK
    return pl.pallas_call(
        functools.partial(_s5_kernel, batch=batch),
        out_shape=jax.ShapeDtypeStruct((seq, batch, width), BF16),
        grid=(nblk, seq // ts),
        in_specs=[pl.BlockSpec((ts, batch, LANES), lambda j, i: (i, 0, j)),
                  pl.BlockSpec((1, LANES, 2 * S5_BLOCK_STATE), lambda j, i: (j, 0, 0)),
                  pl.BlockSpec((1, 2 * S5_BLOCK_STATE, LANES), lambda j, i: (j, 0, 0)),
                  pl.BlockSpec((1, 1, 2 * S5_BLOCK_STATE), lambda j, i: (j, 0, 0)),
                  pl.BlockSpec((1, LANES), lambda j, i: (0, j))],
        out_specs=pl.BlockSpec((ts, batch, LANES), lambda j, i: (i, 0, j)),
        scratch_shapes=[pltpu.VMEM((batch, 2 * S5_BLOCK_STATE), F32),
                        pltpu.VMEM((ts * batch, 2 * S5_BLOCK_STATE), F32)],
        compiler_params=_params("parallel", "arbitrary"), name="s5_core",
    )(u_tm, bw, cw, a_vec, d_skip.reshape(1, width).astype(F32))


def _s5_block_weights(abar_re, abar_im, bbar_re, bbar_im, c_re, c_im):
    g_, p_ = abar_re.shape
    gb = S5_GROUPS_PER_BLOCK
    nblk = g_ // gb
    eye = jnp.eye(gb, dtype=F32)

    def in_blocks(bbar):
        b = bbar.reshape(S5_GROUP, nblk, gb, p_).transpose(1, 2, 0, 3)
        return jnp.einsum("ngcp,gh->ngchp", b, eye).reshape(nblk, gb * S5_GROUP, gb * p_)

    def out_blocks(cmat):
        c = cmat.reshape(nblk, gb, S5_GROUP, p_).transpose(0, 1, 3, 2)
        return jnp.einsum("ngpc,gh->ngphc", c, eye).reshape(nblk, gb * p_, gb * S5_GROUP)

    bw = jnp.concatenate([in_blocks(bbar_re), in_blocks(bbar_im)], axis=2).astype(BF16)
    cw = jnp.concatenate([out_blocks(c_re.astype(F32)), out_blocks(-c_im.astype(F32))], axis=1).astype(BF16)
    a_vec = jnp.concatenate([abar_re.reshape(nblk, 1, gb * p_), abar_im.reshape(nblk, 1, gb * p_)], axis=2)
    return bw, cw, a_vec


TOKEN_TILE = 1024
COL_TILE = 1024
MLP_TOKEN_TILE = 512
MLP_FF_TILE = 512


def _pad_cols(w, width):
    return jnp.pad(w, ((0, 0), (0, width - w.shape[1])))


def _ssd_layer(h, g, w_in, conv_w, conv_b, dt_bias, a_log, d_skip, norm_g, w_out, *, batch, seq):
    inner = SSD_GROUPS * SSD_GROUP_W
    main = 2 * inner + 2 * SSD_GROUPS * SSD_STATE
    n_heads = SSD_GROUPS * SSD_HPG
    w16 = w_in.astype(BF16)
    proj, dtp = norm_matmul(h, g, w16[:, :main], w_small=_pad_cols(w16[:, main:], LANES),
                            tm=TOKEN_TILE, tn=COL_TILE, name="ssd_in_proj")
    y = ssd_core(proj.reshape(batch, seq, main), dtp[:, :n_heads].reshape(batch, seq, n_heads),
                 conv_w, conv_b, dt_bias, a_log, d_skip, batch=batch, seq=seq)
    return norm_matmul(y.reshape(batch * seq, inner), norm_g, w_out.astype(BF16), res=h,
                       tm=TOKEN_TILE // 2, tn=COL_TILE, name="ssd_out_proj")


def _mlstm_layer(h, g, w_in, gate_b, head_g, w_out, *, batch, seq):
    main = 2 * MLSTM_HEADS * (MLSTM_DQK + MLSTM_DV)
    w16 = w_in.astype(BF16)
    proj, gates = norm_matmul(h, g, w16[:, :main], w_small=_pad_cols(w16[:, main:], LANES),
                              tm=TOKEN_TILE, tn=COL_TILE, name="mlstm_in_proj")
    hs = mlstm_core(proj.reshape(batch, seq, main),
                    gates[:, :2 * MLSTM_HEADS].reshape(batch, seq, 2 * MLSTM_HEADS),
                    gate_b, head_g, batch=batch, seq=seq)
    return matmul_residual(hs.reshape(batch * seq, -1), w_out.astype(BF16), h,
                           tm=TOKEN_TILE, tn=COL_TILE, name="mlstm_out_proj")


def _s5_layer(h, g, w_in, b_re, b_im, c_re, c_im, d_skip, log_dt, a_re, a_im, w_out, *, batch, seq):
    d = h.shape[1]
    width = w_in.shape[1]
    tm = min(TOKEN_TILE, seq)
    nt = seq // tm
    ncol = width // COL_TILE
    u_tm = norm_matmul(h, g, w_in.astype(BF16), tm=tm, tn=COL_TILE,
                       out_index=lambda i, j: (i % nt, (i // nt) * ncol + j),
                       out_shape=(seq, batch * width), name="s5_in_proj")
    abar_re, abar_im, bbar_re, bbar_im = s5_discretize(log_dt, a_re, a_im, b_re, b_im)
    bw, cw, a_vec = _s5_block_weights(abar_re, abar_im, bbar_re, bbar_im, c_re, c_im)
    y_tm = s5_core(u_tm.reshape(seq, batch, width), bw, cw, a_vec, d_skip, batch=batch, seq=seq)
    return glu_matmul_residual(y_tm.reshape(seq, batch * width), w_out.astype(BF16), h,
                               batch=batch, tm=tm, tn=COL_TILE, name="s5_out_proj")


def kernel(x, norm_mix_g, norm_mlp_g, ssd_w_in, ssd_conv_w, ssd_conv_b, ssd_dt_bias, ssd_a_log, ssd_d, ssd_norm_g, ssd_w_out, mlstm_w_in, mlstm_gate_b, mlstm_head_g, mlstm_w_out, s5_w_in, s5_b_re, s5_b_im, s5_c_re, s5_c_im, s5_d, s5_log_dt, s5_a_re, s5_a_im, s5_w_out, mlp_w1, mlp_w2, final_norm_g):
    batch, seq, d = x.shape
    depth = norm_mix_g.shape[0]
    h = x.reshape(batch * seq, d)
    for layer in range(depth):
        kind, idx = layer % 3, layer // 3
        g = norm_mix_g[layer]
        if kind == 0:
            h = _ssd_layer(h, g, ssd_w_in[idx], ssd_conv_w[idx], ssd_conv_b[idx], ssd_dt_bias[idx],
                           ssd_a_log[idx], ssd_d[idx], ssd_norm_g[idx], ssd_w_out[idx], batch=batch, seq=seq)
        elif kind == 1:
            h = _mlstm_layer(h, g, mlstm_w_in[idx], mlstm_gate_b[idx], mlstm_head_g[idx], mlstm_w_out[idx],
                             batch=batch, seq=seq)
        else:
            h = _s5_layer(h, g, s5_w_in[idx], s5_b_re[idx], s5_b_im[idx], s5_c_re[idx], s5_c_im[idx],
                          s5_d[idx], s5_log_dt[idx], s5_a_re[idx], s5_a_im[idx], s5_w_out[idx],
                          batch=batch, seq=seq)
        h = mlp_block(h, norm_mlp_g[layer], mlp_w1[layer].astype(BF16), mlp_w2[layer].astype(BF16),
                      final_norm_g, final_norm=(layer == depth - 1),
                      tm=min(MLP_TOKEN_TILE, batch * seq), tf=MLP_FF_TILE, name=f"mlp_{layer}")
    return h.reshape(batch, seq, d)
```

```python
import functools
import math

import jax
import jax.numpy as jnp
from jax import lax
from jax.experimental import pallas as pl
from jax.experimental.pallas import tpu as pltpu

F32 = jnp.float32
BF16 = jnp.bfloat16

NORM_EPS = 1e-5
LOG2E = math.log2(math.e)
LANES = 128
VMEM_LIMIT_BYTES = 56 * 1024 * 1024

SSD_HEAD_DIM = 64
SSD_GROUPS = 8
SSD_HPG = 8
SSD_STATE = 128
SSD_CONV = 4
SSD_GROUP_W = SSD_HPG * SSD_HEAD_DIM
SSD_CHUNK = 256
CONV_HALO = 8

MLSTM_HEADS = 4
MLSTM_DQK = 256
MLSTM_DV = 512
MLSTM_CHUNK = 256

S5_GROUP = 16
S5_STATE = 64
S5_GROUPS_PER_BLOCK = LANES // S5_GROUP
S5_BLOCK_STATE = S5_GROUPS_PER_BLOCK * S5_STATE
S5_TIME_BLOCK = 128


def _params(*semantics):
    return pltpu.CompilerParams(dimension_semantics=semantics, vmem_limit_bytes=VMEM_LIMIT_BYTES)


def _dot(a, b):
    return jnp.dot(a, b, preferred_element_type=F32)


def _dot_nt(a, b):
    return lax.dot_general(a, b, (((1,), (1,)), ((), ())), preferred_element_type=F32)


def _dot_tn(a, b):
    return lax.dot_general(a, b, (((0,), (0,)), ((), ())), preferred_element_type=F32)


def _split3(x):
    hi = x.astype(BF16)
    r1 = x - hi.astype(F32)
    mid = r1.astype(BF16)
    lo = (r1 - mid.astype(F32)).astype(BF16)
    return hi, mid, lo


def _tri(n, upper):
    r = lax.broadcasted_iota(jnp.int32, (n, n), 0)
    c = lax.broadcasted_iota(jnp.int32, (n, n), 1)
    keep = (r <= c) if upper else (c <= r)
    return jnp.where(keep, 1.0, 0.0).astype(BF16)


def _cumsum_rows(x):
    t = _tri(x.shape[0], upper=False)
    hi, mid, lo = _split3(x)
    return _dot(t, hi) + _dot(t, mid) + _dot(t, lo)


def _cumsum_cols(x):
    t = _tri(x.shape[1], upper=True)
    hi, mid, lo = _split3(x)
    return _dot(hi, t) + _dot(mid, t) + _dot(lo, t)


def _causal(n):
    r = lax.broadcasted_iota(jnp.int32, (n, n), 0)
    c = lax.broadcasted_iota(jnp.int32, (n, n), 1)
    return c <= r


def _softplus(x):
    return jnp.maximum(x, 0.0) + jnp.log1p(jnp.exp(-jnp.abs(x)))


def _sigmoid(x):
    return 1.0 / (1.0 + jnp.exp(-x))


def _silu(x):
    return x * _sigmoid(x)


def _log_sigmoid(x):
    return -_softplus(-x)


def _rms_scale(x, g):
    ms = jnp.mean(x * x, axis=-1, keepdims=True)
    return x * lax.rsqrt(ms + NORM_EPS) * g


def _norm_matmul_kernel(*refs, has_small, small_transposed, has_res):
    it = iter(refs)
    x_ref, g_ref, w_ref = next(it), next(it), next(it)
    ws_ref = next(it) if has_small else None
    res_ref = next(it) if has_res else None
    o_ref = next(it)
    os_ref = next(it) if has_small else None
    xn_ref = next(it)

    @pl.when(pl.program_id(1) == 0)
    def _():
        xn = _rms_scale(x_ref[...].astype(F32), g_ref[...]).astype(BF16)
        xn_ref[...] = xn
        if has_small:
            os_ref[...] = _dot_nt(ws_ref[...], xn) if small_transposed else _dot(xn, ws_ref[...])

    y = _dot(xn_ref[...], w_ref[...])
    if has_res:
        y = res_ref[...] + y
    o_ref[...] = y


def norm_matmul(x, g, w, layer, *, n, w_small=None, small_transposed=False, res=None, tm, tn,
                out_index=None, out_shape=None, name):
    n_tok, k = x.shape
    assert n_tok % tm == 0 and n % tn == 0
    grid = (n_tok // tm, n // tn)
    out_index = out_index or (lambda i, j: (i, j))
    out_shape = out_shape or (n_tok, n)
    in_specs = [pl.BlockSpec((tm, k), lambda i, j: (i, 0)),
                pl.BlockSpec((1, k), lambda i, j: (0, 0)),
                pl.BlockSpec((None, k, tn), lambda i, j: (layer, 0, j))]
    args = [x, g.reshape(1, k).astype(F32), w]
    out_shapes = [jax.ShapeDtypeStruct(out_shape, F32)]
    out_specs = [pl.BlockSpec((tm, tn), out_index)]
    if w_small is not None:
        in_specs.append(pl.BlockSpec(w_small.shape, lambda i, j: (0, 0)))
        args.append(w_small)
        if small_transposed:
            out_shapes.append(jax.ShapeDtypeStruct((LANES, n_tok), F32))
            out_specs.append(pl.BlockSpec((LANES, tm), lambda i, j: (0, i)))
        else:
            out_shapes.append(jax.ShapeDtypeStruct((n_tok, LANES), F32))
            out_specs.append(pl.BlockSpec((tm, LANES), lambda i, j: (i, 0)))
    if res is not None:
        in_specs.append(pl.BlockSpec((tm, tn), lambda i, j: (i, j)))
        args.append(res)
    outs = pl.pallas_call(
        functools.partial(_norm_matmul_kernel, has_small=w_small is not None,
                          small_transposed=small_transposed, has_res=res is not None),
        out_shape=out_shapes, grid=grid, in_specs=in_specs, out_specs=out_specs,
        scratch_shapes=[pltpu.VMEM((tm, k), BF16)],
        compiler_params=_params("parallel", "arbitrary"), name=name)(*args)
    return outs if w_small is not None else outs[0]


def _matmul_res_kernel(a_ref, w_ref, res_ref, o_ref):
    o_ref[...] = res_ref[...] + _dot(a_ref[...], w_ref[...])


def matmul_residual(a, w, layer, res, *, tm, tn, name):
    t, k = a.shape
    n = w.shape[2]
    return pl.pallas_call(
        _matmul_res_kernel, out_shape=jax.ShapeDtypeStruct((t, n), F32),
        grid=(t // tm, n // tn),
        in_specs=[pl.BlockSpec((tm, k), lambda i, j: (i, 0)),
                  pl.BlockSpec((None, k, tn), lambda i, j: (layer, 0, j)),
                  pl.BlockSpec((tm, tn), lambda i, j: (i, j))],
        out_specs=pl.BlockSpec((tm, tn), lambda i, j: (i, j)),
        compiler_params=_params("parallel", "arbitrary"), name=name)(a, w, res)


def _glu_res_kernel(a_ref, wv_ref, wg_ref, res_ref, o_ref):
    a = a_ref[...]
    val = _dot(a, wv_ref[...])
    gate = _dot(a, wg_ref[...])
    o_ref[...] = res_ref[...] + val * _sigmoid(gate)


def glu_matmul_residual(a_tm, w, layer, res, *, batch, tm, tn, name):
    s = a_tm.shape[0]
    k = w.shape[1]
    n = w.shape[2] // 2
    nt = s // tm
    return pl.pallas_call(
        _glu_res_kernel, out_shape=jax.ShapeDtypeStruct((batch * s, n), F32),
        grid=(batch * nt, n // tn),
        in_specs=[pl.BlockSpec((tm, k), lambda i, j: (i % nt, i // nt)),
                  pl.BlockSpec((None, k, tn), lambda i, j: (layer, 0, j)),
                  pl.BlockSpec((None, k, tn), lambda i, j: (layer, 0, j + n // tn)),
                  pl.BlockSpec((tm, tn), lambda i, j: (i, j))],
        out_specs=pl.BlockSpec((tm, tn), lambda i, j: (i, j)),
        compiler_params=_params("parallel", "arbitrary"), name=name)(a_tm, w, w, res)


def _mlp_kernel(x_ref, g_ref, w1_ref, w2_ref, gf_ref, o_ref, xn_ref, *, final_norm):
    f = pl.program_id(1)

    @pl.when(f == 0)
    def _():
        x = x_ref[...]
        xn_ref[...] = _rms_scale(x, g_ref[...]).astype(BF16)
        o_ref[...] = x

    h1 = jnp.maximum(_dot(xn_ref[...], w1_ref[...]), 0.0)
    o_ref[...] += _dot((h1 * h1).astype(BF16), w2_ref[...])

    if final_norm:
        @pl.when(f == pl.num_programs(1) - 1)
        def _():
            o_ref[...] = _rms_scale(o_ref[...], gf_ref[...])


def mlp_block(x, g, w1, w2, layer, g_final, *, final_norm, tm, tf, name):
    t, d = x.shape
    dff = w1.shape[2]
    return pl.pallas_call(
        functools.partial(_mlp_kernel, final_norm=final_norm),
        out_shape=jax.ShapeDtypeStruct((t, d), F32),
        grid=(t // tm, dff // tf),
        in_specs=[pl.BlockSpec((tm, d), lambda i, f: (i, 0)),
                  pl.BlockSpec((1, d), lambda i, f: (0, 0)),
                  pl.BlockSpec((None, d, tf), lambda i, f: (layer, 0, f)),
                  pl.BlockSpec((None, tf, d), lambda i, f: (layer, f, 0)),
                  pl.BlockSpec((1, d), lambda i, f: (0, 0))],
        out_specs=pl.BlockSpec((tm, d), lambda i, f: (i, 0)),
        scratch_shapes=[pltpu.VMEM((tm, d), BF16)],
        compiler_params=_params("parallel", "arbitrary"), name=name,
    )(x, g.reshape(1, d), w1, w2, g_final.reshape(1, d))


def _causal_conv_silu(buf_ref, cur, w, b):
    l = cur.shape[0]
    buf_ref[CONV_HALO:CONV_HALO + l, :] = cur
    acc = b + w[SSD_CONV - 1:SSD_CONV, :] * cur
    for k in range(SSD_CONV - 1):
        start = CONV_HALO - (SSD_CONV - 1) + k
        acc = acc + w[k:k + 1, :] * buf_ref[start:start + l, :]
    buf_ref[0:CONV_HALO, :] = cur[l - CONV_HALO:, :]
    return _silu(acc)


def _dot3_tn(x, r):
    hi, mid, lo = _split3(x)
    return _dot_tn(hi, r) + _dot_tn(mid, r) + _dot_tn(lo, r)


def _ssd_head_rows(dt_raw_t, head_p, rows_ref):
    dtt = _softplus(dt_raw_t + head_p[:, 0:1])
    rows_ref[0:SSD_HPG, :] = _cumsum_cols(dtt * (-LOG2E * jnp.exp(head_p[:, 1:2])))
    rows_ref[SSD_HPG:, :] = dtt


def _ssd_kernel(z_ref, x_ref, b_ref, c_ref, dtt_ref, dttn_ref, cp_ref, hp_ref, r_ref,
                o_ref, xbuf, bbuf, cbuf, state_ref, rows_ref, wide_ref, xs_ref, bm_ref, cm_ref):
    first = pl.program_id(2) == 0
    l = x_ref.shape[1]
    sub = LANES
    nsub = l // sub
    hw = SSD_HEAD_DIM
    gw, ns = SSD_GROUP_W, SSD_STATE
    n_a = SSD_HPG * sub + gw

    @pl.when(first)
    def _():
        state_ref[...] = jnp.zeros_like(state_ref)
        for buf in (xbuf, bbuf, cbuf):
            buf[0:CONV_HALO, :] = jnp.zeros((CONV_HALO, buf.shape[1]), F32)
        _ssd_head_rows(dtt_ref[...], hp_ref[0], rows_ref)

    acst = rows_ref[0:SSD_HPG, :]
    wide_ref[:, :n_a] = _dot3_tn(acst, r_ref[:, :n_a])
    wide_ref[:, n_a:] = _dot3_tn(rows_ref[SSD_HPG:, :], r_ref[:, n_a:])

    cp = cp_ref[0]
    cw, cbias = cp[0:SSD_CONV, :], cp[SSD_CONV:SSD_CONV + 1, :]
    xs_ref[...] = _causal_conv_silu(xbuf, x_ref[0], cw[:, :gw], cbias[:, :gw])
    bm_ref[...] = _causal_conv_silu(bbuf, b_ref[0], cw[:, gw:gw + ns], cbias[:, gw:gw + ns]).astype(BF16)
    cm_ref[...] = _causal_conv_silu(cbuf, c_ref[0], cw[:, gw + ns:], cbias[:, gw + ns:]).astype(BF16)

    _ssd_head_rows(dttn_ref[...], hp_ref[0], rows_ref)

    xs, bm16, cm16 = xs_ref[...], bm_ref[...], cm_ref[...]
    acs_blk = wide_ref[:, :SSD_HPG * sub]
    acs64 = wide_ref[:, SSD_HPG * sub:n_a]
    dt64 = wide_ref[:, n_a:]
    last64 = acs64[l - 1:l, :]

    xdt = xs * dt64
    xdt16 = xdt.astype(BF16)
    st = state_ref[...]
    y = _dot(cm16, st.astype(BF16)) * jnp.exp2(acs64)
    state_ref[...] = st * jnp.exp2(last64) + _dot_tn(bm16, (xdt * jnp.exp2(last64 - acs64)).astype(BF16))
    y = y + cp[SSD_CONV + 1:SSD_CONV + 2, :gw] * xs

    cb = _dot_nt(cm16, bm16)
    diag_mask = _causal(sub)
    cb_blk = [[cb[r * sub:(r + 1) * sub, c * sub:(c + 1) * sub] if c < r else
               jnp.where(diag_mask, cb[r * sub:(r + 1) * sub, c * sub:(c + 1) * sub], 0.0)
               for c in range(r + 1)] for r in range(nsub)]
    lane = lax.broadcasted_iota(jnp.int32, (l, sub), 1)
    pair_out = []
    for p in range(SSD_HPG // 2):
        xp = xdt16[:, p * sub:(p + 1) * sub]
        rhs = [jnp.where(lane < hw, xp, jnp.zeros_like(xp)), jnp.where(lane >= hw, xp, jnp.zeros_like(xp))]
        rows = []
        for r in range(nsub):
            lhs_parts, rhs_parts = [], []
            for h in range(2):
                j = 2 * p + h
                col = acs_blk[r * sub:(r + 1) * sub, j * sub:(j + 1) * sub]
                for c in range(r + 1):
                    seg = col - acst[j:j + 1, c * sub:(c + 1) * sub]
                    if c == r:
                        seg = jnp.minimum(seg, 0.0)
                    lhs_parts.append((cb_blk[r][c] * jnp.exp2(seg)).astype(BF16))
                rhs_parts.append(rhs[h][:(r + 1) * sub, :])
            rows.append(_dot(jnp.concatenate(lhs_parts, axis=1), jnp.concatenate(rhs_parts, axis=0)))
        pair_out.append(jnp.concatenate(rows, axis=0))
    y = y + jnp.concatenate(pair_out, axis=1)
    o_ref[0] = y * _silu(z_ref[0])


def _head_expand(lanes_per_head, n_heads):
    return jnp.repeat(jnp.eye(n_heads, dtype=F32), lanes_per_head, axis=1)


def ssd_core(proj, dt_t, conv_w, conv_b, dt_bias, a_log, d_skip, *, batch, seq):
    l = min(SSD_CHUNK, seq)
    nc = seq // l
    g_, j_ = SSD_GROUPS, SSD_HPG
    inner = g_ * SSD_GROUP_W
    nx = inner // SSD_GROUP_W
    nb = 2 * inner // SSD_STATE
    expand64 = _head_expand(SSD_HEAD_DIM, j_)
    r = jnp.concatenate([_head_expand(LANES, j_), expand64, expand64], axis=1).astype(BF16)
    bc_w = g_ * SSD_STATE

    def per_group(p, rows):
        parts = [p[:, :inner].reshape(rows, g_, SSD_GROUP_W), p[:, inner:inner + bc_w].reshape(rows, g_, SSD_STATE),
                 p[:, inner + bc_w:].reshape(rows, g_, SSD_STATE)]
        return jnp.concatenate(parts, axis=2).transpose(1, 0, 2)

    cp_w = SSD_GROUP_W + 2 * SSD_STATE
    dskip = jnp.pad(jnp.repeat(d_skip.reshape(g_, 1, j_), SSD_HEAD_DIM, axis=2), ((0, 0), (0, 0), (0, cp_w - SSD_GROUP_W)))
    conv_p = jnp.concatenate([per_group(conv_w, SSD_CONV), per_group(conv_b.reshape(1, -1), 1), dskip,
                              jnp.zeros((g_, 8 - SSD_CONV - 2, cp_w), F32)], axis=1).astype(F32)
    head_p = jnp.pad(jnp.stack([dt_bias.reshape(g_, j_), a_log.reshape(g_, j_)], axis=2),
                     ((0, 0), (0, 0), (0, LANES - 2))).astype(F32)
    in_specs = [
        pl.BlockSpec((1, l, SSD_GROUP_W), lambda b, g, c: (b, c, g)),
        pl.BlockSpec((1, l, SSD_GROUP_W), lambda b, g, c: (b, c, nx + g)),
        pl.BlockSpec((1, l, SSD_STATE), lambda b, g, c: (b, c, nb + g)),
        pl.BlockSpec((1, l, SSD_STATE), lambda b, g, c: (b, c, nb + g_ + g)),
        pl.BlockSpec((j_, l), lambda b, g, c: (g, b * nc + c)),
        pl.BlockSpec((j_, l), lambda b, g, c: (g, b * nc + jnp.minimum(c + 1, nc - 1))),
        pl.BlockSpec((1, 8, cp_w), lambda b, g, c: (g, 0, 0)),
        pl.BlockSpec((1, 8, LANES), lambda b, g, c: (g, 0, 0)),
        pl.BlockSpec(r.shape, lambda b, g, c: (0, 0)),
    ]
    return pl.pallas_call(
        _ssd_kernel, out_shape=jax.ShapeDtypeStruct((batch, seq, inner), F32),
        grid=(batch, g_, nc), in_specs=in_specs,
        out_specs=pl.BlockSpec((1, l, SSD_GROUP_W), lambda b, g, c: (b, c, g)),
        scratch_shapes=[pltpu.VMEM((CONV_HALO + l, SSD_GROUP_W), F32),
                        pltpu.VMEM((CONV_HALO + l, SSD_STATE), F32),
                        pltpu.VMEM((CONV_HALO + l, SSD_STATE), F32),
                        pltpu.VMEM((SSD_STATE, SSD_GROUP_W), F32),
                        pltpu.VMEM((2 * j_, l), F32),
                        pltpu.VMEM((l, r.shape[1]), F32),
                        pltpu.VMEM((l, SSD_GROUP_W), F32),
                        pltpu.VMEM((l, SSD_STATE), BF16),
                        pltpu.VMEM((l, SSD_STATE), BF16)],
        compiler_params=_params("parallel", "parallel", "arbitrary"), name="ssd_core",
    )(proj, proj, proj, proj, dt_t, dt_t, conv_p, head_p, r)


def _mlstm_kernel(q_ref, k_ref, v_ref, o_ref, gc_ref, gr_ref, gbc_ref, gbr_ref, hg_ref,
                  out_ref, c_ref, n_ref, m_ref):
    l = q_ref.shape[1]

    @pl.when(pl.program_id(2) == 0)
    def _():
        c_ref[...] = jnp.zeros_like(c_ref)
        n_ref[...] = jnp.zeros_like(n_ref)
        m_ref[...] = jnp.zeros_like(m_ref)

    q = q_ref[0] * (MLSTM_DQK ** -0.5)
    k = k_ref[0]
    v16 = v_ref[0].astype(BF16)
    q16 = q.astype(BF16)
    gcol = gc_ref[0, 0] + gbc_ref[0]
    grow = gr_ref[0, 0] + gbr_ref[0]
    i_col, lf_col = gcol[:, 0:1], _log_sigmoid(gcol[:, 1:2])
    i_row, lf_row = grow[0:1, :], _log_sigmoid(grow[1:2, :])
    bcum = _cumsum_rows(lf_col)
    bcum_row = _cumsum_cols(lf_row)
    m_st = m_ref[0:1, 0:1]

    causal = _causal(l)
    intra = jnp.where(causal, bcum - bcum_row + i_row, -jnp.inf)
    inter = bcum + m_st
    m_t = jnp.maximum(inter, jnp.max(intra, axis=1, keepdims=True))
    w = jnp.exp(intra - m_t)
    scale_inter = jnp.exp(inter - m_t)
    qk = _dot_nt(q16, k.astype(BF16)) * w
    num = _dot(qk.astype(BF16), v16) + scale_inter * _dot(q16, c_ref[...].astype(BF16))
    qn = jnp.sum(q * n_ref[...], axis=1, keepdims=True)
    den = jnp.sum(qk, axis=1, keepdims=True) + scale_inter * qn
    h = num / jnp.maximum(jnp.abs(den), jnp.exp(-m_t))

    b_last = bcum[l - 1:l, :]
    tail = b_last - bcum + i_col
    m_new = jnp.maximum(b_last + m_st, jnp.max(tail, axis=0, keepdims=True))
    carry_scale = jnp.exp(b_last + m_st - m_new)
    wk = k * jnp.exp(tail - m_new)
    c_ref[...] = carry_scale * c_ref[...] + _dot_tn(wk.astype(BF16), v16)
    n_ref[...] = carry_scale * n_ref[...] + jnp.sum(wk, axis=0, keepdims=True)
    m_ref[...] = jnp.broadcast_to(m_new, m_ref.shape)

    hn = _rms_scale(h, hg_ref[...])
    out_ref[0] = (_sigmoid(o_ref[0]) * hn).astype(out_ref.dtype)


def mlstm_core(proj, gates, gate_b, head_g, *, batch, seq):
    l = MLSTM_CHUNK
    h_ = MLSTM_HEADS
    qk_w = h_ * MLSTM_DQK
    v_w = h_ * MLSTM_DV
    gsplit = jnp.stack([gates[..., :h_], gates[..., h_:2 * h_]], axis=-1)
    gcol = gsplit.transpose(0, 2, 1, 3)
    grow = gsplit.transpose(0, 2, 3, 1)
    gb = jnp.stack([gate_b[:h_], gate_b[h_:]], axis=-1).astype(F32)
    vo = 2 * qk_w // MLSTM_DV
    return pl.pallas_call(
        _mlstm_kernel, out_shape=jax.ShapeDtypeStruct((batch, seq, v_w), BF16),
        grid=(batch, h_, seq // l),
        in_specs=[pl.BlockSpec((1, l, MLSTM_DQK), lambda b, h, c: (b, c, h)),
                  pl.BlockSpec((1, l, MLSTM_DQK), lambda b, h, c: (b, c, h_ + h)),
                  pl.BlockSpec((1, l, MLSTM_DV), lambda b, h, c: (b, c, vo + h)),
                  pl.BlockSpec((1, l, MLSTM_DV), lambda b, h, c: (b, c, vo + h_ + h)),
                  pl.BlockSpec((1, 1, l, 2), lambda b, h, c: (b, h, c, 0)),
                  pl.BlockSpec((1, 1, 2, l), lambda b, h, c: (b, h, 0, c)),
                  pl.BlockSpec((1, 1, 2), lambda b, h, c: (h, 0, 0)),
                  pl.BlockSpec((1, 2, 1), lambda b, h, c: (h, 0, 0)),
                  pl.BlockSpec((1, MLSTM_DV), lambda b, h, c: (0, h))],
        out_specs=pl.BlockSpec((1, l, MLSTM_DV), lambda b, h, c: (b, c, h)),
        scratch_shapes=[pltpu.VMEM((MLSTM_DQK, MLSTM_DV), F32),
                        pltpu.VMEM((1, MLSTM_DQK), F32),
                        pltpu.VMEM((8, LANES), F32)],
        compiler_params=_params("parallel", "parallel", "arbitrary"), name="mlstm_core",
    )(proj, proj, proj, proj, gcol, grow, gb.reshape(h_, 1, 2), gb.reshape(h_, 2, 1),
      head_g.reshape(1, v_w).astype(F32))


def _s5_discretize_kernel(logdt_ref, are_ref, aim_ref, bre_ref, bim_ref,
                          abre_ref, abim_ref, bbre_ref, bbim_ref):
    ar, ai = are_ref[...], aim_ref[...]
    dt = jnp.exp(logdt_ref[...])
    mag = jnp.exp(ar * dt)
    abar_re, abar_im = mag * jnp.cos(ai * dt), mag * jnp.sin(ai * dt)
    den = ar * ar + ai * ai
    zoh_re = ((abar_re - 1.0) * ar + abar_im * ai) / den
    zoh_im = (abar_im * ar - (abar_re - 1.0) * ai) / den
    abre_ref[...] = abar_re
    abim_ref[...] = abar_im
    for c in range(S5_GROUP):
        br, bi = bre_ref[c], bim_ref[c]
        bbre_ref[c] = zoh_re * br - zoh_im * bi
        bbim_ref[c] = zoh_re * bi + zoh_im * br


def s5_discretize(log_dt, a_re, a_im, b_re, b_im):
    g_, p_ = a_re.shape
    sds = jax.ShapeDtypeStruct
    return pl.pallas_call(
        _s5_discretize_kernel,
        out_shape=[sds((g_, p_), F32), sds((g_, p_), F32),
                   sds((S5_GROUP, g_, p_), F32), sds((S5_GROUP, g_, p_), F32)],
        name="s5_discretize",
    )(log_dt.reshape(g_, 1).astype(F32), a_re.astype(F32), a_im.astype(F32),
      b_re.astype(F32).transpose(2, 0, 1), b_im.astype(F32).transpose(2, 0, 1))


def _s5_kernel(u_ref, bw_ref, cw_ref, a_ref, d_ref, y_ref, st_ref, bu_ref, *, batch):
    ts = u_ref.shape[0]
    half = S5_BLOCK_STATE

    @pl.when(pl.program_id(1) == 0)
    def _():
        st_ref[...] = jnp.zeros_like(st_ref)

    u = u_ref[...].reshape(ts * batch, LANES)
    bu_ref[...] = _dot(u.astype(BF16), bw_ref[0])
    a = a_ref[0]
    a_re = jnp.broadcast_to(a[:, :half], (batch, half))
    a_im = jnp.broadcast_to(a[:, half:], (batch, half))

    def step(t, carry):
        s_re, s_im = carry
        rows = pl.ds(pl.multiple_of(t * batch, batch), batch)
        n_re = a_re * s_re - a_im * s_im + bu_ref[rows, 0:half]
        n_im = a_re * s_im + a_im * s_re + bu_ref[rows, half:2 * half]
        bu_ref[rows, 0:half] = n_re
        bu_ref[rows, half:2 * half] = n_im
        return n_re, n_im

    s_re, s_im = lax.fori_loop(0, ts, step, (st_ref[:, 0:half], st_ref[:, half:2 * half]))
    st_ref[:, 0:half] = s_re
    st_ref[:, half:2 * half] = s_im

    y = _dot(bu_ref[...].astype(BF16), cw_ref[0]) + d_ref[...] * u
    y_ref[...] = jax.nn.gelu(y).astype(y_ref.dtype).reshape(ts, batch, LANES)


def s5_core(u_tm, bw, cw, a_vec, d_skip, *, batch, seq):
    width = u_tm.shape[2]
    nblk = width // LANES
    ts = S5_TIME_BLOCK
    return pl.pallas_call(
        functools.partial(_s5_kernel, batch=batch),
        out_shape=jax.ShapeDtypeStruct((seq, batch, width), BF16),
        grid=(nblk, seq // ts),
        in_specs=[pl.BlockSpec((ts, batch, LANES), lambda j, i: (i, 0, j)),
                  pl.BlockSpec((1, LANES, 2 * S5_BLOCK_STATE), lambda j, i: (j, 0, 0)),
                  pl.BlockSpec((1, 2 * S5_BLOCK_STATE, LANES), lambda j, i: (j, 0, 0)),
                  pl.BlockSpec((1, 1, 2 * S5_BLOCK_STATE), lambda j, i: (j, 0, 0)),
                  pl.BlockSpec((1, LANES), lambda j, i: (0, j))],
        out_specs=pl.BlockSpec((ts, batch, LANES), lambda j, i: (i, 0, j)),
        scratch_shapes=[pltpu.VMEM((batch, 2 * S5_BLOCK_STATE), F32),
                        pltpu.VMEM((ts * batch, 2 * S5_BLOCK_STATE), F32)],
        compiler_params=_params("parallel", "arbitrary"), name="s5_core",
    )(u_tm, bw, cw, a_vec, d_skip.reshape(1, width).astype(F32))


def _s5_block_weights(abar_re, abar_im, bbar_re, bbar_im, c_re, c_im):
    g_, p_ = abar_re.shape
    gb = S5_GROUPS_PER_BLOCK
    nblk = g_ // gb
    eye = jnp.eye(gb, dtype=F32)

    def in_blocks(bbar):
        b = bbar.reshape(S5_GROUP, nblk, gb, p_).transpose(1, 2, 0, 3)
        return jnp.einsum("ngcp,gh->ngchp", b, eye).reshape(nblk, gb * S5_GROUP, gb * p_)

    def out_blocks(cmat):
        c = cmat.reshape(nblk, gb, S5_GROUP, p_).transpose(0, 1, 3, 2)
        return jnp.einsum("ngpc,gh->ngphc", c, eye).reshape(nblk, gb * p_, gb * S5_GROUP)

    bw = jnp.concatenate([in_blocks(bbar_re), in_blocks(bbar_im)], axis=2).astype(BF16)
    cw = jnp.concatenate([out_blocks(c_re.astype(F32)), out_blocks(-c_im.astype(F32))], axis=1).astype(BF16)
    a_vec = jnp.concatenate([abar_re.reshape(nblk, 1, gb * p_), abar_im.reshape(nblk, 1, gb * p_)], axis=2)
    return bw, cw, a_vec


TOKEN_TILE = 1024
COL_TILE = 1024
MLP_TOKEN_TILE = 1024
MLP_FF_TILE = 512


def _small_head(w_in, idx, main, transposed):
    w = jnp.pad(w_in[idx, :, main:], ((0, 0), (0, LANES - (w_in.shape[2] - main)))).astype(BF16)
    return w.T if transposed else w


def _ssd_layer(h, g, w_in, w_in16, w_out16, idx, conv_w, conv_b, dt_bias, a_log, d_skip, norm_g, *, batch, seq):
    inner = SSD_GROUPS * SSD_GROUP_W
    main = 2 * inner + 2 * SSD_GROUPS * SSD_STATE
    proj, dt_t = norm_matmul(h, g, w_in16, idx, n=main, w_small=_small_head(w_in, idx, main, True),
                             small_transposed=True, tm=TOKEN_TILE, tn=COL_TILE, name="ssd_in_proj")
    y = ssd_core(proj.reshape(batch, seq, main), dt_t, conv_w, conv_b, dt_bias, a_log, d_skip,
                 batch=batch, seq=seq)
    return norm_matmul(y.reshape(batch * seq, inner), norm_g, w_out16, idx, n=w_out16.shape[2], res=h,
                       tm=TOKEN_TILE // 2, tn=COL_TILE, name="ssd_out_proj")


def _mlstm_layer(h, g, w_in, w_in16, w_out16, idx, gate_b, head_g, *, batch, seq):
    main = 2 * MLSTM_HEADS * (MLSTM_DQK + MLSTM_DV)
    proj, gates = norm_matmul(h, g, w_in16, idx, n=main, w_small=_small_head(w_in, idx, main, False),
                              tm=TOKEN_TILE, tn=COL_TILE, name="mlstm_in_proj")
    hs = mlstm_core(proj.reshape(batch, seq, main),
                    gates[:, :2 * MLSTM_HEADS].reshape(batch, seq, 2 * MLSTM_HEADS),
                    gate_b, head_g, batch=batch, seq=seq)
    return matmul_residual(hs.reshape(batch * seq, -1), w_out16, idx, h,
                           tm=TOKEN_TILE, tn=COL_TILE, name="mlstm_out_proj")


def _s5_layer(h, g, w_in16, w_out16, idx, b_re, b_im, c_re, c_im, d_skip, log_dt, a_re, a_im, *, batch, seq):
    width = w_in16.shape[2]
    tm = min(TOKEN_TILE, seq)
    nt = seq // tm
    ncol = width // COL_TILE
    u_tm = norm_matmul(h, g, w_in16, idx, n=width, tm=tm, tn=COL_TILE,
                       out_index=lambda i, j: (i % nt, (i // nt) * ncol + j),
                       out_shape=(seq, batch * width), name="s5_in_proj")
    abar_re, abar_im, bbar_re, bbar_im = s5_discretize(log_dt, a_re, a_im, b_re, b_im)
    bw, cw, a_vec = _s5_block_weights(abar_re, abar_im, bbar_re, bbar_im, c_re, c_im)
    y_tm = s5_core(u_tm.reshape(seq, batch, width), bw, cw, a_vec, d_skip, batch=batch, seq=seq)
    return glu_matmul_residual(y_tm.reshape(seq, batch * width), w_out16, idx, h,
                               batch=batch, tm=tm, tn=COL_TILE, name="s5_out_proj")


def kernel(x, norm_mix_g, norm_mlp_g, ssd_w_in, ssd_conv_w, ssd_conv_b, ssd_dt_bias, ssd_a_log, ssd_d, ssd_norm_g, ssd_w_out, mlstm_w_in, mlstm_gate_b, mlstm_head_g, mlstm_w_out, s5_w_in, s5_b_re, s5_b_im, s5_c_re, s5_c_im, s5_d, s5_log_dt, s5_a_re, s5_a_im, s5_w_out, mlp_w1, mlp_w2, final_norm_g):
    batch, seq, d = x.shape
    depth = norm_mix_g.shape[0]
    ssd_in16, ssd_out16 = ssd_w_in.astype(BF16), ssd_w_out.astype(BF16)
    mlstm_in16, mlstm_out16 = mlstm_w_in.astype(BF16), mlstm_w_out.astype(BF16)
    s5_in16, s5_out16 = s5_w_in.astype(BF16), s5_w_out.astype(BF16)
    mlp_w1_16, mlp_w2_16 = mlp_w1.astype(BF16), mlp_w2.astype(BF16)
    h = x.reshape(batch * seq, d)
    for layer in range(depth):
        kind, idx = layer % 3, layer // 3
        g = norm_mix_g[layer]
        if kind == 0:
            h = _ssd_layer(h, g, ssd_w_in, ssd_in16, ssd_out16, idx, ssd_conv_w[idx], ssd_conv_b[idx],
                           ssd_dt_bias[idx], ssd_a_log[idx], ssd_d[idx], ssd_norm_g[idx], batch=batch, seq=seq)
        elif kind == 1:
            h = _mlstm_layer(h, g, mlstm_w_in, mlstm_in16, mlstm_out16, idx, mlstm_gate_b[idx],
                             mlstm_head_g[idx], batch=batch, seq=seq)
        else:
            h = _s5_layer(h, g, s5_in16, s5_out16, idx, s5_b_re[idx], s5_b_im[idx], s5_c_re[idx], s5_c_im[idx],
                          s5_d[idx], s5_log_dt[idx], s5_a_re[idx], s5_a_im[idx], batch=batch, seq=seq)
        h = mlp_block(h, norm_mlp_g[layer], mlp_w1_16, mlp_w2_16, layer, final_norm_g,
                      final_norm=(layer == depth - 1),
                      tm=min(MLP_TOKEN_TILE, batch * seq), tf=MLP_FF_TILE, name=f"mlp_{layer}")
    return h.reshape(batch, seq, d)
```

```python
import functools
import math

import jax
import jax.numpy as jnp
from jax import lax
from jax.experimental import pallas as pl
from jax.experimental.pallas import tpu as pltpu

F32 = jnp.float32
BF16 = jnp.bfloat16

NORM_EPS = 1e-5
LOG2E = math.log2(math.e)
LANES = 128
VMEM_LIMIT_BYTES = 56 * 1024 * 1024

SSD_HEAD_DIM = 64
SSD_GROUPS = 8
SSD_HPG = 8
SSD_STATE = 128
SSD_CONV = 4
SSD_GROUP_W = SSD_HPG * SSD_HEAD_DIM
SSD_CHUNK = 256
CONV_HALO = 8

MLSTM_HEADS = 4
MLSTM_DQK = 256
MLSTM_DV = 512
MLSTM_CHUNK = 256

S5_GROUP = 16
S5_STATE = 64
S5_GROUPS_PER_BLOCK = LANES // S5_GROUP
S5_BLOCK_STATE = S5_GROUPS_PER_BLOCK * S5_STATE
S5_TIME_BLOCK = 128


def _params(*semantics):
    return pltpu.CompilerParams(dimension_semantics=semantics, vmem_limit_bytes=VMEM_LIMIT_BYTES)


def _dot(a, b):
    return jnp.dot(a, b, preferred_element_type=F32)


def _dot_nt(a, b):
    return lax.dot_general(a, b, (((1,), (1,)), ((), ())), preferred_element_type=F32)


def _dot_tn(a, b):
    return lax.dot_general(a, b, (((0,), (0,)), ((), ())), preferred_element_type=F32)


def _split3(x):
    hi = x.astype(BF16)
    r1 = x - hi.astype(F32)
    mid = r1.astype(BF16)
    lo = (r1 - mid.astype(F32)).astype(BF16)
    return hi, mid, lo


def _tri(n, upper):
    r = lax.broadcasted_iota(jnp.int32, (n, n), 0)
    c = lax.broadcasted_iota(jnp.int32, (n, n), 1)
    keep = (r <= c) if upper else (c <= r)
    return jnp.where(keep, 1.0, 0.0).astype(BF16)


def _cumsum_rows(x):
    t = _tri(x.shape[0], upper=False)
    hi, mid, lo = _split3(x)
    return _dot(t, hi) + _dot(t, mid) + _dot(t, lo)


def _cumsum_cols(x):
    t = _tri(x.shape[1], upper=True)
    hi, mid, lo = _split3(x)
    return _dot(hi, t) + _dot(mid, t) + _dot(lo, t)


def _causal(n):
    r = lax.broadcasted_iota(jnp.int32, (n, n), 0)
    c = lax.broadcasted_iota(jnp.int32, (n, n), 1)
    return c <= r


def _softplus(x):
    return jnp.maximum(x, 0.0) + jnp.log1p(jnp.exp(-jnp.abs(x)))


def _sigmoid(x):
    return 1.0 / (1.0 + jnp.exp(-x))


def _silu(x):
    return x * _sigmoid(x)


def _log_sigmoid(x):
    return -_softplus(-x)


def _rms_scale(x, g):
    ms = jnp.mean(x * x, axis=-1, keepdims=True)
    return x * lax.rsqrt(ms + NORM_EPS) * g


def _norm_matmul_kernel(*refs, has_small):
    it = iter(refs)
    x_ref, g_ref, w_ref = next(it), next(it), next(it)
    ws_ref = next(it) if has_small else None
    o_ref = next(it)
    os_ref = next(it) if has_small else None
    xn_ref = next(it)

    @pl.when(pl.program_id(1) == 0)
    def _():
        xn = _rms_scale(x_ref[...], g_ref[...]).astype(BF16)
        xn_ref[...] = xn
        if has_small:
            os_ref[...] = _dot(xn, ws_ref[...])

    o_ref[...] = _dot(xn_ref[...], w_ref[...])


def norm_matmul(x, g, w, layer, *, n, w_small=None, tm, tn, out_index=None, out_shape=None, name):
    n_tok, k = x.shape
    assert n_tok % tm == 0 and n % tn == 0
    grid = (n_tok // tm, n // tn)
    out_index = out_index or (lambda i, j: (i, j))
    out_shape = out_shape or (n_tok, n)
    in_specs = [pl.BlockSpec((tm, k), lambda i, j: (i, 0)),
                pl.BlockSpec((1, k), lambda i, j: (0, 0)),
                pl.BlockSpec((None, k, tn), lambda i, j: (layer, 0, j))]
    args = [x, g.reshape(1, k).astype(F32), w]
    out_shapes = [jax.ShapeDtypeStruct(out_shape, F32)]
    out_specs = [pl.BlockSpec((tm, tn), out_index)]
    if w_small is not None:
        in_specs.append(pl.BlockSpec(w_small.shape, lambda i, j: (0, 0)))
        args.append(w_small)
        out_shapes.append(jax.ShapeDtypeStruct((n_tok, LANES), F32))
        out_specs.append(pl.BlockSpec((tm, LANES), lambda i, j: (i, 0)))
    outs = pl.pallas_call(
        functools.partial(_norm_matmul_kernel, has_small=w_small is not None),
        out_shape=out_shapes, grid=grid, in_specs=in_specs, out_specs=out_specs,
        scratch_shapes=[pltpu.VMEM((tm, k), BF16)],
        compiler_params=_params("parallel", "arbitrary"), name=name)(*args)
    return outs if w_small is not None else outs[0]


def _ssd_in_proj_kernel(x_ref, g_ref, w_ref, wdt_ref, o_ref, dt_ref, xn_ref):
    @pl.when(pl.program_id(1) == 0)
    def _():
        xn = _rms_scale(x_ref[...], g_ref[...]).astype(BF16)
        xn_ref[...] = xn
        dt_ref[...] = _dot_nt(wdt_ref[...], xn)

    o_ref[...] = _dot(xn_ref[...], w_ref[...])


def ssd_in_proj(x, g, w, layer, w_dt_t, *, n, tm, tn):
    n_tok, k = x.shape
    assert n_tok % tm == 0 and n % tn == 0
    return pl.pallas_call(
        _ssd_in_proj_kernel,
        out_shape=[jax.ShapeDtypeStruct((n_tok, n), F32), jax.ShapeDtypeStruct((LANES, n_tok), F32)],
        grid=(n_tok // tm, n // tn),
        in_specs=[pl.BlockSpec((tm, k), lambda i, j: (i, 0)),
                  pl.BlockSpec((1, k), lambda i, j: (0, 0)),
                  pl.BlockSpec((None, k, tn), lambda i, j: (layer, 0, j)),
                  pl.BlockSpec(w_dt_t.shape, lambda i, j: (0, 0))],
        out_specs=[pl.BlockSpec((tm, tn), lambda i, j: (i, j)),
                   pl.BlockSpec((LANES, tm), lambda i, j: (0, i))],
        scratch_shapes=[pltpu.VMEM((tm, k), BF16)],
        compiler_params=_params("parallel", "arbitrary"), name="ssd_in_proj",
    )(x, g.reshape(1, k).astype(F32), w, w_dt_t)


def _matmul_res_kernel(a_ref, w_ref, res_ref, o_ref):
    o_ref[...] = res_ref[...] + _dot(a_ref[...], w_ref[...])


def matmul_residual(a, w, layer, res, *, tm, tn, name):
    t, k = a.shape
    n = w.shape[2]
    return pl.pallas_call(
        _matmul_res_kernel, out_shape=jax.ShapeDtypeStruct((t, n), F32),
        grid=(t // tm, n // tn),
        in_specs=[pl.BlockSpec((tm, k), lambda i, j: (i, 0)),
                  pl.BlockSpec((None, k, tn), lambda i, j: (layer, 0, j)),
                  pl.BlockSpec((tm, tn), lambda i, j: (i, j))],
        out_specs=pl.BlockSpec((tm, tn), lambda i, j: (i, j)),
        compiler_params=_params("parallel", "arbitrary"), name=name)(a, w, res)


def _rms_matmul_res_kernel(a_ref, ss_ref, w_ref, res_ref, o_ref):
    k = a_ref.shape[1]
    ss = ss_ref[...]
    tot = ss[:, :LANES]
    for q in range(1, ss.shape[1] // LANES):
        tot = tot + ss[:, q * LANES:(q + 1) * LANES]
    inv = lax.rsqrt(tot * (1.0 / k) + NORM_EPS)
    y = _dot(a_ref[...], w_ref[...])
    o_ref[...] = res_ref[...] + jnp.concatenate([inv] * (y.shape[1] // LANES), axis=1) * y


def rms_matmul_residual(a, ss, w, layer, res, *, tm, tn, name):
    t, k = a.shape
    n = w.shape[2]
    return pl.pallas_call(
        _rms_matmul_res_kernel, out_shape=jax.ShapeDtypeStruct((t, n), F32),
        grid=(t // tm, n // tn),
        in_specs=[pl.BlockSpec((tm, k), lambda i, j: (i, 0)),
                  pl.BlockSpec((tm, ss.shape[1]), lambda i, j: (i, 0)),
                  pl.BlockSpec((None, k, tn), lambda i, j: (layer, 0, j)),
                  pl.BlockSpec((tm, tn), lambda i, j: (i, j))],
        out_specs=pl.BlockSpec((tm, tn), lambda i, j: (i, j)),
        compiler_params=_params("parallel", "arbitrary"), name=name)(a, ss, w, res)


def _glu_res_kernel(a_ref, wv_ref, wg_ref, res_ref, o_ref):
    a = a_ref[...]
    val = _dot(a, wv_ref[...])
    gate = _dot(a, wg_ref[...])
    o_ref[...] = res_ref[...] + val * _sigmoid(gate)


def glu_matmul_residual(a_tm, w, layer, res, *, batch, tm, tn, name):
    s = a_tm.shape[0]
    k = w.shape[1]
    n = w.shape[2] // 2
    nt = s // tm
    return pl.pallas_call(
        _glu_res_kernel, out_shape=jax.ShapeDtypeStruct((batch * s, n), F32),
        grid=(batch * nt, n // tn),
        in_specs=[pl.BlockSpec((tm, k), lambda i, j: (i % nt, i // nt)),
                  pl.BlockSpec((None, k, tn), lambda i, j: (layer, 0, j)),
                  pl.BlockSpec((None, k, tn), lambda i, j: (layer, 0, j + n // tn)),
                  pl.BlockSpec((tm, tn), lambda i, j: (i, j))],
        out_specs=pl.BlockSpec((tm, tn), lambda i, j: (i, j)),
        compiler_params=_params("parallel", "arbitrary"), name=name)(a_tm, w, w, res)


def _mlp_kernel(x_ref, g_ref, w1_ref, w2_ref, gf_ref, o_ref, xn_ref, *, final_norm):
    f = pl.program_id(1)

    @pl.when(f == 0)
    def _():
        x = x_ref[...]
        xn_ref[...] = _rms_scale(x, g_ref[...]).astype(BF16)
        o_ref[...] = x

    h1 = jnp.maximum(_dot(xn_ref[...], w1_ref[...]), 0.0)
    o_ref[...] += _dot((h1 * h1).astype(BF16), w2_ref[...])

    if final_norm:
        @pl.when(f == pl.num_programs(1) - 1)
        def _():
            o_ref[...] = _rms_scale(o_ref[...], gf_ref[...])


def mlp_block(x, g, w1, w2, layer, g_final, *, final_norm, tm, tf, name):
    t, d = x.shape
    dff = w1.shape[2]
    return pl.pallas_call(
        functools.partial(_mlp_kernel, final_norm=final_norm),
        out_shape=jax.ShapeDtypeStruct((t, d), F32),
        grid=(t // tm, dff // tf),
        in_specs=[pl.BlockSpec((tm, d), lambda i, f: (i, 0)),
                  pl.BlockSpec((1, d), lambda i, f: (0, 0)),
                  pl.BlockSpec((None, d, tf), lambda i, f: (layer, 0, f)),
                  pl.BlockSpec((None, tf, d), lambda i, f: (layer, f, 0)),
                  pl.BlockSpec((1, d), lambda i, f: (0, 0))],
        out_specs=pl.BlockSpec((tm, d), lambda i, f: (i, 0)),
        scratch_shapes=[pltpu.VMEM((tm, d), BF16)],
        compiler_params=_params("parallel", "arbitrary"), name=name,
    )(x, g.reshape(1, d), w1, w2, g_final.reshape(1, d))


def _causal_conv_silu(buf_ref, cur, w, b):
    l = cur.shape[0]
    buf_ref[CONV_HALO:CONV_HALO + l, :] = cur
    acc = b + w[SSD_CONV - 1:SSD_CONV, :] * cur
    for k in range(SSD_CONV - 1):
        start = CONV_HALO - (SSD_CONV - 1) + k
        acc = acc + w[k:k + 1, :] * buf_ref[start:start + l, :]
    buf_ref[0:CONV_HALO, :] = cur[l - CONV_HALO:, :]
    return _silu(acc)


def _dot3_tn(x, r):
    hi, mid, lo = _split3(x)
    return _dot_tn(hi, r) + _dot_tn(mid, r) + _dot_tn(lo, r)


def _ssd_head_rows(dt_raw_t, head_p, rows_ref):
    dtt = _softplus(dt_raw_t + head_p[:, 0:1])
    rows_ref[0:SSD_HPG, :] = _cumsum_cols(dtt * (-LOG2E * jnp.exp(head_p[:, 1:2])))
    rows_ref[SSD_HPG:, :] = dtt


def _ssd_kernel(z_ref, x_ref, b_ref, c_ref, dtt_ref, dttn_ref, cp_ref, hp_ref, r_ref,
                o_ref, ss_ref, xbuf, bbuf, cbuf, state_ref, rows_ref, wide_ref, xs_ref, bm_ref, cm_ref):
    first = pl.program_id(2) == 0
    l = x_ref.shape[1]
    sub = LANES
    nsub = l // sub
    hw = SSD_HEAD_DIM
    gw, ns = SSD_GROUP_W, SSD_STATE
    n_a = SSD_HPG * sub + gw

    @pl.when(first)
    def _():
        state_ref[...] = jnp.zeros_like(state_ref)
        for buf in (xbuf, bbuf, cbuf):
            buf[0:CONV_HALO, :] = jnp.zeros((CONV_HALO, buf.shape[1]), F32)
        _ssd_head_rows(dtt_ref[...], hp_ref[0], rows_ref)

    acst = rows_ref[0:SSD_HPG, :]
    wide_ref[...] = _dot_tn(jnp.concatenate(_split3(rows_ref[...]), axis=0), r_ref[...])

    cp = cp_ref[0]
    cw, cbias = cp[0:SSD_CONV, :], cp[SSD_CONV:SSD_CONV + 1, :]
    xs_ref[...] = _causal_conv_silu(xbuf, x_ref[0], cw[:, :gw], cbias[:, :gw])
    bm_ref[...] = _causal_conv_silu(bbuf, b_ref[0], cw[:, gw:gw + ns], cbias[:, gw:gw + ns]).astype(BF16)
    cm_ref[...] = _causal_conv_silu(cbuf, c_ref[0], cw[:, gw + ns:], cbias[:, gw + ns:]).astype(BF16)

    _ssd_head_rows(dttn_ref[...], hp_ref[0], rows_ref)

    xs, bm16, cm16 = xs_ref[...], bm_ref[...], cm_ref[...]
    acs_blk = wide_ref[:, :SSD_HPG * sub]
    acs64 = wide_ref[:, SSD_HPG * sub:n_a]
    dt64 = wide_ref[:, n_a:]
    last64 = acs64[l - 1:l, :]

    xdt = xs * dt64
    xdt16 = xdt.astype(BF16)
    st = state_ref[...]
    y = _dot(cm16, st.astype(BF16)) * jnp.exp2(acs64)
    state_ref[...] = st * jnp.exp2(last64) + _dot_tn(bm16, (xdt * jnp.exp2(last64 - acs64)).astype(BF16))
    y = y + cp[SSD_CONV + 1:SSD_CONV + 2, :gw] * xs

    cb = _dot_nt(cm16, bm16)
    diag_mask = _causal(sub)
    cb_blk = [[cb[r * sub:(r + 1) * sub, c * sub:(c + 1) * sub] if c < r else
               jnp.where(diag_mask, cb[r * sub:(r + 1) * sub, c * sub:(c + 1) * sub], 0.0)
               for c in range(r + 1)] for r in range(nsub)]
    lane = lax.broadcasted_iota(jnp.int32, (l, sub), 1)
    pair_out = []
    for p in range(SSD_HPG // 2):
        xp = xdt16[:, p * sub:(p + 1) * sub]
        rhs = [jnp.where(lane < hw, xp, jnp.zeros_like(xp)), jnp.where(lane >= hw, xp, jnp.zeros_like(xp))]
        rows = []
        for r in range(nsub):
            lhs_parts, rhs_parts = [], []
            for h in range(2):
                j = 2 * p + h
                col = acs_blk[r * sub:(r + 1) * sub, j * sub:(j + 1) * sub]
                for c in range(r + 1):
                    seg = col - acst[j:j + 1, c * sub:(c + 1) * sub]
                    if c == r:
                        seg = jnp.minimum(seg, 0.0)
                    lhs_parts.append((cb_blk[r][c] * jnp.exp2(seg)).astype(BF16))
                rhs_parts.append(rhs[h][:(r + 1) * sub, :])
            rows.append(_dot(jnp.concatenate(lhs_parts, axis=1), jnp.concatenate(rhs_parts, axis=0)))
        pair_out.append(jnp.concatenate(rows, axis=0))
    y = y + jnp.concatenate(pair_out, axis=1)
    y = y * _silu(z_ref[0])
    ss_ref[0] = jnp.broadcast_to(jnp.sum(y * y, axis=1, keepdims=True), (l, sub))
    o_ref[0] = (y * cp[SSD_CONV + 2:SSD_CONV + 3, :gw]).astype(o_ref.dtype)


def _head_expand(lanes_per_head, n_heads):
    return jnp.repeat(jnp.eye(n_heads, dtype=F32), lanes_per_head, axis=1)


def ssd_core(proj, dt_t, conv_w, conv_b, dt_bias, a_log, d_skip, norm_g, *, batch, seq):
    l = min(SSD_CHUNK, seq)
    nc = seq // l
    g_, j_ = SSD_GROUPS, SSD_HPG
    inner = g_ * SSD_GROUP_W
    nx = inner // SSD_GROUP_W
    nb = 2 * inner // SSD_STATE
    expand64 = _head_expand(SSD_HEAD_DIM, j_)
    r16 = jnp.concatenate([
        jnp.concatenate([_head_expand(LANES, j_), expand64, jnp.zeros_like(expand64)], axis=1),
        jnp.concatenate([jnp.zeros((j_, j_ * LANES + SSD_GROUP_W), F32), expand64], axis=1)], axis=0)
    r = jnp.concatenate([r16, r16, r16], axis=0).astype(BF16)
    bc_w = g_ * SSD_STATE

    def per_group(p, rows):
        parts = [p[:, :inner].reshape(rows, g_, SSD_GROUP_W), p[:, inner:inner + bc_w].reshape(rows, g_, SSD_STATE),
                 p[:, inner + bc_w:].reshape(rows, g_, SSD_STATE)]
        return jnp.concatenate(parts, axis=2).transpose(1, 0, 2)

    cp_w = SSD_GROUP_W + 2 * SSD_STATE
    x_only = ((0, 0), (0, 0), (0, cp_w - SSD_GROUP_W))
    dskip = jnp.pad(jnp.repeat(d_skip.reshape(g_, 1, j_), SSD_HEAD_DIM, axis=2), x_only)
    gain = jnp.pad(norm_g.reshape(g_, 1, SSD_GROUP_W), x_only)
    conv_p = jnp.concatenate([per_group(conv_w, SSD_CONV), per_group(conv_b.reshape(1, -1), 1), dskip, gain,
                              jnp.zeros((g_, 8 - SSD_CONV - 3, cp_w), F32)], axis=1).astype(F32)
    head_p = jnp.pad(jnp.stack([dt_bias.reshape(g_, j_), a_log.reshape(g_, j_)], axis=2),
                     ((0, 0), (0, 0), (0, LANES - 2))).astype(F32)
    in_specs = [
        pl.BlockSpec((1, l, SSD_GROUP_W), lambda b, g, c: (b, c, g)),
        pl.BlockSpec((1, l, SSD_GROUP_W), lambda b, g, c: (b, c, nx + g)),
        pl.BlockSpec((1, l, SSD_STATE), lambda b, g, c: (b, c, nb + g)),
        pl.BlockSpec((1, l, SSD_STATE), lambda b, g, c: (b, c, nb + g_ + g)),
        pl.BlockSpec((j_, l), lambda b, g, c: (g, b * nc + c)),
        pl.BlockSpec((j_, l), lambda b, g, c: (g, b * nc + jnp.minimum(c + 1, nc - 1))),
        pl.BlockSpec((1, 8, cp_w), lambda b, g, c: (g, 0, 0)),
        pl.BlockSpec((1, 8, LANES), lambda b, g, c: (g, 0, 0)),
        pl.BlockSpec(r.shape, lambda b, g, c: (0, 0)),
    ]
    return pl.pallas_call(
        _ssd_kernel,
        out_shape=[jax.ShapeDtypeStruct((batch, seq, inner), BF16),
                   jax.ShapeDtypeStruct((batch, seq, g_ * LANES), F32)],
        grid=(batch, g_, nc), in_specs=in_specs,
        out_specs=[pl.BlockSpec((1, l, SSD_GROUP_W), lambda b, g, c: (b, c, g)),
                   pl.BlockSpec((1, l, LANES), lambda b, g, c: (b, c, g))],
        scratch_shapes=[pltpu.VMEM((CONV_HALO + l, SSD_GROUP_W), F32),
                        pltpu.VMEM((CONV_HALO + l, SSD_STATE), F32),
                        pltpu.VMEM((CONV_HALO + l, SSD_STATE), F32),
                        pltpu.VMEM((SSD_STATE, SSD_GROUP_W), F32),
                        pltpu.VMEM((2 * j_, l), F32),
                        pltpu.VMEM((l, r.shape[1]), F32),
                        pltpu.VMEM((l, SSD_GROUP_W), F32),
                        pltpu.VMEM((l, SSD_STATE), BF16),
                        pltpu.VMEM((l, SSD_STATE), BF16)],
        compiler_params=_params("parallel", "parallel", "arbitrary"), name="ssd_core",
    )(proj, proj, proj, proj, dt_t, dt_t, conv_p, head_p, r)


def _mlstm_kernel(q_ref, k_ref, v_ref, o_ref, gc_ref, gr_ref, gbc_ref, gbr_ref, hg_ref,
                  out_ref, c_ref, n_ref, m_ref):
    l = q_ref.shape[1]

    @pl.when(pl.program_id(2) == 0)
    def _():
        c_ref[...] = jnp.zeros_like(c_ref)
        n_ref[...] = jnp.zeros_like(n_ref)
        m_ref[...] = jnp.zeros_like(m_ref)

    q = q_ref[0] * (MLSTM_DQK ** -0.5)
    k = k_ref[0]
    v16 = v_ref[0].astype(BF16)
    q16 = q.astype(BF16)
    gcol = gc_ref[0, 0] + gbc_ref[0]
    grow = gr_ref[0, 0] + gbr_ref[0]
    i_col, lf_col = gcol[:, 0:1], _log_sigmoid(gcol[:, 1:2])
    i_row, lf_row = grow[0:1, :], _log_sigmoid(grow[1:2, :])
    bcum = _cumsum_rows(lf_col)
    bcum_row = _cumsum_cols(lf_row)
    m_st = m_ref[0:1, 0:1]

    causal = _causal(l)
    intra = jnp.where(causal, bcum - bcum_row + i_row, -jnp.inf)
    inter = bcum + m_st
    m_t = jnp.maximum(inter, jnp.max(intra, axis=1, keepdims=True))
    w = jnp.exp(intra - m_t)
    scale_inter = jnp.exp(inter - m_t)
    qk = _dot_nt(q16, k.astype(BF16)) * w
    num = _dot(qk.astype(BF16), v16) + scale_inter * _dot(q16, c_ref[...].astype(BF16))
    qn = jnp.sum(q * n_ref[...], axis=1, keepdims=True)
    den = jnp.sum(qk, axis=1, keepdims=True) + scale_inter * qn
    h = num / jnp.maximum(jnp.abs(den), jnp.exp(-m_t))

    b_last = bcum[l - 1:l, :]
    tail = b_last - bcum + i_col
    m_new = jnp.maximum(b_last + m_st, jnp.max(tail, axis=0, keepdims=True))
    carry_scale = jnp.exp(b_last + m_st - m_new)
    wk = k * jnp.exp(tail - m_new)
    c_ref[...] = carry_scale * c_ref[...] + _dot_tn(wk.astype(BF16), v16)
    n_ref[...] = carry_scale * n_ref[...] + jnp.sum(wk, axis=0, keepdims=True)
    m_ref[...] = jnp.broadcast_to(m_new, m_ref.shape)

    hn = _rms_scale(h, hg_ref[...])
    out_ref[0] = (_sigmoid(o_ref[0]) * hn).astype(out_ref.dtype)


def mlstm_core(proj, gates, gate_b, head_g, *, batch, seq):
    l = MLSTM_CHUNK
    h_ = MLSTM_HEADS
    qk_w = h_ * MLSTM_DQK
    v_w = h_ * MLSTM_DV
    gsplit = jnp.stack([gates[..., :h_], gates[..., h_:2 * h_]], axis=-1)
    gcol = gsplit.transpose(0, 2, 1, 3)
    grow = gsplit.transpose(0, 2, 3, 1)
    gb = jnp.stack([gate_b[:h_], gate_b[h_:]], axis=-1).astype(F32)
    vo = 2 * qk_w // MLSTM_DV
    return pl.pallas_call(
        _mlstm_kernel, out_shape=jax.ShapeDtypeStruct((batch, seq, v_w), BF16),
        grid=(batch, h_, seq // l),
        in_specs=[pl.BlockSpec((1, l, MLSTM_DQK), lambda b, h, c: (b, c, h)),
                  pl.BlockSpec((1, l, MLSTM_DQK), lambda b, h, c: (b, c, h_ + h)),
                  pl.BlockSpec((1, l, MLSTM_DV), lambda b, h, c: (b, c, vo + h)),
                  pl.BlockSpec((1, l, MLSTM_DV), lambda b, h, c: (b, c, vo + h_ + h)),
                  pl.BlockSpec((1, 1, l, 2), lambda b, h, c: (b, h, c, 0)),
                  pl.BlockSpec((1, 1, 2, l), lambda b, h, c: (b, h, 0, c)),
                  pl.BlockSpec((1, 1, 2), lambda b, h, c: (h, 0, 0)),
                  pl.BlockSpec((1, 2, 1), lambda b, h, c: (h, 0, 0)),
                  pl.BlockSpec((1, MLSTM_DV), lambda b, h, c: (0, h))],
        out_specs=pl.BlockSpec((1, l, MLSTM_DV), lambda b, h, c: (b, c, h)),
        scratch_shapes=[pltpu.VMEM((MLSTM_DQK, MLSTM_DV), F32),
                        pltpu.VMEM((1, MLSTM_DQK), F32),
                        pltpu.VMEM((8, LANES), F32)],
        compiler_params=_params("parallel", "parallel", "arbitrary"), name="mlstm_core",
    )(proj, proj, proj, proj, gcol, grow, gb.reshape(h_, 1, 2), gb.reshape(h_, 2, 1),
      head_g.reshape(1, v_w).astype(F32))


def _s5_discretize_kernel(logdt_ref, are_ref, aim_ref, bre_ref, bim_ref,
                          abre_ref, abim_ref, bbre_ref, bbim_ref):
    ar, ai = are_ref[...], aim_ref[...]
    dt = jnp.exp(logdt_ref[...])
    mag = jnp.exp(ar * dt)
    abar_re, abar_im = mag * jnp.cos(ai * dt), mag * jnp.sin(ai * dt)
    den = ar * ar + ai * ai
    zoh_re = ((abar_re - 1.0) * ar + abar_im * ai) / den
    zoh_im = (abar_im * ar - (abar_re - 1.0) * ai) / den
    abre_ref[...] = abar_re
    abim_ref[...] = abar_im
    for c in range(S5_GROUP):
        br, bi = bre_ref[c], bim_ref[c]
        bbre_ref[c] = zoh_re * br - zoh_im * bi
        bbim_ref[c] = zoh_re * bi + zoh_im * br


def s5_discretize(log_dt, a_re, a_im, b_re, b_im):
    g_, p_ = a_re.shape
    sds = jax.ShapeDtypeStruct
    return pl.pallas_call(
        _s5_discretize_kernel,
        out_shape=[sds((g_, p_), F32), sds((g_, p_), F32),
                   sds((S5_GROUP, g_, p_), F32), sds((S5_GROUP, g_, p_), F32)],
        name="s5_discretize",
    )(log_dt.reshape(g_, 1).astype(F32), a_re.astype(F32), a_im.astype(F32),
      b_re.astype(F32).transpose(2, 0, 1), b_im.astype(F32).transpose(2, 0, 1))


def _s5_kernel(u_ref, bw_ref, cw_ref, a_ref, d_ref, y_ref, st_ref, bu_ref, *, batch):
    ts = u_ref.shape[0]
    half = S5_BLOCK_STATE

    @pl.when(pl.program_id(1) == 0)
    def _():
        st_ref[...] = jnp.zeros_like(st_ref)

    u = u_ref[...].reshape(ts * batch, LANES)
    bu_ref[...] = _dot(u.astype(BF16), bw_ref[0])
    a = a_ref[0]
    a_re = jnp.broadcast_to(a[:, :half], (batch, half))
    a_im = jnp.broadcast_to(a[:, half:], (batch, half))

    def step(t, carry):
        s_re, s_im = carry
        rows = pl.ds(pl.multiple_of(t * batch, batch), batch)
        n_re = a_re * s_re - a_im * s_im + bu_ref[rows, 0:half]
        n_im = a_re * s_im + a_im * s_re + bu_ref[rows, half:2 * half]
        bu_ref[rows, 0:half] = n_re
        bu_ref[rows, half:2 * half] = n_im
        return n_re, n_im

    s_re, s_im = lax.fori_loop(0, ts, step, (st_ref[:, 0:half], st_ref[:, half:2 * half]))
    st_ref[:, 0:half] = s_re
    st_ref[:, half:2 * half] = s_im

    y = _dot(bu_ref[...].astype(BF16), cw_ref[0]) + d_ref[...] * u
    y_ref[...] = jax.nn.gelu(y).astype(y_ref.dtype).reshape(ts, batch, LANES)


def s5_core(u_tm, bw, cw, a_vec, d_skip, *, batch, seq):
    width = u_tm.shape[2]
    nblk = width // LANES
    ts = S5_TIME_BLOCK
    return pl.pallas_call(
        functools.partial(_s5_kernel, batch=batch),
        out_shape=jax.ShapeDtypeStruct((seq, batch, width), BF16),
        grid=(nblk, seq // ts),
        in_specs=[pl.BlockSpec((ts, batch, LANES), lambda j, i: (i, 0, j)),
                  pl.BlockSpec((1, LANES, 2 * S5_BLOCK_STATE), lambda j, i: (j, 0, 0)),
                  pl.BlockSpec((1, 2 * S5_BLOCK_STATE, LANES), lambda j, i: (j, 0, 0)),
                  pl.BlockSpec((1, 1, 2 * S5_BLOCK_STATE), lambda j, i: (j, 0, 0)),
                  pl.BlockSpec((1, LANES), lambda j, i: (0, j))],
        out_specs=pl.BlockSpec((ts, batch, LANES), lambda j, i: (i, 0, j)),
        scratch_shapes=[pltpu.VMEM((batch, 2 * S5_BLOCK_STATE), F32),
                        pltpu.VMEM((ts * batch, 2 * S5_BLOCK_STATE), F32)],
        compiler_params=_params("parallel", "arbitrary"), name="s5_core",
    )(u_tm, bw, cw, a_vec, d_skip.reshape(1, width).astype(F32))


def _s5_block_weights(abar_re, abar_im, bbar_re, bbar_im, c_re, c_im):
    g_, p_ = abar_re.shape
    gb = S5_GROUPS_PER_BLOCK
    nblk = g_ // gb
    eye = jnp.eye(gb, dtype=F32)

    def in_blocks(bbar):
        b = bbar.reshape(S5_GROUP, nblk, gb, p_).transpose(1, 2, 0, 3)
        return jnp.einsum("ngcp,gh->ngchp", b, eye).reshape(nblk, gb * S5_GROUP, gb * p_)

    def out_blocks(cmat):
        c = cmat.reshape(nblk, gb, S5_GROUP, p_).transpose(0, 1, 3, 2)
        return jnp.einsum("ngpc,gh->ngphc", c, eye).reshape(nblk, gb * p_, gb * S5_GROUP)

    bw = jnp.concatenate([in_blocks(bbar_re), in_blocks(bbar_im)], axis=2).astype(BF16)
    cw = jnp.concatenate([out_blocks(c_re.astype(F32)), out_blocks(-c_im.astype(F32))], axis=1).astype(BF16)
    a_vec = jnp.concatenate([abar_re.reshape(nblk, 1, gb * p_), abar_im.reshape(nblk, 1, gb * p_)], axis=2)
    return bw, cw, a_vec


TOKEN_TILE = 1024
COL_TILE = 1024
MLP_TOKEN_TILE = 1024
MLP_FF_TILE = 512


def _small_head(w_in, idx, main, transposed):
    w = jnp.pad(w_in[idx, :, main:], ((0, 0), (0, LANES - (w_in.shape[2] - main)))).astype(BF16)
    return w.T if transposed else w


def _ssd_layer(h, g, w_in, w_in16, w_out16, idx, conv_w, conv_b, dt_bias, a_log, d_skip, norm_g, *, batch, seq):
    inner = SSD_GROUPS * SSD_GROUP_W
    main = 2 * inner + 2 * SSD_GROUPS * SSD_STATE
    proj, dt_t = ssd_in_proj(h, g, w_in16, idx, _small_head(w_in, idx, main, True),
                             n=main, tm=TOKEN_TILE, tn=COL_TILE)
    y16, ss = ssd_core(proj.reshape(batch, seq, main), dt_t, conv_w, conv_b, dt_bias, a_log, d_skip, norm_g,
                       batch=batch, seq=seq)
    return rms_matmul_residual(y16.reshape(batch * seq, inner), ss.reshape(batch * seq, -1), w_out16, idx, h,
                               tm=TOKEN_TILE, tn=COL_TILE // 2, name="ssd_out_proj")


def _mlstm_layer(h, g, w_in, w_in16, w_out16, idx, gate_b, head_g, *, batch, seq):
    main = 2 * MLSTM_HEADS * (MLSTM_DQK + MLSTM_DV)
    proj, gates = norm_matmul(h, g, w_in16, idx, n=main, w_small=_small_head(w_in, idx, main, False),
                              tm=TOKEN_TILE, tn=COL_TILE, name="mlstm_in_proj")
    hs = mlstm_core(proj.reshape(batch, seq, main),
                    gates[:, :2 * MLSTM_HEADS].reshape(batch, seq, 2 * MLSTM_HEADS),
                    gate_b, head_g, batch=batch, seq=seq)
    return matmul_residual(hs.reshape(batch * seq, -1), w_out16, idx, h,
                           tm=TOKEN_TILE, tn=COL_TILE, name="mlstm_out_proj")


def _s5_layer(h, g, w_in16, w_out16, idx, b_re, b_im, c_re, c_im, d_skip, log_dt, a_re, a_im, *, batch, seq):
    width = w_in16.shape[2]
    tm = min(TOKEN_TILE, seq)
    nt = seq // tm
    ncol = width // COL_TILE
    u_tm = norm_matmul(h, g, w_in16, idx, n=width, tm=tm, tn=COL_TILE,
                       out_index=lambda i, j: (i % nt, (i // nt) * ncol + j),
                       out_shape=(seq, batch * width), name="s5_in_proj")
    abar_re, abar_im, bbar_re, bbar_im = s5_discretize(log_dt, a_re, a_im, b_re, b_im)
    bw, cw, a_vec = _s5_block_weights(abar_re, abar_im, bbar_re, bbar_im, c_re, c_im)
    y_tm = s5_core(u_tm.reshape(seq, batch, width), bw, cw, a_vec, d_skip, batch=batch, seq=seq)
    return glu_matmul_residual(y_tm.reshape(seq, batch * width), w_out16, idx, h,
                               batch=batch, tm=tm, tn=COL_TILE, name="s5_out_proj")


def kernel(x, norm_mix_g, norm_mlp_g, ssd_w_in, ssd_conv_w, ssd_conv_b, ssd_dt_bias, ssd_a_log, ssd_d, ssd_norm_g, ssd_w_out, mlstm_w_in, mlstm_gate_b, mlstm_head_g, mlstm_w_out, s5_w_in, s5_b_re, s5_b_im, s5_c_re, s5_c_im, s5_d, s5_log_dt, s5_a_re, s5_a_im, s5_w_out, mlp_w1, mlp_w2, final_norm_g):
    batch, seq, d = x.shape
    depth = norm_mix_g.shape[0]
    ssd_in16, ssd_out16 = ssd_w_in.astype(BF16), ssd_w_out.astype(BF16)
    mlstm_in16, mlstm_out16 = mlstm_w_in.astype(BF16), mlstm_w_out.astype(BF16)
    s5_in16, s5_out16 = s5_w_in.astype(BF16), s5_w_out.astype(BF16)
    mlp_w1_16, mlp_w2_16 = mlp_w1.astype(BF16), mlp_w2.astype(BF16)
    h = x.reshape(batch * seq, d)
    for layer in range(depth):
        kind, idx = layer % 3, layer // 3
        g = norm_mix_g[layer]
        if kind == 0:
            h = _ssd_layer(h, g, ssd_w_in, ssd_in16, ssd_out16, idx, ssd_conv_w[idx], ssd_conv_b[idx],
                           ssd_dt_bias[idx], ssd_a_log[idx], ssd_d[idx], ssd_norm_g[idx], batch=batch, seq=seq)
        elif kind == 1:
            h = _mlstm_layer(h, g, mlstm_w_in, mlstm_in16, mlstm_out16, idx, mlstm_gate_b[idx],
                             mlstm_head_g[idx], batch=batch, seq=seq)
        else:
            h = _s5_layer(h, g, s5_in16, s5_out16, idx, s5_b_re[idx], s5_b_im[idx], s5_c_re[idx], s5_c_im[idx],
                          s5_d[idx], s5_log_dt[idx], s5_a_re[idx], s5_a_im[idx], batch=batch, seq=seq)
        h = mlp_block(h, norm_mlp_g[layer], mlp_w1_16, mlp_w2_16, layer, final_norm_g,
                      final_norm=(layer == depth - 1),
                      tm=min(MLP_TOKEN_TILE, batch * seq), tf=MLP_FF_TILE, name=f"mlp_{layer}")
    return h.reshape(batch, seq, d)
```

```python
import functools
import math

import jax
import jax.numpy as jnp
from jax import lax
from jax.experimental import pallas as pl
from jax.experimental.pallas import tpu as pltpu

F32 = jnp.float32
BF16 = jnp.bfloat16

NORM_EPS = 1e-5
LOG2E = math.log2(math.e)
LANES = 128
VMEM_LIMIT_BYTES = 56 * 1024 * 1024

SSD_HEAD_DIM = 64
SSD_GROUPS = 8
SSD_HPG = 8
SSD_STATE = 128
SSD_CONV = 4
SSD_GROUP_W = SSD_HPG * SSD_HEAD_DIM
SSD_CHUNK = 256
CONV_HALO = 8

MLSTM_HEADS = 4
MLSTM_DQK = 256
MLSTM_DV = 512
MLSTM_CHUNK = 256

S5_GROUP = 16
S5_STATE = 64
S5_GROUPS_PER_BLOCK = LANES // S5_GROUP
S5_BLOCK_STATE = S5_GROUPS_PER_BLOCK * S5_STATE
S5_TIME_BLOCK = 128


def _params(*semantics):
    return pltpu.CompilerParams(dimension_semantics=semantics, vmem_limit_bytes=VMEM_LIMIT_BYTES)


def _dot(a, b):
    return jnp.dot(a, b, preferred_element_type=F32)


def _dot_nt(a, b):
    return lax.dot_general(a, b, (((1,), (1,)), ((), ())), preferred_element_type=F32)


def _dot_tn(a, b):
    return lax.dot_general(a, b, (((0,), (0,)), ((), ())), preferred_element_type=F32)


def _split3(x):
    hi = x.astype(BF16)
    r1 = x - hi.astype(F32)
    mid = r1.astype(BF16)
    lo = (r1 - mid.astype(F32)).astype(BF16)
    return hi, mid, lo


def _tri(n, upper):
    r = lax.broadcasted_iota(jnp.int32, (n, n), 0)
    c = lax.broadcasted_iota(jnp.int32, (n, n), 1)
    keep = (r <= c) if upper else (c <= r)
    return jnp.where(keep, 1.0, 0.0).astype(BF16)


def _cumsum_rows(x):
    t = _tri(x.shape[0], upper=False)
    hi, mid, lo = _split3(x)
    return _dot(t, hi) + _dot(t, mid) + _dot(t, lo)


def _cumsum_cols(x):
    t = _tri(x.shape[1], upper=True)
    hi, mid, lo = _split3(x)
    return _dot(hi, t) + _dot(mid, t) + _dot(lo, t)


def _causal(n):
    r = lax.broadcasted_iota(jnp.int32, (n, n), 0)
    c = lax.broadcasted_iota(jnp.int32, (n, n), 1)
    return c <= r


def _softplus(x):
    return jnp.maximum(x, 0.0) + jnp.log1p(jnp.exp(-jnp.abs(x)))


def _sigmoid(x):
    return 1.0 / (1.0 + jnp.exp(-x))


def _silu(x):
    return x * _sigmoid(x)


def _log_sigmoid(x):
    return -_softplus(-x)


def _rms_scale(x, g):
    ms = jnp.mean(x * x, axis=-1, keepdims=True)
    return x * lax.rsqrt(ms + NORM_EPS) * g


def _in_proj_kernel(*refs, normalize, small):
    it = iter(refs)
    x_ref = next(it)
    g_ref = next(it) if normalize else None
    w_ref = next(it)
    ws_ref = next(it) if small else None
    o_ref = next(it)
    os_ref = next(it) if small else None
    xn_ref = next(it) if normalize else x_ref

    if normalize or small:
        @pl.when(pl.program_id(1) == 0)
        def _():
            if normalize:
                xn_ref[...] = _rms_scale(x_ref[...], g_ref[...]).astype(BF16)
            if small == "rows":
                os_ref[...] = _dot(xn_ref[...], ws_ref[...])
            elif small == "cols":
                os_ref[...] = _dot_nt(ws_ref[...], xn_ref[...])

    o_ref[...] = _dot(xn_ref[...], w_ref[...])


def in_proj(x, g, w, layer, *, n, w_small=None, small=None, tm, tn, out_index=None, out_shape=None, name):
    n_tok, k = x.shape
    assert n_tok % tm == 0 and n % tn == 0
    normalize = g is not None
    out_index = out_index or (lambda i, j: (i, j))
    in_specs = [pl.BlockSpec((tm, k), lambda i, j: (i, 0))]
    args = [x]
    if normalize:
        in_specs.append(pl.BlockSpec((1, k), lambda i, j: (0, 0)))
        args.append(g.reshape(1, k).astype(F32))
    in_specs.append(pl.BlockSpec((None, k, tn), lambda i, j: (layer, 0, j)))
    args.append(w)
    out_shapes = [jax.ShapeDtypeStruct(out_shape or (n_tok, n), F32)]
    out_specs = [pl.BlockSpec((tm, tn), out_index)]
    if small:
        in_specs.append(pl.BlockSpec(w_small.shape, lambda i, j: (0, 0)))
        args.append(w_small)
        if small == "cols":
            out_shapes.append(jax.ShapeDtypeStruct((LANES, n_tok), F32))
            out_specs.append(pl.BlockSpec((LANES, tm), lambda i, j: (0, i)))
        else:
            out_shapes.append(jax.ShapeDtypeStruct((n_tok, LANES), F32))
            out_specs.append(pl.BlockSpec((tm, LANES), lambda i, j: (i, 0)))
    outs = pl.pallas_call(
        functools.partial(_in_proj_kernel, normalize=normalize, small=small),
        out_shape=out_shapes, grid=(n_tok // tm, n // tn), in_specs=in_specs, out_specs=out_specs,
        scratch_shapes=[pltpu.VMEM((tm, k), BF16)] if normalize else [],
        compiler_params=_params("parallel", "arbitrary"), name=name)(*args)
    return outs if small else outs[0]


def _matmul_res_kernel(a_ref, w_ref, res_ref, o_ref):
    o_ref[...] = res_ref[...] + _dot(a_ref[...], w_ref[...])


def matmul_residual(a, w, layer, res, *, tm, tn, name):
    t, k = a.shape
    n = w.shape[2]
    return pl.pallas_call(
        _matmul_res_kernel, out_shape=jax.ShapeDtypeStruct((t, n), F32),
        grid=(t // tm, n // tn),
        in_specs=[pl.BlockSpec((tm, k), lambda i, j: (i, 0)),
                  pl.BlockSpec((None, k, tn), lambda i, j: (layer, 0, j)),
                  pl.BlockSpec((tm, tn), lambda i, j: (i, j))],
        out_specs=pl.BlockSpec((tm, tn), lambda i, j: (i, j)),
        compiler_params=_params("parallel", "arbitrary"), name=name)(a, w, res)


def _rms_matmul_res_kernel(a_ref, ss_ref, w_ref, res_ref, o_ref):
    k = a_ref.shape[1]
    ss = ss_ref[...]
    tot = ss[:, :LANES]
    for q in range(1, ss.shape[1] // LANES):
        tot = tot + ss[:, q * LANES:(q + 1) * LANES]
    inv = lax.rsqrt(tot * (1.0 / k) + NORM_EPS)
    y = _dot(a_ref[...], w_ref[...])
    o_ref[...] = res_ref[...] + jnp.concatenate([inv] * (y.shape[1] // LANES), axis=1) * y


def rms_matmul_residual(a, ss, w, layer, res, *, tm, tn, name):
    t, k = a.shape
    n = w.shape[2]
    return pl.pallas_call(
        _rms_matmul_res_kernel, out_shape=jax.ShapeDtypeStruct((t, n), F32),
        grid=(t // tm, n // tn),
        in_specs=[pl.BlockSpec((tm, k), lambda i, j: (i, 0)),
                  pl.BlockSpec((tm, ss.shape[1]), lambda i, j: (i, 0)),
                  pl.BlockSpec((None, k, tn), lambda i, j: (layer, 0, j)),
                  pl.BlockSpec((tm, tn), lambda i, j: (i, j))],
        out_specs=pl.BlockSpec((tm, tn), lambda i, j: (i, j)),
        compiler_params=_params("parallel", "arbitrary"), name=name)(a, ss, w, res)


def _glu_res_kernel(a_ref, wv_ref, wg_ref, res_ref, o_ref):
    a = a_ref[...]
    val = _dot(a, wv_ref[...])
    gate = _dot(a, wg_ref[...])
    o_ref[...] = res_ref[...] + val * _sigmoid(gate)


def glu_matmul_residual(a_tm, w, layer, res, *, batch, tm, tn, name):
    s = a_tm.shape[0]
    k = w.shape[1]
    n = w.shape[2] // 2
    nt = s // tm
    return pl.pallas_call(
        _glu_res_kernel, out_shape=jax.ShapeDtypeStruct((batch * s, n), F32),
        grid=(batch * nt, n // tn),
        in_specs=[pl.BlockSpec((tm, k), lambda i, j: (i % nt, i // nt)),
                  pl.BlockSpec((None, k, tn), lambda i, j: (layer, 0, j)),
                  pl.BlockSpec((None, k, tn), lambda i, j: (layer, 0, j + n // tn)),
                  pl.BlockSpec((tm, tn), lambda i, j: (i, j))],
        out_specs=pl.BlockSpec((tm, tn), lambda i, j: (i, j)),
        compiler_params=_params("parallel", "arbitrary"), name=name)(a_tm, w, w, res)


def _mlp_kernel(x_ref, g_ref, w1_ref, w2_ref, gn_ref, o_ref, *rest, final):
    xn_ref = rest[-1]
    f = pl.program_id(1)

    @pl.when(f == 0)
    def _():
        x = x_ref[...]
        xn_ref[...] = _rms_scale(x, g_ref[...]).astype(BF16)
        o_ref[...] = x

    h1 = jnp.maximum(_dot(xn_ref[...], w1_ref[...]), 0.0)
    o_ref[...] += _dot((h1 * h1).astype(BF16), w2_ref[...])

    @pl.when(f == pl.num_programs(1) - 1)
    def _():
        normed = _rms_scale(o_ref[...], gn_ref[...])
        if final:
            o_ref[...] = normed
        else:
            rest[0][...] = normed.astype(BF16)


def mlp_block(x, g, w1, w2, layer, g_next, *, final, tm, tf, name):
    t, d = x.shape
    dff = w1.shape[2]
    row_spec = pl.BlockSpec((tm, d), lambda i, f: (i, 0))
    out_shape = [jax.ShapeDtypeStruct((t, d), F32)] + ([] if final else [jax.ShapeDtypeStruct((t, d), BF16)])
    outs = pl.pallas_call(
        functools.partial(_mlp_kernel, final=final),
        out_shape=out_shape, grid=(t // tm, dff // tf),
        in_specs=[row_spec,
                  pl.BlockSpec((1, d), lambda i, f: (0, 0)),
                  pl.BlockSpec((None, d, tf), lambda i, f: (layer, 0, f)),
                  pl.BlockSpec((None, tf, d), lambda i, f: (layer, f, 0)),
                  pl.BlockSpec((1, d), lambda i, f: (0, 0))],
        out_specs=[row_spec] * len(out_shape),
        scratch_shapes=[pltpu.VMEM((tm, d), BF16)],
        compiler_params=_params("parallel", "arbitrary"), name=name,
    )(x, g.reshape(1, d), w1, w2, g_next.reshape(1, d))
    return outs[0] if final else outs


def _causal_conv_silu(buf_ref, cur, w, b):
    l = cur.shape[0]
    buf_ref[CONV_HALO:CONV_HALO + l, :] = cur
    acc = b + w[SSD_CONV - 1:SSD_CONV, :] * cur
    for k in range(SSD_CONV - 1):
        start = CONV_HALO - (SSD_CONV - 1) + k
        acc = acc + w[k:k + 1, :] * buf_ref[start:start + l, :]
    buf_ref[0:CONV_HALO, :] = cur[l - CONV_HALO:, :]
    return _silu(acc)


def _dot3_tn(x, r):
    hi, mid, lo = _split3(x)
    return _dot_tn(hi, r) + _dot_tn(mid, r) + _dot_tn(lo, r)


def _ssd_head_rows(dt_raw_t, head_p, rows_ref):
    dtt = _softplus(dt_raw_t + head_p[:, 0:1])
    rows_ref[0:SSD_HPG, :] = _cumsum_cols(dtt * (-LOG2E * jnp.exp(head_p[:, 1:2])))
    rows_ref[SSD_HPG:, :] = dtt


def _ssd_kernel(z_ref, x_ref, b_ref, c_ref, dtt_ref, dttn_ref, cp_ref, hp_ref, r_ref,
                o_ref, ss_ref, xbuf, bbuf, cbuf, state_ref, rows_ref, wide_ref, xs_ref, bm_ref, cm_ref):
    first = pl.program_id(2) == 0
    l = x_ref.shape[1]
    sub = LANES
    nsub = l // sub
    hw = SSD_HEAD_DIM
    gw, ns = SSD_GROUP_W, SSD_STATE
    n_a = SSD_HPG * sub + gw

    @pl.when(first)
    def _():
        state_ref[...] = jnp.zeros_like(state_ref)
        for buf in (xbuf, bbuf, cbuf):
            buf[0:CONV_HALO, :] = jnp.zeros((CONV_HALO, buf.shape[1]), F32)
        _ssd_head_rows(dtt_ref[...], hp_ref[0], rows_ref)

    acst = rows_ref[0:SSD_HPG, :]
    wide_ref[...] = _dot_tn(jnp.concatenate(_split3(rows_ref[...]), axis=0), r_ref[...])

    cp = cp_ref[0]
    cw, cbias = cp[0:SSD_CONV, :], cp[SSD_CONV:SSD_CONV + 1, :]
    xs_ref[...] = _causal_conv_silu(xbuf, x_ref[0], cw[:, :gw], cbias[:, :gw])
    bm_ref[...] = _causal_conv_silu(bbuf, b_ref[0], cw[:, gw:gw + ns], cbias[:, gw:gw + ns]).astype(BF16)
    cm_ref[...] = _causal_conv_silu(cbuf, c_ref[0], cw[:, gw + ns:], cbias[:, gw + ns:]).astype(BF16)

    _ssd_head_rows(dttn_ref[...], hp_ref[0], rows_ref)

    xs, bm16, cm16 = xs_ref[...], bm_ref[...], cm_ref[...]
    acs_blk = wide_ref[:, :SSD_HPG * sub]
    acs64 = wide_ref[:, SSD_HPG * sub:n_a]
    dt64 = wide_ref[:, n_a:]
    last64 = acs64[l - 1:l, :]

    xdt = xs * dt64
    xdt16 = xdt.astype(BF16)
    st = state_ref[...]
    y = _dot(cm16, st.astype(BF16)) * jnp.exp2(acs64)
    state_ref[...] = st * jnp.exp2(last64) + _dot_tn(bm16, (xdt * jnp.exp2(last64 - acs64)).astype(BF16))
    y = y + cp[SSD_CONV + 1:SSD_CONV + 2, :gw] * xs

    cb = _dot_nt(cm16, bm16)
    diag_mask = _causal(sub)
    cb_blk = [[cb[r * sub:(r + 1) * sub, c * sub:(c + 1) * sub] if c < r else
               jnp.where(diag_mask, cb[r * sub:(r + 1) * sub, c * sub:(c + 1) * sub], 0.0)
               for c in range(r + 1)] for r in range(nsub)]
    lane = lax.broadcasted_iota(jnp.int32, (l, sub), 1)
    pair_out = []
    for p in range(SSD_HPG // 2):
        xp = xdt16[:, p * sub:(p + 1) * sub]
        rhs = [jnp.where(lane < hw, xp, jnp.zeros_like(xp)), jnp.where(lane >= hw, xp, jnp.zeros_like(xp))]
        rows = []
        for r in range(nsub):
            lhs_parts, rhs_parts = [], []
            for h in range(2):
                j = 2 * p + h
                col = acs_blk[r * sub:(r + 1) * sub, j * sub:(j + 1) * sub]
                for c in range(r + 1):
                    seg = col - acst[j:j + 1, c * sub:(c + 1) * sub]
                    if c == r:
                        seg = jnp.minimum(seg, 0.0)
                    lhs_parts.append((cb_blk[r][c] * jnp.exp2(seg)).astype(BF16))
                rhs_parts.append(rhs[h][:(r + 1) * sub, :])
            rows.append(_dot(jnp.concatenate(lhs_parts, axis=1), jnp.concatenate(rhs_parts, axis=0)))
        pair_out.append(jnp.concatenate(rows, axis=0))
    y = y + jnp.concatenate(pair_out, axis=1)
    y = y * _silu(z_ref[0])
    ss_ref[0] = jnp.broadcast_to(jnp.sum(y * y, axis=1, keepdims=True), (l, sub))
    o_ref[0] = (y * cp[SSD_CONV + 2:SSD_CONV + 3, :gw]).astype(o_ref.dtype)


def _head_expand(lanes_per_head, n_heads):
    return jnp.repeat(jnp.eye(n_heads, dtype=F32), lanes_per_head, axis=1)


def ssd_core(proj, dt_t, conv_w, conv_b, dt_bias, a_log, d_skip, norm_g, *, batch, seq):
    l = min(SSD_CHUNK, seq)
    nc = seq // l
    g_, j_ = SSD_GROUPS, SSD_HPG
    inner = g_ * SSD_GROUP_W
    nx = inner // SSD_GROUP_W
    nb = 2 * inner // SSD_STATE
    expand64 = _head_expand(SSD_HEAD_DIM, j_)
    r16 = jnp.concatenate([
        jnp.concatenate([_head_expand(LANES, j_), expand64, jnp.zeros_like(expand64)], axis=1),
        jnp.concatenate([jnp.zeros((j_, j_ * LANES + SSD_GROUP_W), F32), expand64], axis=1)], axis=0)
    r = jnp.concatenate([r16, r16, r16], axis=0).astype(BF16)
    bc_w = g_ * SSD_STATE

    def per_group(p, rows):
        parts = [p[:, :inner].reshape(rows, g_, SSD_GROUP_W), p[:, inner:inner + bc_w].reshape(rows, g_, SSD_STATE),
                 p[:, inner + bc_w:].reshape(rows, g_, SSD_STATE)]
        return jnp.concatenate(parts, axis=2).transpose(1, 0, 2)

    cp_w = SSD_GROUP_W + 2 * SSD_STATE
    x_only = ((0, 0), (0, 0), (0, cp_w - SSD_GROUP_W))
    dskip = jnp.pad(jnp.repeat(d_skip.reshape(g_, 1, j_), SSD_HEAD_DIM, axis=2), x_only)
    gain = jnp.pad(norm_g.reshape(g_, 1, SSD_GROUP_W), x_only)
    conv_p = jnp.concatenate([per_group(conv_w, SSD_CONV), per_group(conv_b.reshape(1, -1), 1), dskip, gain,
                              jnp.zeros((g_, 8 - SSD_CONV - 3, cp_w), F32)], axis=1).astype(F32)
    head_p = jnp.pad(jnp.stack([dt_bias.reshape(g_, j_), a_log.reshape(g_, j_)], axis=2),
                     ((0, 0), (0, 0), (0, LANES - 2))).astype(F32)
    in_specs = [
        pl.BlockSpec((1, l, SSD_GROUP_W), lambda b, g, c: (b, c, g)),
        pl.BlockSpec((1, l, SSD_GROUP_W), lambda b, g, c: (b, c, nx + g)),
        pl.BlockSpec((1, l, SSD_STATE), lambda b, g, c: (b, c, nb + g)),
        pl.BlockSpec((1, l, SSD_STATE), lambda b, g, c: (b, c, nb + g_ + g)),
        pl.BlockSpec((j_, l), lambda b, g, c: (g, b * nc + c)),
        pl.BlockSpec((j_, l), lambda b, g, c: (g, b * nc + jnp.minimum(c + 1, nc - 1))),
        pl.BlockSpec((1, 8, cp_w), lambda b, g, c: (g, 0, 0)),
        pl.BlockSpec((1, 8, LANES), lambda b, g, c: (g, 0, 0)),
        pl.BlockSpec(r.shape, lambda b, g, c: (0, 0)),
    ]
    return pl.pallas_call(
        _ssd_kernel,
        out_shape=[jax.ShapeDtypeStruct((batch, seq, inner), BF16),
                   jax.ShapeDtypeStruct((batch, seq, g_ * LANES), F32)],
        grid=(batch, g_, nc), in_specs=in_specs,
        out_specs=[pl.BlockSpec((1, l, SSD_GROUP_W), lambda b, g, c: (b, c, g)),
                   pl.BlockSpec((1, l, LANES), lambda b, g, c: (b, c, g))],
        scratch_shapes=[pltpu.VMEM((CONV_HALO + l, SSD_GROUP_W), F32),
                        pltpu.VMEM((CONV_HALO + l, SSD_STATE), F32),
                        pltpu.VMEM((CONV_HALO + l, SSD_STATE), F32),
                        pltpu.VMEM((SSD_STATE, SSD_GROUP_W), F32),
                        pltpu.VMEM((2 * j_, l), F32),
                        pltpu.VMEM((l, r.shape[1]), F32),
                        pltpu.VMEM((l, SSD_GROUP_W), F32),
                        pltpu.VMEM((l, SSD_STATE), BF16),
                        pltpu.VMEM((l, SSD_STATE), BF16)],
        compiler_params=_params("parallel", "parallel", "arbitrary"), name="ssd_core",
    )(proj, proj, proj, proj, dt_t, dt_t, conv_p, head_p, r)


def _mlstm_kernel(q_ref, k_ref, v_ref, o_ref, gc_ref, gr_ref, gbc_ref, gbr_ref, hg_ref,
                  out_ref, c_ref, n_ref, m_ref):
    l = q_ref.shape[1]

    @pl.when(pl.program_id(2) == 0)
    def _():
        c_ref[...] = jnp.zeros_like(c_ref)
        n_ref[...] = jnp.zeros_like(n_ref)
        m_ref[...] = jnp.zeros_like(m_ref)

    q = q_ref[0] * (MLSTM_DQK ** -0.5)
    k = k_ref[0]
    v16 = v_ref[0].astype(BF16)
    q16 = q.astype(BF16)
    gcol = gc_ref[0, 0] + gbc_ref[0]
    grow = gr_ref[0, 0] + gbr_ref[0]
    i_col, lf_col = gcol[:, 0:1], _log_sigmoid(gcol[:, 1:2])
    i_row, lf_row = grow[0:1, :], _log_sigmoid(grow[1:2, :])
    bcum = _cumsum_rows(lf_col)
    bcum_row = _cumsum_cols(lf_row)
    m_st = m_ref[0:1, 0:1]

    causal = _causal(l)
    intra = jnp.where(causal, bcum - bcum_row + i_row, -jnp.inf)
    inter = bcum + m_st
    m_t = jnp.maximum(inter, jnp.max(intra, axis=1, keepdims=True))
    w = jnp.exp(intra - m_t)
    scale_inter = jnp.exp(inter - m_t)
    qk = _dot_nt(q16, k.astype(BF16)) * w
    num = _dot(qk.astype(BF16), v16) + scale_inter * _dot(q16, c_ref[...].astype(BF16))
    qn = jnp.sum(q * n_ref[...], axis=1, keepdims=True)
    den = jnp.sum(qk, axis=1, keepdims=True) + scale_inter * qn
    h = num / jnp.maximum(jnp.abs(den), jnp.exp(-m_t))

    b_last = bcum[l - 1:l, :]
    tail = b_last - bcum + i_col
    m_new = jnp.maximum(b_last + m_st, jnp.max(tail, axis=0, keepdims=True))
    carry_scale = jnp.exp(b_last + m_st - m_new)
    wk = k * jnp.exp(tail - m_new)
    c_ref[...] = carry_scale * c_ref[...] + _dot_tn(wk.astype(BF16), v16)
    n_ref[...] = carry_scale * n_ref[...] + jnp.sum(wk, axis=0, keepdims=True)
    m_ref[...] = jnp.broadcast_to(m_new, m_ref.shape)

    hn = _rms_scale(h, hg_ref[...])
    out_ref[0] = (_sigmoid(o_ref[0]) * hn).astype(out_ref.dtype)


def mlstm_core(proj, gates, gate_b, head_g, *, batch, seq):
    l = MLSTM_CHUNK
    h_ = MLSTM_HEADS
    qk_w = h_ * MLSTM_DQK
    v_w = h_ * MLSTM_DV
    gsplit = jnp.stack([gates[..., :h_], gates[..., h_:2 * h_]], axis=-1)
    gcol = gsplit.transpose(0, 2, 1, 3)
    grow = gsplit.transpose(0, 2, 3, 1)
    gb = jnp.stack([gate_b[:h_], gate_b[h_:]], axis=-1).astype(F32)
    vo = 2 * qk_w // MLSTM_DV
    return pl.pallas_call(
        _mlstm_kernel, out_shape=jax.ShapeDtypeStruct((batch, seq, v_w), BF16),
        grid=(batch, h_, seq // l),
        in_specs=[pl.BlockSpec((1, l, MLSTM_DQK), lambda b, h, c: (b, c, h)),
                  pl.BlockSpec((1, l, MLSTM_DQK), lambda b, h, c: (b, c, h_ + h)),
                  pl.BlockSpec((1, l, MLSTM_DV), lambda b, h, c: (b, c, vo + h)),
                  pl.BlockSpec((1, l, MLSTM_DV), lambda b, h, c: (b, c, vo + h_ + h)),
                  pl.BlockSpec((1, 1, l, 2), lambda b, h, c: (b, h, c, 0)),
                  pl.BlockSpec((1, 1, 2, l), lambda b, h, c: (b, h, 0, c)),
                  pl.BlockSpec((1, 1, 2), lambda b, h, c: (h, 0, 0)),
                  pl.BlockSpec((1, 2, 1), lambda b, h, c: (h, 0, 0)),
                  pl.BlockSpec((1, MLSTM_DV), lambda b, h, c: (0, h))],
        out_specs=pl.BlockSpec((1, l, MLSTM_DV), lambda b, h, c: (b, c, h)),
        scratch_shapes=[pltpu.VMEM((MLSTM_DQK, MLSTM_DV), F32),
                        pltpu.VMEM((1, MLSTM_DQK), F32),
                        pltpu.VMEM((8, LANES), F32)],
        compiler_params=_params("parallel", "parallel", "arbitrary"), name="mlstm_core",
    )(proj, proj, proj, proj, gcol, grow, gb.reshape(h_, 1, 2), gb.reshape(h_, 2, 1),
      head_g.reshape(1, v_w).astype(F32))


def _s5_discretize_kernel(logdt_ref, are_ref, aim_ref, bre_ref, bim_ref,
                          abre_ref, abim_ref, bbre_ref, bbim_ref):
    ar, ai = are_ref[...], aim_ref[...]
    dt = jnp.exp(logdt_ref[...])
    mag = jnp.exp(ar * dt)
    abar_re, abar_im = mag * jnp.cos(ai * dt), mag * jnp.sin(ai * dt)
    den = ar * ar + ai * ai
    zoh_re = ((abar_re - 1.0) * ar + abar_im * ai) / den
    zoh_im = (abar_im * ar - (abar_re - 1.0) * ai) / den
    abre_ref[...] = abar_re
    abim_ref[...] = abar_im
    for c in range(S5_GROUP):
        br, bi = bre_ref[c], bim_ref[c]
        bbre_ref[c] = zoh_re * br - zoh_im * bi
        bbim_ref[c] = zoh_re * bi + zoh_im * br


def s5_discretize(log_dt, a_re, a_im, b_re, b_im):
    g_, p_ = a_re.shape
    sds = jax.ShapeDtypeStruct
    return pl.pallas_call(
        _s5_discretize_kernel,
        out_shape=[sds((g_, p_), F32), sds((g_, p_), F32),
                   sds((S5_GROUP, g_, p_), F32), sds((S5_GROUP, g_, p_), F32)],
        name="s5_discretize",
    )(log_dt.reshape(g_, 1).astype(F32), a_re.astype(F32), a_im.astype(F32),
      b_re.astype(F32).transpose(2, 0, 1), b_im.astype(F32).transpose(2, 0, 1))


def _s5_kernel(u_ref, bw_ref, cw_ref, a_ref, d_ref, y_ref, st_ref, bu_ref, *, batch):
    ts = u_ref.shape[0]
    half = S5_BLOCK_STATE

    @pl.when(pl.program_id(1) == 0)
    def _():
        st_ref[...] = jnp.zeros_like(st_ref)

    u = u_ref[...].reshape(ts * batch, LANES)
    bu_ref[...] = _dot(u.astype(BF16), bw_ref[0])
    a = a_ref[0]
    a_re = jnp.broadcast_to(a[:, :half], (batch, half))
    a_im = jnp.broadcast_to(a[:, half:], (batch, half))

    def step(t, carry):
        s_re, s_im = carry
        rows = pl.ds(pl.multiple_of(t * batch, batch), batch)
        n_re = a_re * s_re - a_im * s_im + bu_ref[rows, 0:half]
        n_im = a_re * s_im + a_im * s_re + bu_ref[rows, half:2 * half]
        bu_ref[rows, 0:half] = n_re
        bu_ref[rows, half:2 * half] = n_im
        return n_re, n_im

    s_re, s_im = lax.fori_loop(0, ts, step, (st_ref[:, 0:half], st_ref[:, half:2 * half]))
    st_ref[:, 0:half] = s_re
    st_ref[:, half:2 * half] = s_im

    y = _dot(bu_ref[...].astype(BF16), cw_ref[0]) + d_ref[...] * u
    y_ref[...] = jax.nn.gelu(y).astype(y_ref.dtype).reshape(ts, batch, LANES)


def s5_core(u_tm, bw, cw, a_vec, d_skip, *, batch, seq):
    width = u_tm.shape[2]
    nblk = width // LANES
    ts = S5_TIME_BLOCK
    return pl.pallas_call(
        functools.partial(_s5_kernel, batch=batch),
        out_shape=jax.ShapeDtypeStruct((seq, batch, width), BF16),
        grid=(nblk, seq // ts),
        in_specs=[pl.BlockSpec((ts, batch, LANES), lambda j, i: (i, 0, j)),
                  pl.BlockSpec((1, LANES, 2 * S5_BLOCK_STATE), lambda j, i: (j, 0, 0)),
                  pl.BlockSpec((1, 2 * S5_BLOCK_STATE, LANES), lambda j, i: (j, 0, 0)),
                  pl.BlockSpec((1, 1, 2 * S5_BLOCK_STATE), lambda j, i: (j, 0, 0)),
                  pl.BlockSpec((1, LANES), lambda j, i: (0, j))],
        out_specs=pl.BlockSpec((ts, batch, LANES), lambda j, i: (i, 0, j)),
        scratch_shapes=[pltpu.VMEM((batch, 2 * S5_BLOCK_STATE), F32),
                        pltpu.VMEM((ts * batch, 2 * S5_BLOCK_STATE), F32)],
        compiler_params=_params("parallel", "arbitrary"), name="s5_core",
    )(u_tm, bw, cw, a_vec, d_skip.reshape(1, width).astype(F32))


def _s5_block_weights(abar_re, abar_im, bbar_re, bbar_im, c_re, c_im):
    g_, p_ = abar_re.shape
    gb = S5_GROUPS_PER_BLOCK
    nblk = g_ // gb
    eye = jnp.eye(gb, dtype=F32)

    def in_blocks(bbar):
        b = bbar.reshape(S5_GROUP, nblk, gb, p_).transpose(1, 2, 0, 3)
        return jnp.einsum("ngcp,gh->ngchp", b, eye).reshape(nblk, gb * S5_GROUP, gb * p_)

    def out_blocks(cmat):
        c = cmat.reshape(nblk, gb, S5_GROUP, p_).transpose(0, 1, 3, 2)
        return jnp.einsum("ngpc,gh->ngphc", c, eye).reshape(nblk, gb * p_, gb * S5_GROUP)

    bw = jnp.concatenate([in_blocks(bbar_re), in_blocks(bbar_im)], axis=2).astype(BF16)
    cw = jnp.concatenate([out_blocks(c_re.astype(F32)), out_blocks(-c_im.astype(F32))], axis=1).astype(BF16)
    a_vec = jnp.concatenate([abar_re.reshape(nblk, 1, gb * p_), abar_im.reshape(nblk, 1, gb * p_)], axis=2)
    return bw, cw, a_vec


TOKEN_TILE = 1024
COL_TILE = 1024
MLP_TOKEN_TILE = 1024
MLP_FF_TILE = 512


def _small_head(w_in, idx, main, transposed):
    w = jnp.pad(w_in[idx, :, main:], ((0, 0), (0, LANES - (w_in.shape[2] - main)))).astype(BF16)
    return w.T if transposed else w


def _ssd_layer(h, xin, g, w_in, w_in16, w_out16, idx, conv_w, conv_b, dt_bias, a_log, d_skip, norm_g, *, batch, seq):
    inner = SSD_GROUPS * SSD_GROUP_W
    main = 2 * inner + 2 * SSD_GROUPS * SSD_STATE
    proj, dt_t = in_proj(xin, g, w_in16, idx, n=main, w_small=_small_head(w_in, idx, main, True), small="cols",
                         tm=TOKEN_TILE, tn=COL_TILE, name="ssd_in_proj")
    y16, ss = ssd_core(proj.reshape(batch, seq, main), dt_t, conv_w, conv_b, dt_bias, a_log, d_skip, norm_g,
                       batch=batch, seq=seq)
    return rms_matmul_residual(y16.reshape(batch * seq, inner), ss.reshape(batch * seq, -1), w_out16, idx, h,
                               tm=TOKEN_TILE, tn=COL_TILE // 2, name="ssd_out_proj")


def _mlstm_layer(h, xin, g, w_in, w_in16, w_out16, idx, gate_b, head_g, *, batch, seq):
    main = 2 * MLSTM_HEADS * (MLSTM_DQK + MLSTM_DV)
    proj, gates = in_proj(xin, g, w_in16, idx, n=main, w_small=_small_head(w_in, idx, main, False), small="rows",
                          tm=TOKEN_TILE, tn=COL_TILE, name="mlstm_in_proj")
    hs = mlstm_core(proj.reshape(batch, seq, main),
                    gates[:, :2 * MLSTM_HEADS].reshape(batch, seq, 2 * MLSTM_HEADS),
                    gate_b, head_g, batch=batch, seq=seq)
    return matmul_residual(hs.reshape(batch * seq, -1), w_out16, idx, h,
                           tm=TOKEN_TILE, tn=COL_TILE, name="mlstm_out_proj")


def _s5_layer(h, xin, g, w_in16, w_out16, idx, b_re, b_im, c_re, c_im, d_skip, log_dt, a_re, a_im, *, batch, seq):
    width = w_in16.shape[2]
    tm = min(TOKEN_TILE, seq)
    nt = seq // tm
    ncol = width // COL_TILE
    u_tm = in_proj(xin, g, w_in16, idx, n=width, tm=tm, tn=COL_TILE,
                   out_index=lambda i, j: (i % nt, (i // nt) * ncol + j),
                   out_shape=(seq, batch * width), name="s5_in_proj")
    abar_re, abar_im, bbar_re, bbar_im = s5_discretize(log_dt, a_re, a_im, b_re, b_im)
    bw, cw, a_vec = _s5_block_weights(abar_re, abar_im, bbar_re, bbar_im, c_re, c_im)
    y_tm = s5_core(u_tm.reshape(seq, batch, width), bw, cw, a_vec, d_skip, batch=batch, seq=seq)
    return glu_matmul_residual(y_tm.reshape(seq, batch * width), w_out16, idx, h,
                               batch=batch, tm=tm, tn=COL_TILE, name="s5_out_proj")


def kernel(x, norm_mix_g, norm_mlp_g, ssd_w_in, ssd_conv_w, ssd_conv_b, ssd_dt_bias, ssd_a_log, ssd_d, ssd_norm_g, ssd_w_out, mlstm_w_in, mlstm_gate_b, mlstm_head_g, mlstm_w_out, s5_w_in, s5_b_re, s5_b_im, s5_c_re, s5_c_im, s5_d, s5_log_dt, s5_a_re, s5_a_im, s5_w_out, mlp_w1, mlp_w2, final_norm_g):
    batch, seq, d = x.shape
    depth = norm_mix_g.shape[0]
    ssd_in16, ssd_out16 = ssd_w_in.astype(BF16), ssd_w_out.astype(BF16)
    mlstm_in16, mlstm_out16 = mlstm_w_in.astype(BF16), mlstm_w_out.astype(BF16)
    s5_in16, s5_out16 = s5_w_in.astype(BF16), s5_w_out.astype(BF16)
    mlp_w1_16, mlp_w2_16 = mlp_w1.astype(BF16), mlp_w2.astype(BF16)
    h = x.reshape(batch * seq, d)
    xin, g = h, norm_mix_g[0]
    for layer in range(depth):
        kind, idx = layer % 3, layer // 3
        if kind == 0:
            h = _ssd_layer(h, xin, g, ssd_w_in, ssd_in16, ssd_out16, idx, ssd_conv_w[idx], ssd_conv_b[idx],
                           ssd_dt_bias[idx], ssd_a_log[idx], ssd_d[idx], ssd_norm_g[idx], batch=batch, seq=seq)
        elif kind == 1:
            h = _mlstm_layer(h, xin, g, mlstm_w_in, mlstm_in16, mlstm_out16, idx, mlstm_gate_b[idx],
                             mlstm_head_g[idx], batch=batch, seq=seq)
        else:
            h = _s5_layer(h, xin, g, s5_in16, s5_out16, idx, s5_b_re[idx], s5_b_im[idx], s5_c_re[idx], s5_c_im[idx],
                          s5_d[idx], s5_log_dt[idx], s5_a_re[idx], s5_a_im[idx], batch=batch, seq=seq)
        final = layer == depth - 1
        out = mlp_block(h, norm_mlp_g[layer], mlp_w1_16, mlp_w2_16, layer,
                        final_norm_g if final else norm_mix_g[layer + 1], final=final,
                        tm=min(MLP_TOKEN_TILE, batch * seq), tf=MLP_FF_TILE, name=f"mlp_{layer}")
        if final:
            h = out
        else:
            h, xin = out
            g = None
    return h.reshape(batch, seq, d)
```

```python
import functools
import math

import jax
import jax.numpy as jnp
from jax import lax
from jax.experimental import pallas as pl
from jax.experimental.pallas import tpu as pltpu

F32 = jnp.float32
BF16 = jnp.bfloat16

NORM_EPS = 1e-5
LOG2E = math.log2(math.e)
LANES = 128
VMEM_LIMIT_BYTES = 56 * 1024 * 1024

SSD_HEAD_DIM = 64
SSD_GROUPS = 8
SSD_HPG = 8
SSD_STATE = 128
SSD_CONV = 4
SSD_GROUP_W = SSD_HPG * SSD_HEAD_DIM
SSD_CHUNK = 256
SSD_GROUPS_PER_STEP = 4
CONV_HALO = 8

MLSTM_HEADS = 4
MLSTM_DQK = 256
MLSTM_DV = 512
MLSTM_CHUNK = 256

S5_GROUP = 16
S5_STATE = 64
S5_GROUPS_PER_BLOCK = LANES // S5_GROUP
S5_BLOCK_STATE = S5_GROUPS_PER_BLOCK * S5_STATE
S5_TIME_BLOCK = 128


def _params(*semantics):
    return pltpu.CompilerParams(dimension_semantics=semantics, vmem_limit_bytes=VMEM_LIMIT_BYTES)


def _dot(a, b):
    return jnp.dot(a, b, preferred_element_type=F32)


def _dot_nt(a, b):
    return lax.dot_general(a, b, (((1,), (1,)), ((), ())), preferred_element_type=F32)


def _dot_tn(a, b):
    return lax.dot_general(a, b, (((0,), (0,)), ((), ())), preferred_element_type=F32)


def _split3(x):
    hi = x.astype(BF16)
    r1 = x - hi.astype(F32)
    mid = r1.astype(BF16)
    lo = (r1 - mid.astype(F32)).astype(BF16)
    return hi, mid, lo


def _tri(n, upper):
    r = lax.broadcasted_iota(jnp.int32, (n, n), 0)
    c = lax.broadcasted_iota(jnp.int32, (n, n), 1)
    keep = (r <= c) if upper else (c <= r)
    return jnp.where(keep, 1.0, 0.0).astype(BF16)


def _cumsum_rows(x):
    t = _tri(x.shape[0], upper=False)
    hi, mid, lo = _split3(x)
    return _dot(t, hi) + _dot(t, mid) + _dot(t, lo)


def _cumsum_cols(x):
    t = _tri(x.shape[1], upper=True)
    hi, mid, lo = _split3(x)
    return _dot(hi, t) + _dot(mid, t) + _dot(lo, t)


def _causal(n):
    r = lax.broadcasted_iota(jnp.int32, (n, n), 0)
    c = lax.broadcasted_iota(jnp.int32, (n, n), 1)
    return c <= r


def _softplus(x):
    return jnp.maximum(x, 0.0) + jnp.log1p(jnp.exp(-jnp.abs(x)))


def _sigmoid(x):
    return 1.0 / (1.0 + jnp.exp(-x))


def _silu(x):
    return x * _sigmoid(x)


def _log_sigmoid(x):
    return -_softplus(-x)


def _rms_scale(x, g):
    ms = jnp.mean(x * x, axis=-1, keepdims=True)
    return x * lax.rsqrt(ms + NORM_EPS) * g


def _in_proj_kernel(*refs, normalize, small):
    it = iter(refs)
    x_ref = next(it)
    g_ref = next(it) if normalize else None
    w_ref = next(it)
    ws_ref = next(it) if small else None
    o_ref = next(it)
    os_ref = next(it) if small else None
    xn_ref = next(it) if normalize else x_ref

    if normalize or small:
        @pl.when(pl.program_id(1) == 0)
        def _():
            if normalize:
                xn_ref[...] = _rms_scale(x_ref[...], g_ref[...]).astype(BF16)
            if small == "rows":
                os_ref[...] = _dot(xn_ref[...], ws_ref[...])
            elif small == "cols":
                os_ref[...] = _dot_nt(ws_ref[...], xn_ref[...])

    o_ref[...] = _dot(xn_ref[...], w_ref[...])


def in_proj(x, g, w, layer, *, n, w_small=None, small=None, tm, tn, out_index=None, out_shape=None, name):
    n_tok, k = x.shape
    assert n_tok % tm == 0 and n % tn == 0
    normalize = g is not None
    out_index = out_index or (lambda i, j: (i, j))
    in_specs = [pl.BlockSpec((tm, k), lambda i, j: (i, 0))]
    args = [x]
    if normalize:
        in_specs.append(pl.BlockSpec((1, k), lambda i, j: (0, 0)))
        args.append(g.reshape(1, k).astype(F32))
    in_specs.append(pl.BlockSpec((None, k, tn), lambda i, j: (layer, 0, j)))
    args.append(w)
    out_shapes = [jax.ShapeDtypeStruct(out_shape or (n_tok, n), F32)]
    out_specs = [pl.BlockSpec((tm, tn), out_index)]
    if small:
        in_specs.append(pl.BlockSpec(w_small.shape, lambda i, j: (0, 0)))
        args.append(w_small)
        if small == "cols":
            out_shapes.append(jax.ShapeDtypeStruct((LANES, n_tok), F32))
            out_specs.append(pl.BlockSpec((LANES, tm), lambda i, j: (0, i)))
        else:
            out_shapes.append(jax.ShapeDtypeStruct((n_tok, LANES), F32))
            out_specs.append(pl.BlockSpec((tm, LANES), lambda i, j: (i, 0)))
    outs = pl.pallas_call(
        functools.partial(_in_proj_kernel, normalize=normalize, small=small),
        out_shape=out_shapes, grid=(n_tok // tm, n // tn), in_specs=in_specs, out_specs=out_specs,
        scratch_shapes=[pltpu.VMEM((tm, k), BF16)] if normalize else [],
        compiler_params=_params("parallel", "arbitrary"), name=name)(*args)
    return outs if small else outs[0]


def _matmul_res_kernel(a_ref, w_ref, res_ref, o_ref):
    o_ref[...] = res_ref[...] + _dot(a_ref[...], w_ref[...])


def matmul_residual(a, w, layer, res, *, tm, tn, name):
    t, k = a.shape
    n = w.shape[2]
    return pl.pallas_call(
        _matmul_res_kernel, out_shape=jax.ShapeDtypeStruct((t, n), F32),
        grid=(t // tm, n // tn),
        in_specs=[pl.BlockSpec((tm, k), lambda i, j: (i, 0)),
                  pl.BlockSpec((None, k, tn), lambda i, j: (layer, 0, j)),
                  pl.BlockSpec((tm, tn), lambda i, j: (i, j))],
        out_specs=pl.BlockSpec((tm, tn), lambda i, j: (i, j)),
        compiler_params=_params("parallel", "arbitrary"), name=name)(a, w, res)


def _rms_matmul_res_kernel(a_ref, ss_ref, w_ref, res_ref, o_ref):
    k = a_ref.shape[1]
    ss = ss_ref[...]
    tot = ss[:, :LANES]
    for q in range(1, ss.shape[1] // LANES):
        tot = tot + ss[:, q * LANES:(q + 1) * LANES]
    inv = lax.rsqrt(tot * (1.0 / k) + NORM_EPS)
    y = _dot(a_ref[...], w_ref[...])
    o_ref[...] = res_ref[...] + jnp.concatenate([inv] * (y.shape[1] // LANES), axis=1) * y


def rms_matmul_residual(a, ss, w, layer, res, *, tm, tn, name):
    t, k = a.shape
    n = w.shape[2]
    return pl.pallas_call(
        _rms_matmul_res_kernel, out_shape=jax.ShapeDtypeStruct((t, n), F32),
        grid=(t // tm, n // tn),
        in_specs=[pl.BlockSpec((tm, k), lambda i, j: (i, 0)),
                  pl.BlockSpec((tm, ss.shape[1]), lambda i, j: (i, 0)),
                  pl.BlockSpec((None, k, tn), lambda i, j: (layer, 0, j)),
                  pl.BlockSpec((tm, tn), lambda i, j: (i, j))],
        out_specs=pl.BlockSpec((tm, tn), lambda i, j: (i, j)),
        compiler_params=_params("parallel", "arbitrary"), name=name)(a, ss, w, res)


def _glu_res_kernel(a_ref, wv_ref, wg_ref, res_ref, o_ref):
    a = a_ref[...]
    val = _dot(a, wv_ref[...])
    gate = _dot(a, wg_ref[...])
    o_ref[...] = res_ref[...] + val * _sigmoid(gate)


def glu_matmul_residual(a_tm, w, layer, res, *, batch, tm, tn, name):
    s = a_tm.shape[0]
    k = w.shape[1]
    n = w.shape[2] // 2
    nt = s // tm
    return pl.pallas_call(
        _glu_res_kernel, out_shape=jax.ShapeDtypeStruct((batch * s, n), F32),
        grid=(batch * nt, n // tn),
        in_specs=[pl.BlockSpec((tm, k), lambda i, j: (i % nt, i // nt)),
                  pl.BlockSpec((None, k, tn), lambda i, j: (layer, 0, j)),
                  pl.BlockSpec((None, k, tn), lambda i, j: (layer, 0, j + n // tn)),
                  pl.BlockSpec((tm, tn), lambda i, j: (i, j))],
        out_specs=pl.BlockSpec((tm, tn), lambda i, j: (i, j)),
        compiler_params=_params("parallel", "arbitrary"), name=name)(a_tm, w, w, res)


def _mlp_kernel(x_ref, g_ref, w1_ref, w2_ref, gn_ref, o_ref, *rest, final):
    xn_ref = rest[-1]
    f = pl.program_id(1)

    @pl.when(f == 0)
    def _():
        x = x_ref[...]
        xn_ref[...] = _rms_scale(x, g_ref[...]).astype(BF16)
        o_ref[...] = x

    h1 = jnp.maximum(_dot(xn_ref[...], w1_ref[...]), 0.0)
    o_ref[...] += _dot((h1 * h1).astype(BF16), w2_ref[...])

    @pl.when(f == pl.num_programs(1) - 1)
    def _():
        normed = _rms_scale(o_ref[...], gn_ref[...])
        if final:
            o_ref[...] = normed
        else:
            rest[0][...] = normed.astype(BF16)


def mlp_block(x, g, w1, w2, layer, g_next, *, final, tm, tf, name):
    t, d = x.shape
    dff = w1.shape[2]
    row_spec = pl.BlockSpec((tm, d), lambda i, f: (i, 0))
    out_shape = [jax.ShapeDtypeStruct((t, d), F32)] + ([] if final else [jax.ShapeDtypeStruct((t, d), BF16)])
    outs = pl.pallas_call(
        functools.partial(_mlp_kernel, final=final),
        out_shape=out_shape, grid=(t // tm, dff // tf),
        in_specs=[row_spec,
                  pl.BlockSpec((1, d), lambda i, f: (0, 0)),
                  pl.BlockSpec((None, d, tf), lambda i, f: (layer, 0, f)),
                  pl.BlockSpec((None, tf, d), lambda i, f: (layer, f, 0)),
                  pl.BlockSpec((1, d), lambda i, f: (0, 0))],
        out_specs=[row_spec] * len(out_shape),
        scratch_shapes=[pltpu.VMEM((tm, d), BF16)],
        compiler_params=_params("parallel", "arbitrary"), name=name,
    )(x, g.reshape(1, d), w1, w2, g_next.reshape(1, d))
    return outs[0] if final else outs


def _causal_conv_silu(buf_ref, cur, w, b):
    l = cur.shape[0]
    buf_ref[CONV_HALO:CONV_HALO + l, :] = cur
    acc = b + w[SSD_CONV - 1:SSD_CONV, :] * cur
    for k in range(SSD_CONV - 1):
        start = CONV_HALO - (SSD_CONV - 1) + k
        acc = acc + w[k:k + 1, :] * buf_ref[start:start + l, :]
    buf_ref[0:CONV_HALO, :] = cur[l - CONV_HALO:, :]
    return _silu(acc)


def _dot3_tn(x, r):
    hi, mid, lo = _split3(x)
    return _dot_tn(hi, r) + _dot_tn(mid, r) + _dot_tn(lo, r)


def _ssd_head_rows(dt_raw_t, head_p, rows_ref):
    dtt = _softplus(dt_raw_t + head_p[:, 0:1])
    rows_ref[0:SSD_HPG, :] = _cumsum_cols(dtt * (-LOG2E * jnp.exp(head_p[:, 1:2])))
    rows_ref[SSD_HPG:, :] = dtt


def _ssd_kernel(z_ref, x_ref, b_ref, c_ref, dtt_ref, dttn_ref, cp_ref, hp_ref, r_ref,
                o_ref, ss_ref, xbuf, bbuf, cbuf, state_ref, rows_ref, wide_ref, xs_ref, bm_ref, cm_ref):
    gw, ns = SSD_GROUP_W, SSD_STATE

    @pl.when(pl.program_id(2) == 0)
    def _():
        state_ref[...] = jnp.zeros_like(state_ref)
        for buf in (xbuf, bbuf, cbuf):
            buf[:, 0:CONV_HALO, :] = jnp.zeros((buf.shape[0], CONV_HALO, buf.shape[2]), F32)
        for gi in range(SSD_GROUPS_PER_STEP):
            _ssd_head_rows(dtt_ref[gi * SSD_HPG:(gi + 1) * SSD_HPG, :], hp_ref[gi], rows_ref.at[gi])

    for gi in range(SSD_GROUPS_PER_STEP):
        x_cols, bc_cols = pl.ds(gi * gw, gw), pl.ds(gi * ns, ns)
        _ssd_group(z_ref.at[0, :, x_cols], x_ref.at[0, :, x_cols], b_ref.at[0, :, bc_cols], c_ref.at[0, :, bc_cols],
                   dttn_ref.at[pl.ds(gi * SSD_HPG, SSD_HPG), :], cp_ref.at[gi], hp_ref.at[gi], r_ref,
                   o_ref.at[0, :, x_cols], ss_ref.at[0, :, pl.ds(gi * LANES, LANES)],
                   xbuf.at[gi], bbuf.at[gi], cbuf.at[gi], state_ref.at[gi], rows_ref.at[gi], wide_ref.at[gi],
                   xs_ref.at[gi], bm_ref.at[gi], cm_ref.at[gi])


def _ssd_group(z_ref, x_ref, b_ref, c_ref, dttn_ref, cp_ref, hp_ref, r_ref,
               o_ref, ss_ref, xbuf, bbuf, cbuf, state_ref, rows_ref, wide_ref, xs_ref, bm_ref, cm_ref):
    l = x_ref.shape[0]
    sub = LANES
    nsub = l // sub
    hw = SSD_HEAD_DIM
    gw, ns = SSD_GROUP_W, SSD_STATE
    n_a = SSD_HPG * sub + gw

    acst = rows_ref[0:SSD_HPG, :]
    wide_ref[...] = _dot_tn(jnp.concatenate(_split3(rows_ref[...]), axis=0), r_ref[...])

    cp = cp_ref[...]
    cw, cbias = cp[0:SSD_CONV, :], cp[SSD_CONV:SSD_CONV + 1, :]
    xs_ref[...] = _causal_conv_silu(xbuf, x_ref[...], cw[:, :gw], cbias[:, :gw])
    bm_ref[...] = _causal_conv_silu(bbuf, b_ref[...], cw[:, gw:gw + ns], cbias[:, gw:gw + ns]).astype(BF16)
    cm_ref[...] = _causal_conv_silu(cbuf, c_ref[...], cw[:, gw + ns:], cbias[:, gw + ns:]).astype(BF16)

    _ssd_head_rows(dttn_ref[...], hp_ref[...], rows_ref)

    xs, bm16, cm16 = xs_ref[...], bm_ref[...], cm_ref[...]
    acs_blk = wide_ref[:, :SSD_HPG * sub]
    acs64 = wide_ref[:, SSD_HPG * sub:n_a]
    dt64 = wide_ref[:, n_a:]
    last64 = acs64[l - 1:l, :]

    xdt = xs * dt64
    xdt16 = xdt.astype(BF16)
    st = state_ref[...]
    y = _dot(cm16, st.astype(BF16)) * jnp.exp2(acs64)
    state_ref[...] = st * jnp.exp2(last64) + _dot_tn(bm16, (xdt * jnp.exp2(last64 - acs64)).astype(BF16))
    y = y + cp[SSD_CONV + 1:SSD_CONV + 2, :gw] * xs

    cb = _dot_nt(cm16, bm16)
    diag_mask = _causal(sub)
    cb_blk = [[cb[r * sub:(r + 1) * sub, c * sub:(c + 1) * sub] if c < r else
               jnp.where(diag_mask, cb[r * sub:(r + 1) * sub, c * sub:(c + 1) * sub], 0.0)
               for c in range(r + 1)] for r in range(nsub)]
    lane = lax.broadcasted_iota(jnp.int32, (l, sub), 1)
    pair_out = []
    for p in range(SSD_HPG // 2):
        xp = xdt16[:, p * sub:(p + 1) * sub]
        rhs = [jnp.where(lane < hw, xp, jnp.zeros_like(xp)), jnp.where(lane >= hw, xp, jnp.zeros_like(xp))]
        rows = []
        for r in range(nsub):
            lhs_parts, rhs_parts = [], []
            for h in range(2):
                j = 2 * p + h
                col = acs_blk[r * sub:(r + 1) * sub, j * sub:(j + 1) * sub]
                for c in range(r + 1):
                    seg = col - acst[j:j + 1, c * sub:(c + 1) * sub]
                    if c == r:
                        seg = jnp.minimum(seg, 0.0)
                    lhs_parts.append((cb_blk[r][c] * jnp.exp2(seg)).astype(BF16))
                rhs_parts.append(rhs[h][:(r + 1) * sub, :])
            rows.append(_dot(jnp.concatenate(lhs_parts, axis=1), jnp.concatenate(rhs_parts, axis=0)))
        pair_out.append(jnp.concatenate(rows, axis=0))
    y = y + jnp.concatenate(pair_out, axis=1)
    y = y * _silu(z_ref[...])
    ss_ref[...] = jnp.broadcast_to(jnp.sum(y * y, axis=1, keepdims=True), (l, sub))
    o_ref[...] = (y * cp[SSD_CONV + 2:SSD_CONV + 3, :gw]).astype(o_ref.dtype)


def _head_expand(lanes_per_head, n_heads):
    return jnp.repeat(jnp.eye(n_heads, dtype=F32), lanes_per_head, axis=1)


def ssd_core(proj, dt_t, conv_w, conv_b, dt_bias, a_log, d_skip, norm_g, *, batch, seq):
    l = min(SSD_CHUNK, seq)
    nc = seq // l
    g_, j_ = SSD_GROUPS, SSD_HPG
    inner = g_ * SSD_GROUP_W
    nx = inner // SSD_GROUP_W
    nb = 2 * inner // SSD_STATE
    expand64 = _head_expand(SSD_HEAD_DIM, j_)
    r16 = jnp.concatenate([
        jnp.concatenate([_head_expand(LANES, j_), expand64, jnp.zeros_like(expand64)], axis=1),
        jnp.concatenate([jnp.zeros((j_, j_ * LANES + SSD_GROUP_W), F32), expand64], axis=1)], axis=0)
    r = jnp.concatenate([r16, r16, r16], axis=0).astype(BF16)
    bc_w = g_ * SSD_STATE

    def per_group(p, rows):
        parts = [p[:, :inner].reshape(rows, g_, SSD_GROUP_W), p[:, inner:inner + bc_w].reshape(rows, g_, SSD_STATE),
                 p[:, inner + bc_w:].reshape(rows, g_, SSD_STATE)]
        return jnp.concatenate(parts, axis=2).transpose(1, 0, 2)

    cp_w = SSD_GROUP_W + 2 * SSD_STATE
    x_only = ((0, 0), (0, 0), (0, cp_w - SSD_GROUP_W))
    dskip = jnp.pad(jnp.repeat(d_skip.reshape(g_, 1, j_), SSD_HEAD_DIM, axis=2), x_only)
    gain = jnp.pad(norm_g.reshape(g_, 1, SSD_GROUP_W), x_only)
    conv_p = jnp.concatenate([per_group(conv_w, SSD_CONV), per_group(conv_b.reshape(1, -1), 1), dskip, gain,
                              jnp.zeros((g_, 8 - SSD_CONV - 3, cp_w), F32)], axis=1).astype(F32)
    head_p = jnp.pad(jnp.stack([dt_bias.reshape(g_, j_), a_log.reshape(g_, j_)], axis=2),
                     ((0, 0), (0, 0), (0, LANES - 2))).astype(F32)
    gps = SSD_GROUPS_PER_STEP
    xw, bw = gps * SSD_GROUP_W, gps * SSD_STATE
    in_specs = [
        pl.BlockSpec((1, l, xw), lambda b, g, c: (b, c, g)),
        pl.BlockSpec((1, l, xw), lambda b, g, c: (b, c, nx // gps + g)),
        pl.BlockSpec((1, l, bw), lambda b, g, c: (b, c, nb // gps + g)),
        pl.BlockSpec((1, l, bw), lambda b, g, c: (b, c, (nb + g_) // gps + g)),
        pl.BlockSpec((gps * j_, l), lambda b, g, c: (g, b * nc + c)),
        pl.BlockSpec((gps * j_, l), lambda b, g, c: (g, b * nc + jnp.minimum(c + 1, nc - 1))),
        pl.BlockSpec((gps, 8, cp_w), lambda b, g, c: (g, 0, 0)),
        pl.BlockSpec((gps, 8, LANES), lambda b, g, c: (g, 0, 0)),
        pl.BlockSpec(r.shape, lambda b, g, c: (0, 0)),
    ]
    return pl.pallas_call(
        _ssd_kernel,
        out_shape=[jax.ShapeDtypeStruct((batch, seq, inner), BF16),
                   jax.ShapeDtypeStruct((batch, seq, g_ * LANES), F32)],
        grid=(batch, g_ // gps, nc), in_specs=in_specs,
        out_specs=[pl.BlockSpec((1, l, xw), lambda b, g, c: (b, c, g)),
                   pl.BlockSpec((1, l, gps * LANES), lambda b, g, c: (b, c, g))],
        scratch_shapes=[pltpu.VMEM((gps, CONV_HALO + l, SSD_GROUP_W), F32),
                        pltpu.VMEM((gps, CONV_HALO + l, SSD_STATE), F32),
                        pltpu.VMEM((gps, CONV_HALO + l, SSD_STATE), F32),
                        pltpu.VMEM((gps, SSD_STATE, SSD_GROUP_W), F32),
                        pltpu.VMEM((gps, 2 * j_, l), F32),
                        pltpu.VMEM((gps, l, r.shape[1]), F32),
                        pltpu.VMEM((gps, l, SSD_GROUP_W), F32),
                        pltpu.VMEM((gps, l, SSD_STATE), BF16),
                        pltpu.VMEM((gps, l, SSD_STATE), BF16)],
        compiler_params=_params("parallel", "parallel", "arbitrary"), name="ssd_core",
    )(proj, proj, proj, proj, dt_t, dt_t, conv_p, head_p, r)


def _mlstm_kernel(q_ref, k_ref, v_ref, o_ref, gc_ref, gr_ref, gbc_ref, gbr_ref, hg_ref,
                  out_ref, c_ref, n_ref, m_ref):
    @pl.when(pl.program_id(1) == 0)
    def _():
        c_ref[...] = jnp.zeros_like(c_ref)
        n_ref[...] = jnp.zeros_like(n_ref)
        m_ref[...] = jnp.zeros_like(m_ref)

    for hi in range(MLSTM_HEADS):
        qk_cols, v_cols = pl.ds(hi * MLSTM_DQK, MLSTM_DQK), pl.ds(hi * MLSTM_DV, MLSTM_DV)
        _mlstm_head(q_ref.at[0, :, qk_cols], k_ref.at[0, :, qk_cols], v_ref.at[0, :, v_cols], o_ref.at[0, :, v_cols],
                    gc_ref.at[0, hi], gr_ref.at[0, hi], gbc_ref.at[hi], gbr_ref.at[hi], hg_ref.at[:, v_cols],
                    out_ref.at[0, :, v_cols], c_ref.at[hi], n_ref.at[hi], m_ref.at[hi])


def _mlstm_head(q_ref, k_ref, v_ref, o_ref, gc_ref, gr_ref, gbc_ref, gbr_ref, hg_ref,
                out_ref, c_ref, n_ref, m_ref):
    l = q_ref.shape[0]
    q = q_ref[...] * (MLSTM_DQK ** -0.5)
    k = k_ref[...]
    v16 = v_ref[...].astype(BF16)
    q16 = q.astype(BF16)
    gcol = gc_ref[...] + gbc_ref[...]
    grow = gr_ref[...] + gbr_ref[...]
    i_col, lf_col = gcol[:, 0:1], _log_sigmoid(gcol[:, 1:2])
    i_row, lf_row = grow[0:1, :], _log_sigmoid(grow[1:2, :])
    bcum = _cumsum_rows(lf_col)
    bcum_row = _cumsum_cols(lf_row)
    m_st = m_ref[0:1, 0:1]

    causal = _causal(l)
    intra = jnp.where(causal, bcum - bcum_row + i_row, -jnp.inf)
    inter = bcum + m_st
    m_t = jnp.maximum(inter, jnp.max(intra, axis=1, keepdims=True))
    w = jnp.exp(intra - m_t)
    scale_inter = jnp.exp(inter - m_t)
    qk = _dot_nt(q16, k.astype(BF16)) * w
    num = _dot(qk.astype(BF16), v16) + scale_inter * _dot(q16, c_ref[...].astype(BF16))
    qn = jnp.sum(q * n_ref[...], axis=1, keepdims=True)
    den = jnp.sum(qk, axis=1, keepdims=True) + scale_inter * qn
    h = num / jnp.maximum(jnp.abs(den), jnp.exp(-m_t))

    b_last = bcum[l - 1:l, :]
    tail = b_last - bcum + i_col
    m_new = jnp.maximum(b_last + m_st, jnp.max(tail, axis=0, keepdims=True))
    carry_scale = jnp.exp(b_last + m_st - m_new)
    wk = k * jnp.exp(tail - m_new)
    c_ref[...] = carry_scale * c_ref[...] + _dot_tn(wk.astype(BF16), v16)
    n_ref[...] = carry_scale * n_ref[...] + jnp.sum(wk, axis=0, keepdims=True)
    m_ref[...] = jnp.broadcast_to(m_new, m_ref.shape)

    hn = _rms_scale(h, hg_ref[...])
    out_ref[...] = (_sigmoid(o_ref[...]) * hn).astype(out_ref.dtype)


def mlstm_core(proj, gates, gate_b, head_g, *, batch, seq):
    l = MLSTM_CHUNK
    h_ = MLSTM_HEADS
    qk_w = h_ * MLSTM_DQK
    v_w = h_ * MLSTM_DV
    gsplit = jnp.stack([gates[..., :h_], gates[..., h_:2 * h_]], axis=-1)
    gcol = gsplit.transpose(0, 2, 1, 3)
    grow = gsplit.transpose(0, 2, 3, 1)
    gb = jnp.stack([gate_b[:h_], gate_b[h_:]], axis=-1).astype(F32)
    return pl.pallas_call(
        _mlstm_kernel, out_shape=jax.ShapeDtypeStruct((batch, seq, v_w), BF16),
        grid=(batch, seq // l),
        in_specs=[pl.BlockSpec((1, l, qk_w), lambda b, c: (b, c, 0)),
                  pl.BlockSpec((1, l, qk_w), lambda b, c: (b, c, 1)),
                  pl.BlockSpec((1, l, v_w), lambda b, c: (b, c, 2 * qk_w // v_w)),
                  pl.BlockSpec((1, l, v_w), lambda b, c: (b, c, 2 * qk_w // v_w + 1)),
                  pl.BlockSpec((1, h_, l, 2), lambda b, c: (b, 0, c, 0)),
                  pl.BlockSpec((1, h_, 2, l), lambda b, c: (b, 0, 0, c)),
                  pl.BlockSpec((h_, 1, 2), lambda b, c: (0, 0, 0)),
                  pl.BlockSpec((h_, 2, 1), lambda b, c: (0, 0, 0)),
                  pl.BlockSpec((1, v_w), lambda b, c: (0, 0))],
        out_specs=pl.BlockSpec((1, l, v_w), lambda b, c: (b, c, 0)),
        scratch_shapes=[pltpu.VMEM((h_, MLSTM_DQK, MLSTM_DV), F32),
                        pltpu.VMEM((h_, 1, MLSTM_DQK), F32),
                        pltpu.VMEM((h_, 8, LANES), F32)],
        compiler_params=_params("parallel", "arbitrary"), name="mlstm_core",
    )(proj, proj, proj, proj, gcol, grow, gb.reshape(h_, 1, 2), gb.reshape(h_, 2, 1),
      head_g.reshape(1, v_w).astype(F32))


def _s5_discretize_kernel(logdt_ref, are_ref, aim_ref, bre_ref, bim_ref,
                          abre_ref, abim_ref, bbre_ref, bbim_ref):
    ar, ai = are_ref[...], aim_ref[...]
    dt = jnp.exp(logdt_ref[...])
    mag = jnp.exp(ar * dt)
    abar_re, abar_im = mag * jnp.cos(ai * dt), mag * jnp.sin(ai * dt)
    den = ar * ar + ai * ai
    zoh_re = ((abar_re - 1.0) * ar + abar_im * ai) / den
    zoh_im = (abar_im * ar - (abar_re - 1.0) * ai) / den
    abre_ref[...] = abar_re
    abim_ref[...] = abar_im
    for c in range(S5_GROUP):
        br, bi = bre_ref[c], bim_ref[c]
        bbre_ref[c] = zoh_re * br - zoh_im * bi
        bbim_ref[c] = zoh_re * bi + zoh_im * br


def s5_discretize(log_dt, a_re, a_im, b_re, b_im):
    g_, p_ = a_re.shape
    sds = jax.ShapeDtypeStruct
    return pl.pallas_call(
        _s5_discretize_kernel,
        out_shape=[sds((g_, p_), F32), sds((g_, p_), F32),
                   sds((S5_GROUP, g_, p_), F32), sds((S5_GROUP, g_, p_), F32)],
        name="s5_discretize",
    )(log_dt.reshape(g_, 1).astype(F32), a_re.astype(F32), a_im.astype(F32),
      b_re.astype(F32).transpose(2, 0, 1), b_im.astype(F32).transpose(2, 0, 1))


def _s5_kernel(u_ref, bw_ref, cw_ref, a_ref, d_ref, y_ref, st_ref, bu_ref, *, batch):
    ts = u_ref.shape[0]
    half = S5_BLOCK_STATE

    @pl.when(pl.program_id(1) == 0)
    def _():
        st_ref[...] = jnp.zeros_like(st_ref)

    u = u_ref[...].reshape(ts * batch, LANES)
    bu_ref[...] = _dot(u.astype(BF16), bw_ref[0])
    a = a_ref[0]
    a_re = jnp.broadcast_to(a[:, :half], (batch, half))
    a_im = jnp.broadcast_to(a[:, half:], (batch, half))

    def step(t, carry):
        s_re, s_im = carry
        rows = pl.ds(pl.multiple_of(t * batch, batch), batch)
        n_re = a_re * s_re - a_im * s_im + bu_ref[rows, 0:half]
        n_im = a_re * s_im + a_im * s_re + bu_ref[rows, half:2 * half]
        bu_ref[rows, 0:half] = n_re
        bu_ref[rows, half:2 * half] = n_im
        return n_re, n_im

    s_re, s_im = lax.fori_loop(0, ts, step, (st_ref[:, 0:half], st_ref[:, half:2 * half]))
    st_ref[:, 0:half] = s_re
    st_ref[:, half:2 * half] = s_im

    y = _dot(bu_ref[...].astype(BF16), cw_ref[0]) + d_ref[...] * u
    y_ref[...] = jax.nn.gelu(y).astype(y_ref.dtype).reshape(ts, batch, LANES)


def s5_core(u_tm, bw, cw, a_vec, d_skip, *, batch, seq):
    width = u_tm.shape[2]
    nblk = width // LANES
    ts = S5_TIME_BLOCK
    return pl.pallas_call(
        functools.partial(_s5_kernel, batch=batch),
        out_shape=jax.ShapeDtypeStruct((seq, batch, width), BF16),
        grid=(nblk, seq // ts),
        in_specs=[pl.BlockSpec((ts, batch, LANES), lambda j, i: (i, 0, j)),
                  pl.BlockSpec((1, LANES, 2 * S5_BLOCK_STATE), lambda j, i: (j, 0, 0)),
                  pl.BlockSpec((1, 2 * S5_BLOCK_STATE, LANES), lambda j, i: (j, 0, 0)),
                  pl.BlockSpec((1, 1, 2 * S5_BLOCK_STATE), lambda j, i: (j, 0, 0)),
                  pl.BlockSpec((1, LANES), lambda j, i: (0, j))],
        out_specs=pl.BlockSpec((ts, batch, LANES), lambda j, i: (i, 0, j)),
        scratch_shapes=[pltpu.VMEM((batch, 2 * S5_BLOCK_STATE), F32),
                        pltpu.VMEM((ts * batch, 2 * S5_BLOCK_STATE), F32)],
        compiler_params=_params("parallel", "arbitrary"), name="s5_core",
    )(u_tm, bw, cw, a_vec, d_skip.reshape(1, width).astype(F32))


def _s5_block_weights(abar_re, abar_im, bbar_re, bbar_im, c_re, c_im):
    g_, p_ = abar_re.shape
    gb = S5_GROUPS_PER_BLOCK
    nblk = g_ // gb
    eye = jnp.eye(gb, dtype=F32)

    def in_blocks(bbar):
        b = bbar.reshape(S5_GROUP, nblk, gb, p_).transpose(1, 2, 0, 3)
        return jnp.einsum("ngcp,gh->ngchp", b, eye).reshape(nblk, gb * S5_GROUP, gb * p_)

    def out_blocks(cmat):
        c = cmat.reshape(nblk, gb, S5_GROUP, p_).transpose(0, 1, 3, 2)
        return jnp.einsum("ngpc,gh->ngphc", c, eye).reshape(nblk, gb * p_, gb * S5_GROUP)

    bw = jnp.concatenate([in_blocks(bbar_re), in_blocks(bbar_im)], axis=2).astype(BF16)
    cw = jnp.concatenate([out_blocks(c_re.astype(F32)), out_blocks(-c_im.astype(F32))], axis=1).astype(BF16)
    a_vec = jnp.concatenate([abar_re.reshape(nblk, 1, gb * p_), abar_im.reshape(nblk, 1, gb * p_)], axis=2)
    return bw, cw, a_vec


TOKEN_TILE = 1024
COL_TILE = 1024
MLP_TOKEN_TILE = 1024
MLP_FF_TILE = 512


def _small_head(w_in, idx, main, transposed):
    w = jnp.pad(w_in[idx, :, main:], ((0, 0), (0, LANES - (w_in.shape[2] - main)))).astype(BF16)
    return w.T if transposed else w


def _ssd_layer(h, xin, g, w_in, w_in16, w_out16, idx, conv_w, conv_b, dt_bias, a_log, d_skip, norm_g, *, batch, seq):
    inner = SSD_GROUPS * SSD_GROUP_W
    main = 2 * inner + 2 * SSD_GROUPS * SSD_STATE
    proj, dt_t = in_proj(xin, g, w_in16, idx, n=main, w_small=_small_head(w_in, idx, main, True), small="cols",
                         tm=TOKEN_TILE, tn=COL_TILE, name="ssd_in_proj")
    y16, ss = ssd_core(proj.reshape(batch, seq, main), dt_t, conv_w, conv_b, dt_bias, a_log, d_skip, norm_g,
                       batch=batch, seq=seq)
    return rms_matmul_residual(y16.reshape(batch * seq, inner), ss.reshape(batch * seq, -1), w_out16, idx, h,
                               tm=TOKEN_TILE, tn=COL_TILE // 2, name="ssd_out_proj")


def _mlstm_layer(h, xin, g, w_in, w_in16, w_out16, idx, gate_b, head_g, *, batch, seq):
    main = 2 * MLSTM_HEADS * (MLSTM_DQK + MLSTM_DV)
    proj, gates = in_proj(xin, g, w_in16, idx, n=main, w_small=_small_head(w_in, idx, main, False), small="rows",
                          tm=TOKEN_TILE, tn=COL_TILE, name="mlstm_in_proj")
    hs = mlstm_core(proj.reshape(batch, seq, main),
                    gates[:, :2 * MLSTM_HEADS].reshape(batch, seq, 2 * MLSTM_HEADS),
                    gate_b, head_g, batch=batch, seq=seq)
    return matmul_residual(hs.reshape(batch * seq, -1), w_out16, idx, h,
                           tm=TOKEN_TILE, tn=COL_TILE, name="mlstm_out_proj")


def _s5_layer(h, xin, g, w_in16, w_out16, idx, b_re, b_im, c_re, c_im, d_skip, log_dt, a_re, a_im, *, batch, seq):
    width = w_in16.shape[2]
    tm = min(TOKEN_TILE, seq)
    nt = seq // tm
    ncol = width // COL_TILE
    u_tm = in_proj(xin, g, w_in16, idx, n=width, tm=tm, tn=COL_TILE,
                   out_index=lambda i, j: (i % nt, (i // nt) * ncol + j),
                   out_shape=(seq, batch * width), name="s5_in_proj")
    abar_re, abar_im, bbar_re, bbar_im = s5_discretize(log_dt, a_re, a_im, b_re, b_im)
    bw, cw, a_vec = _s5_block_weights(abar_re, abar_im, bbar_re, bbar_im, c_re, c_im)
    y_tm = s5_core(u_tm.reshape(seq, batch, width), bw, cw, a_vec, d_skip, batch=batch, seq=seq)
    return glu_matmul_residual(y_tm.reshape(seq, batch * width), w_out16, idx, h,
                               batch=batch, tm=tm, tn=COL_TILE, name="s5_out_proj")


def kernel(x, norm_mix_g, norm_mlp_g, ssd_w_in, ssd_conv_w, ssd_conv_b, ssd_dt_bias, ssd_a_log, ssd_d, ssd_norm_g, ssd_w_out, mlstm_w_in, mlstm_gate_b, mlstm_head_g, mlstm_w_out, s5_w_in, s5_b_re, s5_b_im, s5_c_re, s5_c_im, s5_d, s5_log_dt, s5_a_re, s5_a_im, s5_w_out, mlp_w1, mlp_w2, final_norm_g):
    batch, seq, d = x.shape
    depth = norm_mix_g.shape[0]
    ssd_in16, ssd_out16 = ssd_w_in.astype(BF16), ssd_w_out.astype(BF16)
    mlstm_in16, mlstm_out16 = mlstm_w_in.astype(BF16), mlstm_w_out.astype(BF16)
    s5_in16, s5_out16 = s5_w_in.astype(BF16), s5_w_out.astype(BF16)
    mlp_w1_16, mlp_w2_16 = mlp_w1.astype(BF16), mlp_w2.astype(BF16)
    h = x.reshape(batch * seq, d)
    xin, g = h, norm_mix_g[0]
    for layer in range(depth):
        kind, idx = layer % 3, layer // 3
        if kind == 0:
            h = _ssd_layer(h, xin, g, ssd_w_in, ssd_in16, ssd_out16, idx, ssd_conv_w[idx], ssd_conv_b[idx],
                           ssd_dt_bias[idx], ssd_a_log[idx], ssd_d[idx], ssd_norm_g[idx], batch=batch, seq=seq)
        elif kind == 1:
            h = _mlstm_layer(h, xin, g, mlstm_w_in, mlstm_in16, mlstm_out16, idx, mlstm_gate_b[idx],
                             mlstm_head_g[idx], batch=batch, seq=seq)
        else:
            h = _s5_layer(h, xin, g, s5_in16, s5_out16, idx, s5_b_re[idx], s5_b_im[idx], s5_c_re[idx], s5_c_im[idx],
                          s5_d[idx], s5_log_dt[idx], s5_a_re[idx], s5_a_im[idx], batch=batch, seq=seq)
        final = layer == depth - 1
        out = mlp_block(h, norm_mlp_g[layer], mlp_w1_16, mlp_w2_16, layer,
                        final_norm_g if final else norm_mix_g[layer + 1], final=final,
                        tm=min(MLP_TOKEN_TILE, batch * seq), tf=MLP_FF_TILE, name=f"mlp_{layer}")
        if final:
            h = out
        else:
            h, xin = out
            g = None
    return h.reshape(batch, seq, d)
```

```python
import functools
import math

import jax
import jax.numpy as jnp
from jax import lax
from jax.experimental import pallas as pl
from jax.experimental.pallas import tpu as pltpu

F32 = jnp.float32
BF16 = jnp.bfloat16

NORM_EPS = 1e-5
LOG2E = math.log2(math.e)
LANES = 128
VMEM_LIMIT_BYTES = 56 * 1024 * 1024

SSD_HEAD_DIM = 64
SSD_GROUPS = 8
SSD_HPG = 8
SSD_STATE = 128
SSD_CONV = 4
SSD_GROUP_W = SSD_HPG * SSD_HEAD_DIM
SSD_CHUNK = 256
SSD_GROUPS_PER_STEP = 8
CONV_HALO = 8

MLSTM_HEADS = 4
MLSTM_DQK = 256
MLSTM_DV = 512
MLSTM_CHUNK = 256

S5_GROUP = 16
S5_STATE = 64
S5_GROUPS_PER_BLOCK = LANES // S5_GROUP
S5_BLOCK_STATE = S5_GROUPS_PER_BLOCK * S5_STATE
S5_TIME_BLOCK = 64
S5_BLOCKS_PER_STEP = 4


def _params(*semantics):
    return pltpu.CompilerParams(dimension_semantics=semantics, vmem_limit_bytes=VMEM_LIMIT_BYTES)


def _dot(a, b):
    return jnp.dot(a, b, preferred_element_type=F32)


def _dot_nt(a, b):
    return lax.dot_general(a, b, (((1,), (1,)), ((), ())), preferred_element_type=F32)


def _dot_tn(a, b):
    return lax.dot_general(a, b, (((0,), (0,)), ((), ())), preferred_element_type=F32)


def _split3(x):
    hi = x.astype(BF16)
    r1 = x - hi.astype(F32)
    mid = r1.astype(BF16)
    lo = (r1 - mid.astype(F32)).astype(BF16)
    return hi, mid, lo


def _tri(n, upper):
    r = lax.broadcasted_iota(jnp.int32, (n, n), 0)
    c = lax.broadcasted_iota(jnp.int32, (n, n), 1)
    keep = (r <= c) if upper else (c <= r)
    return jnp.where(keep, 1.0, 0.0).astype(BF16)


def _cumsum_rows(x):
    t = _tri(x.shape[0], upper=False)
    hi, mid, lo = _split3(x)
    return _dot(t, hi) + _dot(t, mid) + _dot(t, lo)


def _cumsum_cols(x):
    t = _tri(x.shape[1], upper=True)
    hi, mid, lo = _split3(x)
    return _dot(hi, t) + _dot(mid, t) + _dot(lo, t)


def _causal(n):
    r = lax.broadcasted_iota(jnp.int32, (n, n), 0)
    c = lax.broadcasted_iota(jnp.int32, (n, n), 1)
    return c <= r


def _softplus(x):
    return jnp.maximum(x, 0.0) + jnp.log1p(jnp.exp(-jnp.abs(x)))


def _sigmoid(x):
    return 1.0 / (1.0 + jnp.exp(-x))


def _silu(x):
    return x * _sigmoid(x)


def _log_sigmoid(x):
    return -_softplus(-x)


def _rms_scale(x, g):
    ms = jnp.mean(x * x, axis=-1, keepdims=True)
    return x * lax.rsqrt(ms + NORM_EPS) * g


def _in_proj_kernel(*refs, normalize, small):
    it = iter(refs)
    x_ref = next(it)
    g_ref = next(it) if normalize else None
    w_ref = next(it)
    ws_ref = next(it) if small else None
    o_ref = next(it)
    os_ref = next(it) if small else None
    xn_ref = next(it) if normalize else x_ref

    if normalize or small:
        @pl.when(pl.program_id(1) == 0)
        def _():
            if normalize:
                xn_ref[...] = _rms_scale(x_ref[...], g_ref[...]).astype(BF16)
            if small == "rows":
                os_ref[...] = _dot(xn_ref[...], ws_ref[...])
            elif small == "cols":
                os_ref[...] = _dot_nt(ws_ref[...], xn_ref[...])

    o_ref[...] = _dot(xn_ref[...], w_ref[...])


def in_proj(x, g, w, layer, *, n, w_small=None, small=None, tm, tn, out_index=None, out_shape=None, name):
    n_tok, k = x.shape
    assert n_tok % tm == 0 and n % tn == 0
    normalize = g is not None
    out_index = out_index or (lambda i, j: (i, j))
    in_specs = [pl.BlockSpec((tm, k), lambda i, j: (i, 0))]
    args = [x]
    if normalize:
        in_specs.append(pl.BlockSpec((1, k), lambda i, j: (0, 0)))
        args.append(g.reshape(1, k).astype(F32))
    in_specs.append(pl.BlockSpec((None, k, tn), lambda i, j: (layer, 0, j)))
    args.append(w)
    out_shapes = [jax.ShapeDtypeStruct(out_shape or (n_tok, n), F32)]
    out_specs = [pl.BlockSpec((tm, tn), out_index)]
    if small:
        in_specs.append(pl.BlockSpec(w_small.shape, lambda i, j: (0, 0)))
        args.append(w_small)
        if small == "cols":
            out_shapes.append(jax.ShapeDtypeStruct((LANES, n_tok), F32))
            out_specs.append(pl.BlockSpec((LANES, tm), lambda i, j: (0, i)))
        else:
            out_shapes.append(jax.ShapeDtypeStruct((n_tok, LANES), F32))
            out_specs.append(pl.BlockSpec((tm, LANES), lambda i, j: (i, 0)))
    outs = pl.pallas_call(
        functools.partial(_in_proj_kernel, normalize=normalize, small=small),
        out_shape=out_shapes, grid=(n_tok // tm, n // tn), in_specs=in_specs, out_specs=out_specs,
        scratch_shapes=[pltpu.VMEM((tm, k), BF16)] if normalize else [],
        compiler_params=_params("parallel", "arbitrary"), name=name)(*args)
    return outs if small else outs[0]


def _matmul_res_kernel(a_ref, w_ref, res_ref, o_ref):
    o_ref[...] = res_ref[...] + _dot(a_ref[...], w_ref[...])


def matmul_residual(a, w, layer, res, *, tm, tn, name):
    t, k = a.shape
    n = w.shape[2]
    return pl.pallas_call(
        _matmul_res_kernel, out_shape=jax.ShapeDtypeStruct((t, n), F32),
        grid=(t // tm, n // tn),
        in_specs=[pl.BlockSpec((tm, k), lambda i, j: (i, 0)),
                  pl.BlockSpec((None, k, tn), lambda i, j: (layer, 0, j)),
                  pl.BlockSpec((tm, tn), lambda i, j: (i, j))],
        out_specs=pl.BlockSpec((tm, tn), lambda i, j: (i, j)),
        compiler_params=_params("parallel", "arbitrary"), name=name)(a, w, res)


def _rms_matmul_res_kernel(a_ref, ss_ref, w_ref, res_ref, o_ref):
    k = a_ref.shape[1]
    ss = ss_ref[...]
    tot = ss[:, :LANES]
    for q in range(1, ss.shape[1] // LANES):
        tot = tot + ss[:, q * LANES:(q + 1) * LANES]
    inv = lax.rsqrt(tot * (1.0 / k) + NORM_EPS)
    y = _dot(a_ref[...], w_ref[...])
    o_ref[...] = res_ref[...] + jnp.concatenate([inv] * (y.shape[1] // LANES), axis=1) * y


def rms_matmul_residual(a, ss, w, layer, res, *, tm, tn, name):
    t, k = a.shape
    n = w.shape[2]
    return pl.pallas_call(
        _rms_matmul_res_kernel, out_shape=jax.ShapeDtypeStruct((t, n), F32),
        grid=(t // tm, n // tn),
        in_specs=[pl.BlockSpec((tm, k), lambda i, j: (i, 0)),
                  pl.BlockSpec((tm, ss.shape[1]), lambda i, j: (i, 0)),
                  pl.BlockSpec((None, k, tn), lambda i, j: (layer, 0, j)),
                  pl.BlockSpec((tm, tn), lambda i, j: (i, j))],
        out_specs=pl.BlockSpec((tm, tn), lambda i, j: (i, j)),
        compiler_params=_params("parallel", "arbitrary"), name=name)(a, ss, w, res)


def _glu_res_kernel(a_ref, wv_ref, wg_ref, res_ref, o_ref):
    a = a_ref[...]
    val = _dot(a, wv_ref[...])
    gate = _dot(a, wg_ref[...])
    o_ref[...] = res_ref[...] + val * _sigmoid(gate)


def glu_matmul_residual(a_tm, w, layer, res, *, batch, tm, tn, name):
    s = a_tm.shape[0]
    k = w.shape[1]
    n = w.shape[2] // 2
    nt = s // tm
    return pl.pallas_call(
        _glu_res_kernel, out_shape=jax.ShapeDtypeStruct((batch * s, n), F32),
        grid=(batch * nt, n // tn),
        in_specs=[pl.BlockSpec((tm, k), lambda i, j: (i % nt, i // nt)),
                  pl.BlockSpec((None, k, tn), lambda i, j: (layer, 0, j)),
                  pl.BlockSpec((None, k, tn), lambda i, j: (layer, 0, j + n // tn)),
                  pl.BlockSpec((tm, tn), lambda i, j: (i, j))],
        out_specs=pl.BlockSpec((tm, tn), lambda i, j: (i, j)),
        compiler_params=_params("parallel", "arbitrary"), name=name)(a_tm, w, w, res)


def _mlp_kernel(x_ref, g_ref, w1_ref, w2_ref, gn_ref, o_ref, *rest, final):
    xn_ref = rest[-1]
    f = pl.program_id(1)

    @pl.when(f == 0)
    def _():
        x = x_ref[...]
        xn_ref[...] = _rms_scale(x, g_ref[...]).astype(BF16)
        o_ref[...] = x

    h1 = jnp.maximum(_dot(xn_ref[...], w1_ref[...]), 0.0)
    o_ref[...] += _dot((h1 * h1).astype(BF16), w2_ref[...])

    @pl.when(f == pl.num_programs(1) - 1)
    def _():
        normed = _rms_scale(o_ref[...], gn_ref[...])
        if final:
            o_ref[...] = normed
        else:
            rest[0][...] = normed.astype(BF16)


def mlp_block(x, g, w1, w2, layer, g_next, *, final, tm, tf, name):
    t, d = x.shape
    dff = w1.shape[2]
    row_spec = pl.BlockSpec((tm, d), lambda i, f: (i, 0))
    out_shape = [jax.ShapeDtypeStruct((t, d), F32)] + ([] if final else [jax.ShapeDtypeStruct((t, d), BF16)])
    outs = pl.pallas_call(
        functools.partial(_mlp_kernel, final=final),
        out_shape=out_shape, grid=(t // tm, dff // tf),
        in_specs=[row_spec,
                  pl.BlockSpec((1, d), lambda i, f: (0, 0)),
                  pl.BlockSpec((None, d, tf), lambda i, f: (layer, 0, f)),
                  pl.BlockSpec((None, tf, d), lambda i, f: (layer, f, 0)),
                  pl.BlockSpec((1, d), lambda i, f: (0, 0))],
        out_specs=[row_spec] * len(out_shape),
        scratch_shapes=[pltpu.VMEM((tm, d), BF16)],
        compiler_params=_params("parallel", "arbitrary"), name=name,
    )(x, g.reshape(1, d), w1, w2, g_next.reshape(1, d))
    return outs[0] if final else outs


def _causal_conv_silu(buf_ref, cur, w, b):
    l = cur.shape[0]
    buf_ref[CONV_HALO:CONV_HALO + l, :] = cur
    acc = b + w[SSD_CONV - 1:SSD_CONV, :] * cur
    for k in range(SSD_CONV - 1):
        start = CONV_HALO - (SSD_CONV - 1) + k
        acc = acc + w[k:k + 1, :] * buf_ref[start:start + l, :]
    buf_ref[0:CONV_HALO, :] = cur[l - CONV_HALO:, :]
    return _silu(acc)


def _dot3_tn(x, r):
    hi, mid, lo = _split3(x)
    return _dot_tn(hi, r) + _dot_tn(mid, r) + _dot_tn(lo, r)


def _ssd_head_rows(dt_raw_t, head_p, rows_ref):
    dtt = _softplus(dt_raw_t + head_p[:, 0:1])
    rows_ref[0:SSD_HPG, :] = _cumsum_cols(dtt * (-LOG2E * jnp.exp(head_p[:, 1:2])))
    rows_ref[SSD_HPG:, :] = dtt


def _ssd_kernel(z_ref, x_ref, b_ref, c_ref, dtt_ref, dttn_ref, cp_ref, hp_ref, r_ref,
                o_ref, ss_ref, xbuf, bbuf, cbuf, state_ref, rows_ref, wide_ref, xs_ref, bm_ref, cm_ref):
    gw, ns = SSD_GROUP_W, SSD_STATE

    @pl.when(pl.program_id(2) == 0)
    def _():
        state_ref[...] = jnp.zeros_like(state_ref)
        for buf in (xbuf, bbuf, cbuf):
            buf[:, 0:CONV_HALO, :] = jnp.zeros((buf.shape[0], CONV_HALO, buf.shape[2]), F32)
        for gi in range(SSD_GROUPS_PER_STEP):
            _ssd_head_rows(dtt_ref[gi * SSD_HPG:(gi + 1) * SSD_HPG, :], hp_ref[gi], rows_ref.at[gi])

    def prepare(gi):
        x_cols, bc_cols = pl.ds(gi * gw, gw), pl.ds(gi * ns, ns)
        return _ssd_group_prepare(
            x_ref.at[0, :, x_cols], b_ref.at[0, :, bc_cols], c_ref.at[0, :, bc_cols],
            dttn_ref.at[pl.ds(gi * SSD_HPG, SSD_HPG), :], cp_ref.at[gi], hp_ref.at[gi], r_ref,
            xbuf.at[gi], bbuf.at[gi], cbuf.at[gi], rows_ref.at[gi], wide_ref.at[gi],
            xs_ref.at[gi], bm_ref.at[gi], cm_ref.at[gi])

    for gi in range(SSD_GROUPS_PER_STEP):
        acst = prepare(gi)
        x_cols = pl.ds(gi * gw, gw)
        _ssd_group_main(acst, z_ref.at[0, :, x_cols], cp_ref.at[gi], o_ref.at[0, :, x_cols],
                        ss_ref.at[0, :, pl.ds(gi * LANES, LANES)], state_ref.at[gi], wide_ref.at[gi],
                        xs_ref.at[gi], bm_ref.at[gi], cm_ref.at[gi])


def _ssd_group_prepare(x_ref, b_ref, c_ref, dttn_ref, cp_ref, hp_ref, r_ref,
                       xbuf, bbuf, cbuf, rows_ref, wide_ref, xs_ref, bm_ref, cm_ref):
    gw, ns = SSD_GROUP_W, SSD_STATE
    acst = rows_ref[0:SSD_HPG, :]
    wide_ref[...] = _dot_tn(jnp.concatenate(_split3(rows_ref[...]), axis=0), r_ref[...])

    cp = cp_ref[...]
    cw, cbias = cp[0:SSD_CONV, :], cp[SSD_CONV:SSD_CONV + 1, :]
    xs_ref[...] = _causal_conv_silu(xbuf, x_ref[...], cw[:, :gw], cbias[:, :gw])
    bm_ref[...] = _causal_conv_silu(bbuf, b_ref[...], cw[:, gw:gw + ns], cbias[:, gw:gw + ns]).astype(BF16)
    cm_ref[...] = _causal_conv_silu(cbuf, c_ref[...], cw[:, gw + ns:], cbias[:, gw + ns:]).astype(BF16)

    _ssd_head_rows(dttn_ref[...], hp_ref[...], rows_ref)
    return acst


def _ssd_group_main(acst, z_ref, cp_ref, o_ref, ss_ref, state_ref, wide_ref, xs_ref, bm_ref, cm_ref):
    l = xs_ref.shape[0]
    sub = LANES
    nsub = l // sub
    hw = SSD_HEAD_DIM
    gw = SSD_GROUP_W
    n_a = SSD_HPG * sub + gw
    cp = cp_ref[...]
    xs, bm16, cm16 = xs_ref[...], bm_ref[...], cm_ref[...]
    acs_blk = wide_ref[:, :SSD_HPG * sub]
    acs64 = wide_ref[:, SSD_HPG * sub:n_a]
    dt64 = wide_ref[:, n_a:]
    last64 = acs64[l - 1:l, :]

    xdt = xs * dt64
    xdt16 = xdt.astype(BF16)
    st = state_ref[...]
    y = _dot(cm16, st.astype(BF16)) * jnp.exp2(acs64)
    state_ref[...] = st * jnp.exp2(last64) + _dot_tn(bm16, (xdt * jnp.exp2(last64 - acs64)).astype(BF16))
    y = y + cp[SSD_CONV + 1:SSD_CONV + 2, :gw] * xs

    cb = _dot_nt(cm16, bm16)
    diag_mask = _causal(sub)
    cb_blk = [[cb[r * sub:(r + 1) * sub, c * sub:(c + 1) * sub] if c < r else
               jnp.where(diag_mask, cb[r * sub:(r + 1) * sub, c * sub:(c + 1) * sub], 0.0)
               for c in range(r + 1)] for r in range(nsub)]
    lane = lax.broadcasted_iota(jnp.int32, (l, sub), 1)
    pair_out = []
    for p in range(SSD_HPG // 2):
        xp = xdt16[:, p * sub:(p + 1) * sub]
        rhs = [jnp.where(lane < hw, xp, jnp.zeros_like(xp)), jnp.where(lane >= hw, xp, jnp.zeros_like(xp))]
        rows = []
        for r in range(nsub):
            lhs_parts, rhs_parts = [], []
            for h in range(2):
                j = 2 * p + h
                col = acs_blk[r * sub:(r + 1) * sub, j * sub:(j + 1) * sub]
                for c in range(r + 1):
                    seg = col - acst[j:j + 1, c * sub:(c + 1) * sub]
                    if c == r:
                        seg = jnp.minimum(seg, 0.0)
                    lhs_parts.append((cb_blk[r][c] * jnp.exp2(seg)).astype(BF16))
                rhs_parts.append(rhs[h][:(r + 1) * sub, :])
            rows.append(_dot(jnp.concatenate(lhs_parts, axis=1), jnp.concatenate(rhs_parts, axis=0)))
        pair_out.append(jnp.concatenate(rows, axis=0))
    y = y + jnp.concatenate(pair_out, axis=1)
    y = y * _silu(z_ref[...])
    ss_ref[...] = jnp.broadcast_to(jnp.sum(y * y, axis=1, keepdims=True), (l, sub))
    o_ref[...] = (y * cp[SSD_CONV + 2:SSD_CONV + 3, :gw]).astype(o_ref.dtype)


def _head_expand(lanes_per_head, n_heads):
    return jnp.repeat(jnp.eye(n_heads, dtype=F32), lanes_per_head, axis=1)


def ssd_core(proj, dt_t, conv_w, conv_b, dt_bias, a_log, d_skip, norm_g, *, batch, seq):
    l = min(SSD_CHUNK, seq)
    nc = seq // l
    g_, j_ = SSD_GROUPS, SSD_HPG
    inner = g_ * SSD_GROUP_W
    nx = inner // SSD_GROUP_W
    nb = 2 * inner // SSD_STATE
    expand64 = _head_expand(SSD_HEAD_DIM, j_)
    r16 = jnp.concatenate([
        jnp.concatenate([_head_expand(LANES, j_), expand64, jnp.zeros_like(expand64)], axis=1),
        jnp.concatenate([jnp.zeros((j_, j_ * LANES + SSD_GROUP_W), F32), expand64], axis=1)], axis=0)
    r = jnp.concatenate([r16, r16, r16], axis=0).astype(BF16)
    bc_w = g_ * SSD_STATE

    def per_group(p, rows):
        parts = [p[:, :inner].reshape(rows, g_, SSD_GROUP_W), p[:, inner:inner + bc_w].reshape(rows, g_, SSD_STATE),
                 p[:, inner + bc_w:].reshape(rows, g_, SSD_STATE)]
        return jnp.concatenate(parts, axis=2).transpose(1, 0, 2)

    cp_w = SSD_GROUP_W + 2 * SSD_STATE
    x_only = ((0, 0), (0, 0), (0, cp_w - SSD_GROUP_W))
    dskip = jnp.pad(jnp.repeat(d_skip.reshape(g_, 1, j_), SSD_HEAD_DIM, axis=2), x_only)
    gain = jnp.pad(norm_g.reshape(g_, 1, SSD_GROUP_W), x_only)
    conv_p = jnp.concatenate([per_group(conv_w, SSD_CONV), per_group(conv_b.reshape(1, -1), 1), dskip, gain,
                              jnp.zeros((g_, 8 - SSD_CONV - 3, cp_w), F32)], axis=1).astype(F32)
    head_p = jnp.pad(jnp.stack([dt_bias.reshape(g_, j_), a_log.reshape(g_, j_)], axis=2),
                     ((0, 0), (0, 0), (0, LANES - 2))).astype(F32)
    gps = SSD_GROUPS_PER_STEP
    xw, bw = gps * SSD_GROUP_W, gps * SSD_STATE
    in_specs = [
        pl.BlockSpec((1, l, xw), lambda b, g, c: (b, c, g)),
        pl.BlockSpec((1, l, xw), lambda b, g, c: (b, c, nx // gps + g)),
        pl.BlockSpec((1, l, bw), lambda b, g, c: (b, c, nb // gps + g)),
        pl.BlockSpec((1, l, bw), lambda b, g, c: (b, c, (nb + g_) // gps + g)),
        pl.BlockSpec((gps * j_, l), lambda b, g, c: (g, b * nc + c)),
        pl.BlockSpec((gps * j_, l), lambda b, g, c: (g, b * nc + jnp.minimum(c + 1, nc - 1))),
        pl.BlockSpec((gps, 8, cp_w), lambda b, g, c: (g, 0, 0)),
        pl.BlockSpec((gps, 8, LANES), lambda b, g, c: (g, 0, 0)),
        pl.BlockSpec(r.shape, lambda b, g, c: (0, 0)),
    ]
    return pl.pallas_call(
        _ssd_kernel,
        out_shape=[jax.ShapeDtypeStruct((batch, seq, inner), BF16),
                   jax.ShapeDtypeStruct((batch, seq, g_ * LANES), F32)],
        grid=(batch, g_ // gps, nc), in_specs=in_specs,
        out_specs=[pl.BlockSpec((1, l, xw), lambda b, g, c: (b, c, g)),
                   pl.BlockSpec((1, l, gps * LANES), lambda b, g, c: (b, c, g))],
        scratch_shapes=[pltpu.VMEM((gps, CONV_HALO + l, SSD_GROUP_W), F32),
                        pltpu.VMEM((gps, CONV_HALO + l, SSD_STATE), F32),
                        pltpu.VMEM((gps, CONV_HALO + l, SSD_STATE), F32),
                        pltpu.VMEM((gps, SSD_STATE, SSD_GROUP_W), F32),
                        pltpu.VMEM((gps, 2 * j_, l), F32),
                        pltpu.VMEM((gps, l, r.shape[1]), F32),
                        pltpu.VMEM((gps, l, SSD_GROUP_W), F32),
                        pltpu.VMEM((gps, l, SSD_STATE), BF16),
                        pltpu.VMEM((gps, l, SSD_STATE), BF16)],
        compiler_params=_params("parallel", "parallel", "arbitrary"), name="ssd_core",
    )(proj, proj, proj, proj, dt_t, dt_t, conv_p, head_p, r)


def _mlstm_kernel(q_ref, k_ref, v_ref, o_ref, gc_ref, gr_ref, gbc_ref, gbr_ref, hg_ref,
                  out_ref, c_ref, n_ref, m_ref):
    @pl.when(pl.program_id(1) == 0)
    def _():
        c_ref[...] = jnp.zeros_like(c_ref)
        n_ref[...] = jnp.zeros_like(n_ref)
        m_ref[...] = jnp.zeros_like(m_ref)

    for hi in range(MLSTM_HEADS):
        qk_cols, v_cols = pl.ds(hi * MLSTM_DQK, MLSTM_DQK), pl.ds(hi * MLSTM_DV, MLSTM_DV)
        _mlstm_head(q_ref.at[0, :, qk_cols], k_ref.at[0, :, qk_cols], v_ref.at[0, :, v_cols], o_ref.at[0, :, v_cols],
                    gc_ref.at[0, hi], gr_ref.at[0, hi], gbc_ref.at[hi], gbr_ref.at[hi], hg_ref.at[:, v_cols],
                    out_ref.at[0, :, v_cols], c_ref.at[hi], n_ref.at[hi], m_ref.at[hi])


def _mlstm_head(q_ref, k_ref, v_ref, o_ref, gc_ref, gr_ref, gbc_ref, gbr_ref, hg_ref,
                out_ref, c_ref, n_ref, m_ref):
    l = q_ref.shape[0]
    q = q_ref[...] * (MLSTM_DQK ** -0.5)
    k = k_ref[...]
    v16 = v_ref[...].astype(BF16)
    q16 = q.astype(BF16)
    gcol = gc_ref[...] + gbc_ref[...]
    grow = gr_ref[...] + gbr_ref[...]
    i_col, lf_col = gcol[:, 0:1], _log_sigmoid(gcol[:, 1:2])
    i_row, lf_row = grow[0:1, :], _log_sigmoid(grow[1:2, :])
    bcum = _cumsum_rows(lf_col)
    bcum_row = _cumsum_cols(lf_row)
    m_st = m_ref[0:1, 0:1]

    causal = _causal(l)
    intra = jnp.where(causal, bcum - bcum_row + i_row, -jnp.inf)
    inter = bcum + m_st
    m_t = jnp.maximum(inter, jnp.max(intra, axis=1, keepdims=True))
    w = jnp.exp(intra - m_t)
    scale_inter = jnp.exp(inter - m_t)
    qk = _dot_nt(q16, k.astype(BF16)) * w
    num = _dot(qk.astype(BF16), v16) + scale_inter * _dot(q16, c_ref[...].astype(BF16))
    qn = jnp.sum(q * n_ref[...], axis=1, keepdims=True)
    den = jnp.sum(qk, axis=1, keepdims=True) + scale_inter * qn
    h = num / jnp.maximum(jnp.abs(den), jnp.exp(-m_t))

    b_last = bcum[l - 1:l, :]
    tail = b_last - bcum + i_col
    m_new = jnp.maximum(b_last + m_st, jnp.max(tail, axis=0, keepdims=True))
    carry_scale = jnp.exp(b_last + m_st - m_new)
    wk = k * jnp.exp(tail - m_new)
    c_ref[...] = carry_scale * c_ref[...] + _dot_tn(wk.astype(BF16), v16)
    n_ref[...] = carry_scale * n_ref[...] + jnp.sum(wk, axis=0, keepdims=True)
    m_ref[...] = jnp.broadcast_to(m_new, m_ref.shape)

    hn = _rms_scale(h, hg_ref[...])
    out_ref[...] = (_sigmoid(o_ref[...]) * hn).astype(out_ref.dtype)


def mlstm_core(proj, gates, gate_b, head_g, *, batch, seq):
    l = MLSTM_CHUNK
    h_ = MLSTM_HEADS
    qk_w = h_ * MLSTM_DQK
    v_w = h_ * MLSTM_DV
    gsplit = jnp.stack([gates[..., :h_], gates[..., h_:2 * h_]], axis=-1)
    gcol = gsplit.transpose(0, 2, 1, 3)
    grow = gsplit.transpose(0, 2, 3, 1)
    gb = jnp.stack([gate_b[:h_], gate_b[h_:]], axis=-1).astype(F32)
    return pl.pallas_call(
        _mlstm_kernel, out_shape=jax.ShapeDtypeStruct((batch, seq, v_w), BF16),
        grid=(batch, seq // l),
        in_specs=[pl.BlockSpec((1, l, qk_w), lambda b, c: (b, c, 0)),
                  pl.BlockSpec((1, l, qk_w), lambda b, c: (b, c, 1)),
                  pl.BlockSpec((1, l, v_w), lambda b, c: (b, c, 2 * qk_w // v_w)),
                  pl.BlockSpec((1, l, v_w), lambda b, c: (b, c, 2 * qk_w // v_w + 1)),
                  pl.BlockSpec((1, h_, l, 2), lambda b, c: (b, 0, c, 0)),
                  pl.BlockSpec((1, h_, 2, l), lambda b, c: (b, 0, 0, c)),
                  pl.BlockSpec((h_, 1, 2), lambda b, c: (0, 0, 0)),
                  pl.BlockSpec((h_, 2, 1), lambda b, c: (0, 0, 0)),
                  pl.BlockSpec((1, v_w), lambda b, c: (0, 0))],
        out_specs=pl.BlockSpec((1, l, v_w), lambda b, c: (b, c, 0)),
        scratch_shapes=[pltpu.VMEM((h_, MLSTM_DQK, MLSTM_DV), F32),
                        pltpu.VMEM((h_, 1, MLSTM_DQK), F32),
                        pltpu.VMEM((h_, 8, LANES), F32)],
        compiler_params=_params("parallel", "arbitrary"), name="mlstm_core",
    )(proj, proj, proj, proj, gcol, grow, gb.reshape(h_, 1, 2), gb.reshape(h_, 2, 1),
      head_g.reshape(1, v_w).astype(F32))


def _s5_discretize_kernel(logdt_ref, are_ref, aim_ref, bre_ref, bim_ref,
                          abre_ref, abim_ref, bbre_ref, bbim_ref):
    ar, ai = are_ref[...], aim_ref[...]
    dt = jnp.exp(logdt_ref[...])
    mag = jnp.exp(ar * dt)
    abar_re, abar_im = mag * jnp.cos(ai * dt), mag * jnp.sin(ai * dt)
    den = ar * ar + ai * ai
    zoh_re = ((abar_re - 1.0) * ar + abar_im * ai) / den
    zoh_im = (abar_im * ar - (abar_re - 1.0) * ai) / den
    abre_ref[...] = abar_re
    abim_ref[...] = abar_im
    for c in range(S5_GROUP):
        br, bi = bre_ref[c], bim_ref[c]
        bbre_ref[c] = zoh_re * br - zoh_im * bi
        bbim_ref[c] = zoh_re * bi + zoh_im * br


def s5_discretize(log_dt, a_re, a_im, b_re, b_im):
    g_, p_ = a_re.shape
    sds = jax.ShapeDtypeStruct
    return pl.pallas_call(
        _s5_discretize_kernel,
        out_shape=[sds((g_, p_), F32), sds((g_, p_), F32),
                   sds((S5_GROUP, g_, p_), F32), sds((S5_GROUP, g_, p_), F32)],
        name="s5_discretize",
    )(log_dt.reshape(g_, 1).astype(F32), a_re.astype(F32), a_im.astype(F32),
      b_re.astype(F32).transpose(2, 0, 1), b_im.astype(F32).transpose(2, 0, 1))


def _s5_kernel(u_ref, bw_ref, cw_ref, a_ref, d_ref, y_ref, st_ref, bu_ref, *, batch):
    ts = u_ref.shape[0]
    half = S5_BLOCK_STATE
    nblk = S5_BLOCKS_PER_STEP

    @pl.when(pl.program_id(1) == 0)
    def _():
        st_ref[...] = jnp.zeros_like(st_ref)

    def lanes(n):
        return slice(n * LANES, (n + 1) * LANES)

    def project_in(n):
        u = u_ref[:, :, lanes(n)].reshape(ts * batch, LANES)
        bu_ref[n] = _dot(u.astype(BF16), bw_ref[n])

    def scan(n):
        a = a_ref[n]
        a_re = jnp.broadcast_to(a[:, :half], (batch, half))
        a_im = jnp.broadcast_to(a[:, half:], (batch, half))
        s_re, s_im = st_ref[n, :, 0:half], st_ref[n, :, half:2 * half]
        for t in range(ts):
            rows = slice(t * batch, (t + 1) * batch)
            s_re, s_im = (a_re * s_re - a_im * s_im + bu_ref[n, rows, 0:half],
                          a_re * s_im + a_im * s_re + bu_ref[n, rows, half:2 * half])
            bu_ref[n, rows, 0:half] = s_re
            bu_ref[n, rows, half:2 * half] = s_im
        st_ref[n, :, 0:half] = s_re
        st_ref[n, :, half:2 * half] = s_im

    def project_out(n):
        u = u_ref[:, :, lanes(n)].reshape(ts * batch, LANES)
        y = _dot(bu_ref[n].astype(BF16), cw_ref[n]) + d_ref[:, lanes(n)] * u
        y_ref[:, :, lanes(n)] = jax.nn.gelu(y).astype(y_ref.dtype).reshape(ts, batch, LANES)

    project_in(0)
    for n in range(nblk):
        if n + 1 < nblk:
            project_in(n + 1)
        scan(n)
        if n > 0:
            project_out(n - 1)
    project_out(nblk - 1)


def s5_core(u_tm, bw, cw, a_vec, d_skip, *, batch, seq):
    width = u_tm.shape[2]
    nblk = width // LANES
    ts = min(S5_TIME_BLOCK, seq)
    per = S5_BLOCKS_PER_STEP
    return pl.pallas_call(
        functools.partial(_s5_kernel, batch=batch),
        out_shape=jax.ShapeDtypeStruct((seq, batch, width), BF16),
        grid=(nblk // per, seq // ts),
        in_specs=[pl.BlockSpec((ts, batch, per * LANES), lambda j, i: (i, 0, j)),
                  pl.BlockSpec((per, LANES, 2 * S5_BLOCK_STATE), lambda j, i: (j, 0, 0)),
                  pl.BlockSpec((per, 2 * S5_BLOCK_STATE, LANES), lambda j, i: (j, 0, 0)),
                  pl.BlockSpec((per, 1, 2 * S5_BLOCK_STATE), lambda j, i: (j, 0, 0)),
                  pl.BlockSpec((1, per * LANES), lambda j, i: (0, j))],
        out_specs=pl.BlockSpec((ts, batch, per * LANES), lambda j, i: (i, 0, j)),
        scratch_shapes=[pltpu.VMEM((per, batch, 2 * S5_BLOCK_STATE), F32),
                        pltpu.VMEM((per, ts * batch, 2 * S5_BLOCK_STATE), F32)],
        compiler_params=_params("parallel", "arbitrary"), name="s5_core",
    )(u_tm, bw, cw, a_vec, d_skip.reshape(1, width).astype(F32))


def _s5_block_weights(abar_re, abar_im, bbar_re, bbar_im, c_re, c_im):
    g_, p_ = abar_re.shape
    gb = S5_GROUPS_PER_BLOCK
    nblk = g_ // gb
    eye = jnp.eye(gb, dtype=F32)

    def in_blocks(bbar):
        b = bbar.reshape(S5_GROUP, nblk, gb, p_).transpose(1, 2, 0, 3)
        return jnp.einsum("ngcp,gh->ngchp", b, eye).reshape(nblk, gb * S5_GROUP, gb * p_)

    def out_blocks(cmat):
        c = cmat.reshape(nblk, gb, S5_GROUP, p_).transpose(0, 1, 3, 2)
        return jnp.einsum("ngpc,gh->ngphc", c, eye).reshape(nblk, gb * p_, gb * S5_GROUP)

    bw = jnp.concatenate([in_blocks(bbar_re), in_blocks(bbar_im)], axis=2).astype(BF16)
    cw = jnp.concatenate([out_blocks(c_re.astype(F32)), out_blocks(-c_im.astype(F32))], axis=1).astype(BF16)
    a_vec = jnp.concatenate([abar_re.reshape(nblk, 1, gb * p_), abar_im.reshape(nblk, 1, gb * p_)], axis=2)
    return bw, cw, a_vec


TOKEN_TILE = 1024
COL_TILE = 1024
MLP_TOKEN_TILE = 1024
MLP_FF_TILE = 512


def _small_head(w_in, idx, main, transposed):
    w = jnp.pad(w_in[idx, :, main:], ((0, 0), (0, LANES - (w_in.shape[2] - main)))).astype(BF16)
    return w.T if transposed else w


def _ssd_layer(h, xin, g, w_in, w_in16, w_out16, idx, conv_w, conv_b, dt_bias, a_log, d_skip, norm_g, *, batch, seq):
    inner = SSD_GROUPS * SSD_GROUP_W
    main = 2 * inner + 2 * SSD_GROUPS * SSD_STATE
    proj, dt_t = in_proj(xin, g, w_in16, idx, n=main, w_small=_small_head(w_in, idx, main, True), small="cols",
                         tm=TOKEN_TILE, tn=COL_TILE, name="ssd_in_proj")
    y16, ss = ssd_core(proj.reshape(batch, seq, main), dt_t, conv_w, conv_b, dt_bias, a_log, d_skip, norm_g,
                       batch=batch, seq=seq)
    return rms_matmul_residual(y16.reshape(batch * seq, inner), ss.reshape(batch * seq, -1), w_out16, idx, h,
                               tm=TOKEN_TILE, tn=COL_TILE // 2, name="ssd_out_proj")


def _mlstm_layer(h, xin, g, w_in, w_in16, w_out16, idx, gate_b, head_g, *, batch, seq):
    main = 2 * MLSTM_HEADS * (MLSTM_DQK + MLSTM_DV)
    proj, gates = in_proj(xin, g, w_in16, idx, n=main, w_small=_small_head(w_in, idx, main, False), small="rows",
                          tm=TOKEN_TILE, tn=COL_TILE, name="mlstm_in_proj")
    hs = mlstm_core(proj.reshape(batch, seq, main),
                    gates[:, :2 * MLSTM_HEADS].reshape(batch, seq, 2 * MLSTM_HEADS),
                    gate_b, head_g, batch=batch, seq=seq)
    return matmul_residual(hs.reshape(batch * seq, -1), w_out16, idx, h,
                           tm=TOKEN_TILE, tn=COL_TILE, name="mlstm_out_proj")


def _s5_layer(h, xin, g, w_in16, w_out16, idx, b_re, b_im, c_re, c_im, d_skip, log_dt, a_re, a_im, *, batch, seq):
    width = w_in16.shape[2]
    tm = min(TOKEN_TILE, seq)
    nt = seq // tm
    ncol = width // COL_TILE
    u_tm = in_proj(xin, g, w_in16, idx, n=width, tm=tm, tn=COL_TILE,
                   out_index=lambda i, j: (i % nt, (i // nt) * ncol + j),
                   out_shape=(seq, batch * width), name="s5_in_proj")
    abar_re, abar_im, bbar_re, bbar_im = s5_discretize(log_dt, a_re, a_im, b_re, b_im)
    bw, cw, a_vec = _s5_block_weights(abar_re, abar_im, bbar_re, bbar_im, c_re, c_im)
    y_tm = s5_core(u_tm.reshape(seq, batch, width), bw, cw, a_vec, d_skip, batch=batch, seq=seq)
    return glu_matmul_residual(y_tm.reshape(seq, batch * width), w_out16, idx, h,
                               batch=batch, tm=tm, tn=COL_TILE, name="s5_out_proj")


def kernel(x, norm_mix_g, norm_mlp_g, ssd_w_in, ssd_conv_w, ssd_conv_b, ssd_dt_bias, ssd_a_log, ssd_d, ssd_norm_g, ssd_w_out, mlstm_w_in, mlstm_gate_b, mlstm_head_g, mlstm_w_out, s5_w_in, s5_b_re, s5_b_im, s5_c_re, s5_c_im, s5_d, s5_log_dt, s5_a_re, s5_a_im, s5_w_out, mlp_w1, mlp_w2, final_norm_g):
    batch, seq, d = x.shape
    depth = norm_mix_g.shape[0]
    ssd_in16, ssd_out16 = ssd_w_in.astype(BF16), ssd_w_out.astype(BF16)
    mlstm_in16, mlstm_out16 = mlstm_w_in.astype(BF16), mlstm_w_out.astype(BF16)
    s5_in16, s5_out16 = s5_w_in.astype(BF16), s5_w_out.astype(BF16)
    mlp_w1_16, mlp_w2_16 = mlp_w1.astype(BF16), mlp_w2.astype(BF16)
    h = x.reshape(batch * seq, d)
    xin, g = h, norm_mix_g[0]
    for layer in range(depth):
        kind, idx = layer % 3, layer // 3
        if kind == 0:
            h = _ssd_layer(h, xin, g, ssd_w_in, ssd_in16, ssd_out16, idx, ssd_conv_w[idx], ssd_conv_b[idx],
                           ssd_dt_bias[idx], ssd_a_log[idx], ssd_d[idx], ssd_norm_g[idx], batch=batch, seq=seq)
        elif kind == 1:
            h = _mlstm_layer(h, xin, g, mlstm_w_in, mlstm_in16, mlstm_out16, idx, mlstm_gate_b[idx],
                             mlstm_head_g[idx], batch=batch, seq=seq)
        else:
            h = _s5_layer(h, xin, g, s5_in16, s5_out16, idx, s5_b_re[idx], s5_b_im[idx], s5_c_re[idx], s5_c_im[idx],
                          s5_d[idx], s5_log_dt[idx], s5_a_re[idx], s5_a_im[idx], batch=batch, seq=seq)
        final = layer == depth - 1
        out = mlp_block(h, norm_mlp_g[layer], mlp_w1_16, mlp_w2_16, layer,
                        final_norm_g if final else norm_mix_g[layer + 1], final=final,
                        tm=min(MLP_TOKEN_TILE, batch * seq), tf=MLP_FF_TILE, name=f"mlp_{layer}")
        if final:
            h = out
        else:
            h, xin = out
            g = None
    return h.reshape(batch, seq, d)
```

```python
import functools
import math

import jax
import jax.numpy as jnp
from jax import lax
from jax.experimental import pallas as pl
from jax.experimental.pallas import tpu as pltpu

F32 = jnp.float32
BF16 = jnp.bfloat16

NORM_EPS = 1e-5
LOG2E = math.log2(math.e)
LANES = 128
VMEM_LIMIT_BYTES = 56 * 1024 * 1024

SSD_HEAD_DIM = 64
SSD_GROUPS = 8
SSD_HPG = 8
SSD_STATE = 128
SSD_CONV = 4
SSD_GROUP_W = SSD_HPG * SSD_HEAD_DIM
SSD_CHUNK = 256
SSD_GROUPS_PER_STEP = 8
CONV_HALO = 8

MLSTM_HEADS = 4
MLSTM_DQK = 256
MLSTM_DV = 512
MLSTM_CHUNK = 256

S5_GROUP = 16
S5_STATE = 64
S5_GROUPS_PER_BLOCK = LANES // S5_GROUP
S5_BLOCK_STATE = S5_GROUPS_PER_BLOCK * S5_STATE
S5_TIME_BLOCK = 64
S5_BLOCKS_PER_STEP = 4


def _params(*semantics):
    return pltpu.CompilerParams(dimension_semantics=semantics, vmem_limit_bytes=VMEM_LIMIT_BYTES)


def _dot(a, b):
    return jnp.dot(a, b, preferred_element_type=F32)


def _dot_nt(a, b):
    return lax.dot_general(a, b, (((1,), (1,)), ((), ())), preferred_element_type=F32)


def _dot_tn(a, b):
    return lax.dot_general(a, b, (((0,), (0,)), ((), ())), preferred_element_type=F32)


def _split3(x):
    hi = x.astype(BF16)
    r1 = x - hi.astype(F32)
    mid = r1.astype(BF16)
    lo = (r1 - mid.astype(F32)).astype(BF16)
    return hi, mid, lo


def _tri(n, upper):
    r = lax.broadcasted_iota(jnp.int32, (n, n), 0)
    c = lax.broadcasted_iota(jnp.int32, (n, n), 1)
    keep = (r <= c) if upper else (c <= r)
    return jnp.where(keep, 1.0, 0.0).astype(BF16)


def _cumsum_rows(x):
    t = _tri(x.shape[0], upper=False)
    hi, mid, lo = _split3(x)
    return _dot(t, hi) + _dot(t, mid) + _dot(t, lo)


def _cumsum_cols(x):
    t = _tri(x.shape[1], upper=True)
    hi, mid, lo = _split3(x)
    return _dot(hi, t) + _dot(mid, t) + _dot(lo, t)


def _causal(n):
    r = lax.broadcasted_iota(jnp.int32, (n, n), 0)
    c = lax.broadcasted_iota(jnp.int32, (n, n), 1)
    return c <= r


def _softplus(x):
    return jnp.maximum(x, 0.0) + jnp.log1p(jnp.exp(-jnp.abs(x)))


def _sigmoid(x):
    return 1.0 / (1.0 + jnp.exp(-x))


def _silu(x):
    return x * _sigmoid(x)


def _log_sigmoid(x):
    return -_softplus(-x)


def _rms_scale(x, g):
    ms = jnp.mean(x * x, axis=-1, keepdims=True)
    return x * lax.rsqrt(ms + NORM_EPS) * g


def _in_proj_kernel(*refs, normalize, small):
    it = iter(refs)
    x_ref = next(it)
    g_ref = next(it) if normalize else None
    w_ref = next(it)
    ws_ref = next(it) if small else None
    o_ref = next(it)
    os_ref = next(it) if small else None
    xn_ref = next(it) if normalize else x_ref

    if normalize or small:
        @pl.when(pl.program_id(1) == 0)
        def _():
            if normalize:
                xn_ref[...] = _rms_scale(x_ref[...], g_ref[...]).astype(BF16)
            if small == "rows":
                os_ref[...] = _dot(xn_ref[...], ws_ref[...])
            elif small == "cols":
                os_ref[...] = _dot_nt(ws_ref[...], xn_ref[...])

    o_ref[...] = _dot(xn_ref[...], w_ref[...])


def in_proj(x, g, w, layer, *, n, w_small=None, small=None, tm, tn, out_index=None, out_shape=None, name):
    n_tok, k = x.shape
    assert n_tok % tm == 0 and n % tn == 0
    normalize = g is not None
    out_index = out_index or (lambda i, j: (i, j))
    in_specs = [pl.BlockSpec((tm, k), lambda i, j: (i, 0))]
    args = [x]
    if normalize:
        in_specs.append(pl.BlockSpec((1, k), lambda i, j: (0, 0)))
        args.append(g.reshape(1, k).astype(F32))
    in_specs.append(pl.BlockSpec((None, k, tn), lambda i, j: (layer, 0, j)))
    args.append(w)
    out_shapes = [jax.ShapeDtypeStruct(out_shape or (n_tok, n), F32)]
    out_specs = [pl.BlockSpec((tm, tn), out_index)]
    if small:
        in_specs.append(pl.BlockSpec(w_small.shape, lambda i, j: (0, 0)))
        args.append(w_small)
        if small == "cols":
            out_shapes.append(jax.ShapeDtypeStruct((LANES, n_tok), F32))
            out_specs.append(pl.BlockSpec((LANES, tm), lambda i, j: (0, i)))
        else:
            out_shapes.append(jax.ShapeDtypeStruct((n_tok, LANES), F32))
            out_specs.append(pl.BlockSpec((tm, LANES), lambda i, j: (i, 0)))
    outs = pl.pallas_call(
        functools.partial(_in_proj_kernel, normalize=normalize, small=small),
        out_shape=out_shapes, grid=(n_tok // tm, n // tn), in_specs=in_specs, out_specs=out_specs,
        scratch_shapes=[pltpu.VMEM((tm, k), BF16)] if normalize else [],
        compiler_params=_params("parallel", "arbitrary"), name=name)(*args)
    return outs if small else outs[0]


def _matmul_res_kernel(a_ref, w_ref, res_ref, o_ref):
    o_ref[...] = res_ref[...] + _dot(a_ref[...], w_ref[...])


def matmul_residual(a, w, layer, res, *, tm, tn, name):
    t, k = a.shape
    n = w.shape[2]
    return pl.pallas_call(
        _matmul_res_kernel, out_shape=jax.ShapeDtypeStruct((t, n), F32),
        grid=(t // tm, n // tn),
        in_specs=[pl.BlockSpec((tm, k), lambda i, j: (i, 0)),
                  pl.BlockSpec((None, k, tn), lambda i, j: (layer, 0, j)),
                  pl.BlockSpec((tm, tn), lambda i, j: (i, j))],
        out_specs=pl.BlockSpec((tm, tn), lambda i, j: (i, j)),
        compiler_params=_params("parallel", "arbitrary"), name=name)(a, w, res)


def _rms_matmul_res_kernel(a_ref, ss_ref, w_ref, res_ref, o_ref):
    k = a_ref.shape[1]
    ss = ss_ref[...]
    tot = ss[:, :LANES]
    for q in range(1, ss.shape[1] // LANES):
        tot = tot + ss[:, q * LANES:(q + 1) * LANES]
    inv = lax.rsqrt(tot * (1.0 / k) + NORM_EPS)
    y = _dot(a_ref[...], w_ref[...])
    o_ref[...] = res_ref[...] + jnp.concatenate([inv] * (y.shape[1] // LANES), axis=1) * y


def rms_matmul_residual(a, ss, w, layer, res, *, tm, tn, name):
    t, k = a.shape
    n = w.shape[2]
    return pl.pallas_call(
        _rms_matmul_res_kernel, out_shape=jax.ShapeDtypeStruct((t, n), F32),
        grid=(t // tm, n // tn),
        in_specs=[pl.BlockSpec((tm, k), lambda i, j: (i, 0)),
                  pl.BlockSpec((tm, ss.shape[1]), lambda i, j: (i, 0)),
                  pl.BlockSpec((None, k, tn), lambda i, j: (layer, 0, j)),
                  pl.BlockSpec((tm, tn), lambda i, j: (i, j))],
        out_specs=pl.BlockSpec((tm, tn), lambda i, j: (i, j)),
        compiler_params=_params("parallel", "arbitrary"), name=name)(a, ss, w, res)


def _glu_res_kernel(a_ref, wv_ref, wg_ref, res_ref, o_ref):
    a = a_ref[...]
    val = _dot(a, wv_ref[...])
    gate = _dot(a, wg_ref[...])
    o_ref[...] = res_ref[...] + val * _sigmoid(gate)


def glu_matmul_residual(a_tm, w, layer, res, *, batch, tm, tn, name):
    s = a_tm.shape[0]
    k = w.shape[1]
    n = w.shape[2] // 2
    nt = s // tm
    return pl.pallas_call(
        _glu_res_kernel, out_shape=jax.ShapeDtypeStruct((batch * s, n), F32),
        grid=(batch * nt, n // tn),
        in_specs=[pl.BlockSpec((tm, k), lambda i, j: (i % nt, i // nt)),
                  pl.BlockSpec((None, k, tn), lambda i, j: (layer, 0, j)),
                  pl.BlockSpec((None, k, tn), lambda i, j: (layer, 0, j + n // tn)),
                  pl.BlockSpec((tm, tn), lambda i, j: (i, j))],
        out_specs=pl.BlockSpec((tm, tn), lambda i, j: (i, j)),
        compiler_params=_params("parallel", "arbitrary"), name=name)(a_tm, w, w, res)


def _mlp_kernel(x_ref, g_ref, w1_ref, w2_ref, gn_ref, o_ref, *rest, final):
    xn_ref = rest[-1]
    f = pl.program_id(1)

    @pl.when(f == 0)
    def _():
        x = x_ref[...]
        xn_ref[...] = _rms_scale(x, g_ref[...]).astype(BF16)
        o_ref[...] = x

    h1 = jnp.maximum(_dot(xn_ref[...], w1_ref[...]), 0.0)
    o_ref[...] += _dot((h1 * h1).astype(BF16), w2_ref[...])

    @pl.when(f == pl.num_programs(1) - 1)
    def _():
        normed = _rms_scale(o_ref[...], gn_ref[...])
        if final:
            o_ref[...] = normed
        else:
            rest[0][...] = normed.astype(BF16)


def mlp_block(x, g, w1, w2, layer, g_next, *, final, tm, tf, name):
    t, d = x.shape
    dff = w1.shape[2]
    row_spec = pl.BlockSpec((tm, d), lambda i, f: (i, 0))
    out_shape = [jax.ShapeDtypeStruct((t, d), F32)] + ([] if final else [jax.ShapeDtypeStruct((t, d), BF16)])
    outs = pl.pallas_call(
        functools.partial(_mlp_kernel, final=final),
        out_shape=out_shape, grid=(t // tm, dff // tf),
        in_specs=[row_spec,
                  pl.BlockSpec((1, d), lambda i, f: (0, 0)),
                  pl.BlockSpec((None, d, tf), lambda i, f: (layer, 0, f)),
                  pl.BlockSpec((None, tf, d), lambda i, f: (layer, f, 0)),
                  pl.BlockSpec((1, d), lambda i, f: (0, 0))],
        out_specs=[row_spec] * len(out_shape),
        scratch_shapes=[pltpu.VMEM((tm, d), BF16)],
        compiler_params=_params("parallel", "arbitrary"), name=name,
    )(x, g.reshape(1, d), w1, w2, g_next.reshape(1, d))
    return outs[0] if final else outs


def _causal_conv_silu(buf_ref, cur, w, b):
    l = cur.shape[0]
    buf_ref[CONV_HALO:CONV_HALO + l, :] = cur
    acc = b + w[SSD_CONV - 1:SSD_CONV, :] * cur
    for k in range(SSD_CONV - 1):
        start = CONV_HALO - (SSD_CONV - 1) + k
        acc = acc + w[k:k + 1, :] * buf_ref[start:start + l, :]
    buf_ref[0:CONV_HALO, :] = cur[l - CONV_HALO:, :]
    return _silu(acc)


def _ssd_head_rows(dt_raw_t, head_p, rows_ref):
    dtt = _softplus(dt_raw_t + head_p[:, 0:1])
    rows_ref[0:SSD_HPG, :] = _cumsum_cols(dtt * (-LOG2E * jnp.exp(head_p[:, 1:2])))
    rows_ref[SSD_HPG:, :] = dtt


def _ssd_kernel(z_ref, x_ref, b_ref, c_ref, dtt_ref, dttn_ref, cp_ref, hp_ref, r_ref,
                o_ref, ss_ref, xbuf, bbuf, cbuf, state_ref, rows_ref, wide_ref, xs_ref, bm_ref, cm_ref):
    gw, ns = SSD_GROUP_W, SSD_STATE

    @pl.when(pl.program_id(2) == 0)
    def _():
        state_ref[...] = jnp.zeros_like(state_ref)
        for buf in (xbuf, bbuf, cbuf):
            buf[:, 0:CONV_HALO, :] = jnp.zeros((buf.shape[0], CONV_HALO, buf.shape[2]), F32)
        for gi in range(SSD_GROUPS_PER_STEP):
            _ssd_head_rows(dtt_ref[gi * SSD_HPG:(gi + 1) * SSD_HPG, :], hp_ref[gi], rows_ref.at[gi])

    def prepare(gi):
        x_cols, bc_cols = pl.ds(gi * gw, gw), pl.ds(gi * ns, ns)
        return _ssd_group_prepare(
            x_ref.at[0, :, x_cols], b_ref.at[0, :, bc_cols], c_ref.at[0, :, bc_cols],
            dttn_ref.at[pl.ds(gi * SSD_HPG, SSD_HPG), :], cp_ref.at[gi], hp_ref.at[gi], r_ref,
            xbuf.at[gi], bbuf.at[gi], cbuf.at[gi], rows_ref.at[gi], wide_ref.at[gi],
            xs_ref.at[gi], bm_ref.at[gi], cm_ref.at[gi])

    for gi in range(SSD_GROUPS_PER_STEP):
        acst = prepare(gi)
        x_cols = pl.ds(gi * gw, gw)
        _ssd_group_main(acst, z_ref.at[0, :, x_cols], cp_ref.at[gi], o_ref.at[0, :, x_cols],
                        ss_ref.at[0, :, pl.ds(gi * LANES, LANES)], state_ref.at[gi], wide_ref.at[gi],
                        xs_ref.at[gi], bm_ref.at[gi], cm_ref.at[gi])


def _ssd_group_prepare(x_ref, b_ref, c_ref, dttn_ref, cp_ref, hp_ref, r_ref,
                       xbuf, bbuf, cbuf, rows_ref, wide_ref, xs_ref, bm_ref, cm_ref):
    gw, ns = SSD_GROUP_W, SSD_STATE
    acst = rows_ref[0:SSD_HPG, :]
    wide_ref[...] = _dot_tn(jnp.concatenate(_split3(rows_ref[...]), axis=0), r_ref[...])

    cp = cp_ref[...]
    cw, cbias = cp[0:SSD_CONV, :], cp[SSD_CONV:SSD_CONV + 1, :]
    xs_ref[...] = _causal_conv_silu(xbuf, x_ref[...], cw[:, :gw], cbias[:, :gw])
    bm_ref[...] = _causal_conv_silu(bbuf, b_ref[...], cw[:, gw:gw + ns], cbias[:, gw:gw + ns]).astype(BF16)
    cm_ref[...] = _causal_conv_silu(cbuf, c_ref[...], cw[:, gw + ns:], cbias[:, gw + ns:]).astype(BF16)

    _ssd_head_rows(dttn_ref[...], hp_ref[...], rows_ref)
    return acst


def _ssd_group_main(acst, z_ref, cp_ref, o_ref, ss_ref, state_ref, wide_ref, xs_ref, bm_ref, cm_ref):
    l = xs_ref.shape[0]
    sub = LANES
    nsub = l // sub
    hw = SSD_HEAD_DIM
    gw = SSD_GROUP_W
    n_a = SSD_HPG * sub + gw
    cp = cp_ref[...]
    xs, bm16, cm16 = xs_ref[...], bm_ref[...], cm_ref[...]
    acs_blk = wide_ref[:, :SSD_HPG * sub]
    acs64 = wide_ref[:, SSD_HPG * sub:n_a]
    dt64 = wide_ref[:, n_a:]
    last64 = acs64[l - 1:l, :]

    xdt = xs * dt64
    xdt16 = xdt.astype(BF16)
    st = state_ref[...]
    y = _dot(cm16, st.astype(BF16)) * jnp.exp2(acs64)
    state_ref[...] = st * jnp.exp2(last64) + _dot_tn(bm16, (xdt * jnp.exp2(last64 - acs64)).astype(BF16))
    y = y + cp[SSD_CONV + 1:SSD_CONV + 2, :gw] * xs

    cb = _dot_nt(cm16, bm16)
    diag_mask = _causal(sub)
    cb_blk = [[cb[r * sub:(r + 1) * sub, c * sub:(c + 1) * sub] if c < r else
               jnp.where(diag_mask, cb[r * sub:(r + 1) * sub, c * sub:(c + 1) * sub], 0.0)
               for c in range(r + 1)] for r in range(nsub)]
    lane = lax.broadcasted_iota(jnp.int32, (l, sub), 1)
    pair_out = []
    for p in range(SSD_HPG // 2):
        xp = xdt16[:, p * sub:(p + 1) * sub]
        rhs = [jnp.where(lane < hw, xp, jnp.zeros_like(xp)), jnp.where(lane >= hw, xp, jnp.zeros_like(xp))]
        rows = []
        for r in range(nsub):
            lhs_parts, rhs_parts = [], []
            for h in range(2):
                j = 2 * p + h
                col = acs_blk[r * sub:(r + 1) * sub, j * sub:(j + 1) * sub]
                for c in range(r + 1):
                    seg = col - acst[j:j + 1, c * sub:(c + 1) * sub]
                    if c == r:
                        seg = jnp.minimum(seg, 0.0)
                    lhs_parts.append((cb_blk[r][c] * jnp.exp2(seg)).astype(BF16))
                rhs_parts.append(rhs[h][:(r + 1) * sub, :])
            rows.append(_dot(jnp.concatenate(lhs_parts, axis=1), jnp.concatenate(rhs_parts, axis=0)))
        pair_out.append(jnp.concatenate(rows, axis=0))
    y = y + jnp.concatenate(pair_out, axis=1)
    y = y * _silu(z_ref[...])
    ss_ref[...] = jnp.broadcast_to(jnp.sum(y * y, axis=1, keepdims=True), (l, sub))
    o_ref[...] = (y * cp[SSD_CONV + 2:SSD_CONV + 3, :gw]).astype(o_ref.dtype)


def _head_expand(lanes_per_head, n_heads):
    return jnp.repeat(jnp.eye(n_heads, dtype=F32), lanes_per_head, axis=1)


def ssd_core(proj, dt_t, conv_w, conv_b, dt_bias, a_log, d_skip, norm_g, *, batch, seq):
    l = min(SSD_CHUNK, seq)
    nc = seq // l
    g_, j_ = SSD_GROUPS, SSD_HPG
    inner = g_ * SSD_GROUP_W
    nx = inner // SSD_GROUP_W
    nb = 2 * inner // SSD_STATE
    expand64 = _head_expand(SSD_HEAD_DIM, j_)
    r16 = jnp.concatenate([
        jnp.concatenate([_head_expand(LANES, j_), expand64, jnp.zeros_like(expand64)], axis=1),
        jnp.concatenate([jnp.zeros((j_, j_ * LANES + SSD_GROUP_W), F32), expand64], axis=1)], axis=0)
    r = jnp.concatenate([r16, r16, r16], axis=0).astype(BF16)
    bc_w = g_ * SSD_STATE

    def per_group(p, rows):
        parts = [p[:, :inner].reshape(rows, g_, SSD_GROUP_W), p[:, inner:inner + bc_w].reshape(rows, g_, SSD_STATE),
                 p[:, inner + bc_w:].reshape(rows, g_, SSD_STATE)]
        return jnp.concatenate(parts, axis=2).transpose(1, 0, 2)

    cp_w = SSD_GROUP_W + 2 * SSD_STATE
    x_only = ((0, 0), (0, 0), (0, cp_w - SSD_GROUP_W))
    dskip = jnp.pad(jnp.repeat(d_skip.reshape(g_, 1, j_), SSD_HEAD_DIM, axis=2), x_only)
    gain = jnp.pad(norm_g.reshape(g_, 1, SSD_GROUP_W), x_only)
    conv_p = jnp.concatenate([per_group(conv_w, SSD_CONV), per_group(conv_b.reshape(1, -1), 1), dskip, gain,
                              jnp.zeros((g_, 8 - SSD_CONV - 3, cp_w), F32)], axis=1).astype(F32)
    head_p = jnp.pad(jnp.stack([dt_bias.reshape(g_, j_), a_log.reshape(g_, j_)], axis=2),
                     ((0, 0), (0, 0), (0, LANES - 2))).astype(F32)
    gps = SSD_GROUPS_PER_STEP
    xw, bw = gps * SSD_GROUP_W, gps * SSD_STATE
    in_specs = [
        pl.BlockSpec((1, l, xw), lambda b, g, c: (b, c, g)),
        pl.BlockSpec((1, l, xw), lambda b, g, c: (b, c, nx // gps + g)),
        pl.BlockSpec((1, l, bw), lambda b, g, c: (b, c, nb // gps + g)),
        pl.BlockSpec((1, l, bw), lambda b, g, c: (b, c, (nb + g_) // gps + g)),
        pl.BlockSpec((gps * j_, l), lambda b, g, c: (g, b * nc + c)),
        pl.BlockSpec((gps * j_, l), lambda b, g, c: (g, b * nc + jnp.minimum(c + 1, nc - 1))),
        pl.BlockSpec((gps, 8, cp_w), lambda b, g, c: (g, 0, 0)),
        pl.BlockSpec((gps, 8, LANES), lambda b, g, c: (g, 0, 0)),
        pl.BlockSpec(r.shape, lambda b, g, c: (0, 0)),
    ]
    return pl.pallas_call(
        _ssd_kernel,
        out_shape=[jax.ShapeDtypeStruct((batch, seq, inner), BF16),
                   jax.ShapeDtypeStruct((batch, seq, g_ * LANES), F32)],
        grid=(batch, g_ // gps, nc), in_specs=in_specs,
        out_specs=[pl.BlockSpec((1, l, xw), lambda b, g, c: (b, c, g)),
                   pl.BlockSpec((1, l, gps * LANES), lambda b, g, c: (b, c, g))],
        scratch_shapes=[pltpu.VMEM((gps, CONV_HALO + l, SSD_GROUP_W), F32),
                        pltpu.VMEM((gps, CONV_HALO + l, SSD_STATE), F32),
                        pltpu.VMEM((gps, CONV_HALO + l, SSD_STATE), F32),
                        pltpu.VMEM((gps, SSD_STATE, SSD_GROUP_W), F32),
                        pltpu.VMEM((gps, 2 * j_, l), F32),
                        pltpu.VMEM((gps, l, r.shape[1]), F32),
                        pltpu.VMEM((gps, l, SSD_GROUP_W), F32),
                        pltpu.VMEM((gps, l, SSD_STATE), BF16),
                        pltpu.VMEM((gps, l, SSD_STATE), BF16)],
        compiler_params=_params("parallel", "parallel", "arbitrary"), name="ssd_core",
    )(proj, proj, proj, proj, dt_t, dt_t, conv_p, head_p, r)


def _mlstm_kernel(q_ref, k_ref, v_ref, o_ref, gc_ref, gr_ref, gbc_ref, gbr_ref, hg_ref,
                  out_ref, c_ref, n_ref, m_ref):
    @pl.when(pl.program_id(1) == 0)
    def _():
        c_ref[...] = jnp.zeros_like(c_ref)
        n_ref[...] = jnp.zeros_like(n_ref)
        m_ref[...] = jnp.zeros_like(m_ref)

    for hi in range(MLSTM_HEADS):
        qk_cols, v_cols = pl.ds(hi * MLSTM_DQK, MLSTM_DQK), pl.ds(hi * MLSTM_DV, MLSTM_DV)
        _mlstm_head(q_ref.at[0, :, qk_cols], k_ref.at[0, :, qk_cols], v_ref.at[0, :, v_cols], o_ref.at[0, :, v_cols],
                    gc_ref.at[0, hi], gr_ref.at[0, hi], gbc_ref.at[hi], gbr_ref.at[hi], hg_ref.at[:, v_cols],
                    out_ref.at[0, :, v_cols], c_ref.at[hi], n_ref.at[hi], m_ref.at[hi])


def _mlstm_head(q_ref, k_ref, v_ref, o_ref, gc_ref, gr_ref, gbc_ref, gbr_ref, hg_ref,
                out_ref, c_ref, n_ref, m_ref):
    l = q_ref.shape[0]
    q = q_ref[...] * (MLSTM_DQK ** -0.5)
    k = k_ref[...]
    v16 = v_ref[...].astype(BF16)
    q16 = q.astype(BF16)
    gcol = gc_ref[...] + gbc_ref[...]
    grow = gr_ref[...] + gbr_ref[...]
    i_col, lf_col = gcol[:, 0:1], _log_sigmoid(gcol[:, 1:2])
    i_row, lf_row = grow[0:1, :], _log_sigmoid(grow[1:2, :])
    bcum = _cumsum_rows(lf_col)
    bcum_row = _cumsum_cols(lf_row)
    m_st = m_ref[0:1, 0:1]

    causal = _causal(l)
    intra = jnp.where(causal, bcum - bcum_row + i_row, -jnp.inf)
    inter = bcum + m_st
    m_t = jnp.maximum(inter, jnp.max(intra, axis=1, keepdims=True))
    w = jnp.exp(intra - m_t)
    scale_inter = jnp.exp(inter - m_t)
    qk = _dot_nt(q16, k.astype(BF16)) * w
    num = _dot(qk.astype(BF16), v16) + scale_inter * _dot(q16, c_ref[...].astype(BF16))
    qn = jnp.sum(q * n_ref[...], axis=1, keepdims=True)
    den = jnp.sum(qk, axis=1, keepdims=True) + scale_inter * qn
    h = num / jnp.maximum(jnp.abs(den), jnp.exp(-m_t))

    b_last = bcum[l - 1:l, :]
    tail = b_last - bcum + i_col
    m_new = jnp.maximum(b_last + m_st, jnp.max(tail, axis=0, keepdims=True))
    carry_scale = jnp.exp(b_last + m_st - m_new)
    wk = k * jnp.exp(tail - m_new)
    c_ref[...] = carry_scale * c_ref[...] + _dot_tn(wk.astype(BF16), v16)
    n_ref[...] = carry_scale * n_ref[...] + jnp.sum(wk, axis=0, keepdims=True)
    m_ref[...] = jnp.broadcast_to(m_new, m_ref.shape)

    hn = _rms_scale(h, hg_ref[...])
    out_ref[...] = (_sigmoid(o_ref[...]) * hn).astype(out_ref.dtype)


def mlstm_core(proj, gates, gate_b, head_g, *, batch, seq):
    l = MLSTM_CHUNK
    h_ = MLSTM_HEADS
    qk_w = h_ * MLSTM_DQK
    v_w = h_ * MLSTM_DV
    gsplit = jnp.stack([gates[..., :h_], gates[..., h_:2 * h_]], axis=-1)
    gcol = gsplit.transpose(0, 2, 1, 3)
    grow = gsplit.transpose(0, 2, 3, 1)
    gb = jnp.stack([gate_b[:h_], gate_b[h_:]], axis=-1).astype(F32)
    return pl.pallas_call(
        _mlstm_kernel, out_shape=jax.ShapeDtypeStruct((batch, seq, v_w), BF16),
        grid=(batch, seq // l),
        in_specs=[pl.BlockSpec((1, l, qk_w), lambda b, c: (b, c, 0)),
                  pl.BlockSpec((1, l, qk_w), lambda b, c: (b, c, 1)),
                  pl.BlockSpec((1, l, v_w), lambda b, c: (b, c, 2 * qk_w // v_w)),
                  pl.BlockSpec((1, l, v_w), lambda b, c: (b, c, 2 * qk_w // v_w + 1)),
                  pl.BlockSpec((1, h_, l, 2), lambda b, c: (b, 0, c, 0)),
                  pl.BlockSpec((1, h_, 2, l), lambda b, c: (b, 0, 0, c)),
                  pl.BlockSpec((h_, 1, 2), lambda b, c: (0, 0, 0)),
                  pl.BlockSpec((h_, 2, 1), lambda b, c: (0, 0, 0)),
                  pl.BlockSpec((1, v_w), lambda b, c: (0, 0))],
        out_specs=pl.BlockSpec((1, l, v_w), lambda b, c: (b, c, 0)),
        scratch_shapes=[pltpu.VMEM((h_, MLSTM_DQK, MLSTM_DV), F32),
                        pltpu.VMEM((h_, 1, MLSTM_DQK), F32),
                        pltpu.VMEM((h_, 8, LANES), F32)],
        compiler_params=_params("parallel", "arbitrary"), name="mlstm_core",
    )(proj, proj, proj, proj, gcol, grow, gb.reshape(h_, 1, 2), gb.reshape(h_, 2, 1),
      head_g.reshape(1, v_w).astype(F32))


def _s5_discretize_kernel(logdt_ref, are_ref, aim_ref, bre_ref, bim_ref,
                          abre_ref, abim_ref, bbre_ref, bbim_ref):
    ar, ai = are_ref[...], aim_ref[...]
    dt = jnp.exp(logdt_ref[...])
    mag = jnp.exp(ar * dt)
    abar_re, abar_im = mag * jnp.cos(ai * dt), mag * jnp.sin(ai * dt)
    den = ar * ar + ai * ai
    zoh_re = ((abar_re - 1.0) * ar + abar_im * ai) / den
    zoh_im = (abar_im * ar - (abar_re - 1.0) * ai) / den
    abre_ref[...] = abar_re
    abim_ref[...] = abar_im
    for c in range(S5_GROUP):
        br, bi = bre_ref[c], bim_ref[c]
        bbre_ref[c] = zoh_re * br - zoh_im * bi
        bbim_ref[c] = zoh_re * bi + zoh_im * br


def s5_discretize(log_dt, a_re, a_im, b_re, b_im):
    g_, p_ = a_re.shape
    sds = jax.ShapeDtypeStruct
    return pl.pallas_call(
        _s5_discretize_kernel,
        out_shape=[sds((g_, p_), F32), sds((g_, p_), F32),
                   sds((S5_GROUP, g_, p_), F32), sds((S5_GROUP, g_, p_), F32)],
        name="s5_discretize",
    )(log_dt.reshape(g_, 1).astype(F32), a_re.astype(F32), a_im.astype(F32),
      b_re.astype(F32).transpose(2, 0, 1), b_im.astype(F32).transpose(2, 0, 1))


def _s5_kernel(u_ref, bw_ref, cw_ref, a_ref, d_ref, y_ref, st_ref, bu_ref, *, batch):
    ts = u_ref.shape[0]
    half = S5_BLOCK_STATE
    nblk = S5_BLOCKS_PER_STEP

    @pl.when(pl.program_id(1) == 0)
    def _():
        st_ref[...] = jnp.zeros_like(st_ref)

    def lanes(n):
        return slice(n * LANES, (n + 1) * LANES)

    def project_in(n):
        u = u_ref[:, :, lanes(n)].reshape(ts * batch, LANES)
        bu_ref[n] = _dot(u.astype(BF16), bw_ref[n])

    def scan(n):
        a = a_ref[n]
        a_re = jnp.broadcast_to(a[:, :half], (batch, half))
        a_im = jnp.broadcast_to(a[:, half:], (batch, half))
        s_re, s_im = st_ref[n, :, 0:half], st_ref[n, :, half:2 * half]
        for t in range(ts):
            rows = slice(t * batch, (t + 1) * batch)
            s_re, s_im = (a_re * s_re - a_im * s_im + bu_ref[n, rows, 0:half],
                          a_re * s_im + a_im * s_re + bu_ref[n, rows, half:2 * half])
            bu_ref[n, rows, 0:half] = s_re
            bu_ref[n, rows, half:2 * half] = s_im
        st_ref[n, :, 0:half] = s_re
        st_ref[n, :, half:2 * half] = s_im

    def project_out(n):
        u = u_ref[:, :, lanes(n)].reshape(ts * batch, LANES)
        y = _dot(bu_ref[n].astype(BF16), cw_ref[n]) + d_ref[:, lanes(n)] * u
        y_ref[:, :, lanes(n)] = jax.nn.gelu(y).astype(y_ref.dtype).reshape(ts, batch, LANES)

    project_in(0)
    for n in range(nblk):
        if n + 1 < nblk:
            project_in(n + 1)
        scan(n)
        if n > 0:
            project_out(n - 1)
    project_out(nblk - 1)


def s5_core(u_tm, bw, cw, a_vec, d_skip, *, batch, seq):
    width = u_tm.shape[2]
    nblk = width // LANES
    ts = min(S5_TIME_BLOCK, seq)
    per = S5_BLOCKS_PER_STEP
    return pl.pallas_call(
        functools.partial(_s5_kernel, batch=batch),
        out_shape=jax.ShapeDtypeStruct((seq, batch, width), BF16),
        grid=(nblk // per, seq // ts),
        in_specs=[pl.BlockSpec((ts, batch, per * LANES), lambda j, i: (i, 0, j)),
                  pl.BlockSpec((per, LANES, 2 * S5_BLOCK_STATE), lambda j, i: (j, 0, 0)),
                  pl.BlockSpec((per, 2 * S5_BLOCK_STATE, LANES), lambda j, i: (j, 0, 0)),
                  pl.BlockSpec((per, 1, 2 * S5_BLOCK_STATE), lambda j, i: (j, 0, 0)),
                  pl.BlockSpec((1, per * LANES), lambda j, i: (0, j))],
        out_specs=pl.BlockSpec((ts, batch, per * LANES), lambda j, i: (i, 0, j)),
        scratch_shapes=[pltpu.VMEM((per, batch, 2 * S5_BLOCK_STATE), F32),
                        pltpu.VMEM((per, ts * batch, 2 * S5_BLOCK_STATE), F32)],
        compiler_params=_params("parallel", "arbitrary"), name="s5_core",
    )(u_tm, bw, cw, a_vec, d_skip.reshape(1, width).astype(F32))


def _s5_block_weights(abar_re, abar_im, bbar_re, bbar_im, c_re, c_im):
    g_, p_ = abar_re.shape
    gb = S5_GROUPS_PER_BLOCK
    nblk = g_ // gb
    eye = jnp.eye(gb, dtype=F32)

    def in_blocks(bbar):
        b = bbar.reshape(S5_GROUP, nblk, gb, p_).transpose(1, 2, 0, 3)
        return jnp.einsum("ngcp,gh->ngchp", b, eye).reshape(nblk, gb * S5_GROUP, gb * p_)

    def out_blocks(cmat):
        c = cmat.reshape(nblk, gb, S5_GROUP, p_).transpose(0, 1, 3, 2)
        return jnp.einsum("ngpc,gh->ngphc", c, eye).reshape(nblk, gb * p_, gb * S5_GROUP)

    bw = jnp.concatenate([in_blocks(bbar_re), in_blocks(bbar_im)], axis=2).astype(BF16)
    cw = jnp.concatenate([out_blocks(c_re.astype(F32)), out_blocks(-c_im.astype(F32))], axis=1).astype(BF16)
    a_vec = jnp.concatenate([abar_re.reshape(nblk, 1, gb * p_), abar_im.reshape(nblk, 1, gb * p_)], axis=2)
    return bw, cw, a_vec


SSD_MAIN_COLS = 2 * SSD_GROUPS * SSD_GROUP_W + 2 * SSD_GROUPS * SSD_STATE
MLSTM_MAIN_COLS = 2 * MLSTM_HEADS * (MLSTM_DQK + MLSTM_DV)
TOKEN_TILE = 1024
COL_TILE = 1024
MLP_TOKEN_TILE = 1024
MLP_FF_TILE = 512


def _small_head(w_in, idx, main, transposed):
    w = jnp.pad(w_in[idx, :, main:], ((0, 0), (0, LANES - (w_in.shape[2] - main)))).astype(BF16)
    return w.T if transposed else w


def _ssd_layer(h, xin, g, w_in, w_in16, w_out16, idx, conv_w, conv_b, dt_bias, a_log, d_skip, norm_g, *, batch, seq):
    inner = SSD_GROUPS * SSD_GROUP_W
    main = SSD_MAIN_COLS
    proj, dt_t = in_proj(xin, g, w_in16, idx, n=main, w_small=_small_head(w_in, idx, main, True), small="cols",
                         tm=TOKEN_TILE, tn=COL_TILE, name="ssd_in_proj")
    y16, ss = ssd_core(proj.reshape(batch, seq, main), dt_t, conv_w, conv_b, dt_bias, a_log, d_skip, norm_g,
                       batch=batch, seq=seq)
    return rms_matmul_residual(y16.reshape(batch * seq, inner), ss.reshape(batch * seq, -1), w_out16, idx, h,
                               tm=TOKEN_TILE, tn=COL_TILE // 2, name="ssd_out_proj")


def _mlstm_layer(h, xin, g, w_in, w_in16, w_out16, idx, gate_b, head_g, *, batch, seq):
    main = MLSTM_MAIN_COLS
    proj, gates = in_proj(xin, g, w_in16, idx, n=main, w_small=_small_head(w_in, idx, main, False), small="rows",
                          tm=TOKEN_TILE, tn=COL_TILE, name="mlstm_in_proj")
    hs = mlstm_core(proj.reshape(batch, seq, main),
                    gates[:, :2 * MLSTM_HEADS].reshape(batch, seq, 2 * MLSTM_HEADS),
                    gate_b, head_g, batch=batch, seq=seq)
    return matmul_residual(hs.reshape(batch * seq, -1), w_out16, idx, h,
                           tm=TOKEN_TILE, tn=COL_TILE, name="mlstm_out_proj")


def _s5_layer(h, xin, g, w_in16, w_out16, idx, b_re, b_im, c_re, c_im, d_skip, log_dt, a_re, a_im, *, batch, seq):
    width = w_in16.shape[2]
    tm = min(TOKEN_TILE, seq)
    nt = seq // tm
    ncol = width // COL_TILE
    u_tm = in_proj(xin, g, w_in16, idx, n=width, tm=tm, tn=COL_TILE,
                   out_index=lambda i, j: (i % nt, (i // nt) * ncol + j),
                   out_shape=(seq, batch * width), name="s5_in_proj")
    abar_re, abar_im, bbar_re, bbar_im = s5_discretize(log_dt, a_re, a_im, b_re, b_im)
    bw, cw, a_vec = _s5_block_weights(abar_re, abar_im, bbar_re, bbar_im, c_re, c_im)
    y_tm = s5_core(u_tm.reshape(seq, batch, width), bw, cw, a_vec, d_skip, batch=batch, seq=seq)
    return glu_matmul_residual(y_tm.reshape(seq, batch * width), w_out16, idx, h,
                               batch=batch, tm=tm, tn=COL_TILE, name="s5_out_proj")


def kernel(x, norm_mix_g, norm_mlp_g, ssd_w_in, ssd_conv_w, ssd_conv_b, ssd_dt_bias, ssd_a_log, ssd_d, ssd_norm_g, ssd_w_out, mlstm_w_in, mlstm_gate_b, mlstm_head_g, mlstm_w_out, s5_w_in, s5_b_re, s5_b_im, s5_c_re, s5_c_im, s5_d, s5_log_dt, s5_a_re, s5_a_im, s5_w_out, mlp_w1, mlp_w2, final_norm_g):
    batch, seq, d = x.shape
    depth = norm_mix_g.shape[0]
    ssd_in16, ssd_out16 = ssd_w_in[:, :, :SSD_MAIN_COLS].astype(BF16), ssd_w_out.astype(BF16)
    mlstm_in16, mlstm_out16 = mlstm_w_in[:, :, :MLSTM_MAIN_COLS].astype(BF16), mlstm_w_out.astype(BF16)
    s5_in16, s5_out16 = s5_w_in.astype(BF16), s5_w_out.astype(BF16)
    mlp_w1_16, mlp_w2_16 = mlp_w1.astype(BF16), mlp_w2.astype(BF16)
    h = x.reshape(batch * seq, d)
    xin, g = h, norm_mix_g[0]
    for layer in range(depth):
        kind, idx = layer % 3, layer // 3
        if kind == 0:
            h = _ssd_layer(h, xin, g, ssd_w_in, ssd_in16, ssd_out16, idx, ssd_conv_w[idx], ssd_conv_b[idx],
                           ssd_dt_bias[idx], ssd_a_log[idx], ssd_d[idx], ssd_norm_g[idx], batch=batch, seq=seq)
        elif kind == 1:
            h = _mlstm_layer(h, xin, g, mlstm_w_in, mlstm_in16, mlstm_out16, idx, mlstm_gate_b[idx],
                             mlstm_head_g[idx], batch=batch, seq=seq)
        else:
            h = _s5_layer(h, xin, g, s5_in16, s5_out16, idx, s5_b_re[idx], s5_b_im[idx], s5_c_re[idx], s5_c_im[idx],
                          s5_d[idx], s5_log_dt[idx], s5_a_re[idx], s5_a_im[idx], batch=batch, seq=seq)
        final = layer == depth - 1
        out = mlp_block(h, norm_mlp_g[layer], mlp_w1_16, mlp_w2_16, layer,
                        final_norm_g if final else norm_mix_g[layer + 1], final=final,
                        tm=min(MLP_TOKEN_TILE, batch * seq), tf=MLP_FF_TILE, name=f"mlp_{layer}")
        if final:
            h = out
        else:
            h, xin = out
            g = None
    return h.reshape(batch, seq, d)
```

```python
import functools
import math

import jax
import jax.numpy as jnp
from jax import lax
from jax.experimental import pallas as pl
from jax.experimental.pallas import tpu as pltpu

F32 = jnp.float32
BF16 = jnp.bfloat16

NORM_EPS = 1e-5
LOG2E = math.log2(math.e)
LANES = 128
VMEM_LIMIT_BYTES = 56 * 1024 * 1024

SSD_HEAD_DIM = 64
SSD_GROUPS = 8
SSD_HPG = 8
SSD_STATE = 128
SSD_CONV = 4
SSD_GROUP_W = SSD_HPG * SSD_HEAD_DIM
SSD_CHUNK = 256
SSD_GROUPS_PER_STEP = 8
CONV_HALO = 8

MLSTM_HEADS = 4
MLSTM_DQK = 256
MLSTM_DV = 512
MLSTM_CHUNK = 256

S5_GROUP = 16
S5_STATE = 64
S5_GROUPS_PER_BLOCK = LANES // S5_GROUP
S5_BLOCK_STATE = S5_GROUPS_PER_BLOCK * S5_STATE
S5_TIME_BLOCK = 64
S5_BLOCKS_PER_STEP = 4


def _params(*semantics):
    return pltpu.CompilerParams(dimension_semantics=semantics, vmem_limit_bytes=VMEM_LIMIT_BYTES)


def _dot(a, b):
    return jnp.dot(a, b, preferred_element_type=F32)


def _dot_nt(a, b):
    return lax.dot_general(a, b, (((1,), (1,)), ((), ())), preferred_element_type=F32)


def _dot_tn(a, b):
    return lax.dot_general(a, b, (((0,), (0,)), ((), ())), preferred_element_type=F32)


def _split3(x):
    hi = x.astype(BF16)
    r1 = x - hi.astype(F32)
    mid = r1.astype(BF16)
    lo = (r1 - mid.astype(F32)).astype(BF16)
    return hi, mid, lo


def _tri(n, upper):
    r = lax.broadcasted_iota(jnp.int32, (n, n), 0)
    c = lax.broadcasted_iota(jnp.int32, (n, n), 1)
    keep = (r <= c) if upper else (c <= r)
    return jnp.where(keep, 1.0, 0.0).astype(BF16)


def _cumsum_rows(x):
    t = _tri(x.shape[0], upper=False)
    hi, mid, lo = _split3(x)
    return _dot(t, hi) + _dot(t, mid) + _dot(t, lo)


def _cumsum_cols(x):
    t = _tri(x.shape[1], upper=True)
    hi, mid, lo = _split3(x)
    return _dot(hi, t) + _dot(mid, t) + _dot(lo, t)


def _causal(n):
    r = lax.broadcasted_iota(jnp.int32, (n, n), 0)
    c = lax.broadcasted_iota(jnp.int32, (n, n), 1)
    return c <= r


def _softplus(x):
    return jnp.maximum(x, 0.0) + jnp.log1p(jnp.exp(-jnp.abs(x)))


def _sigmoid(x):
    return 1.0 / (1.0 + jnp.exp(-x))


def _silu(x):
    return x * _sigmoid(x)


def _log_sigmoid(x):
    return -_softplus(-x)


def _rms_scale(x, g):
    ms = jnp.mean(x * x, axis=-1, keepdims=True)
    return x * lax.rsqrt(ms + NORM_EPS) * g


def _in_proj_kernel(*refs, normalize, small, w_transposed):
    it = iter(refs)
    x_ref = next(it)
    g_ref = next(it) if normalize else None
    w_ref = next(it)
    ws_ref = next(it) if small else None
    o_ref = next(it)
    os_ref = next(it) if small else None
    xn_ref = next(it) if normalize else x_ref

    if normalize or small:
        @pl.when(pl.program_id(1) == 0)
        def _():
            if normalize:
                xn_ref[...] = _rms_scale(x_ref[...], g_ref[...]).astype(BF16)
            if small == "rows":
                os_ref[...] = _dot(xn_ref[...], ws_ref[...])
            elif small == "cols":
                os_ref[...] = _dot_nt(ws_ref[...], xn_ref[...])

    o_ref[...] = _dot_nt(xn_ref[...], w_ref[...]) if w_transposed else _dot(xn_ref[...], w_ref[...])


def in_proj(x, g, w, layer, *, n, w_transposed=False, w_small=None, small=None, tm, tn,
            out_index=None, out_shape=None, name):
    n_tok, k = x.shape
    assert n_tok % tm == 0 and n % tn == 0
    normalize = g is not None
    out_index = out_index or (lambda i, j: (i, j))
    in_specs = [pl.BlockSpec((tm, k), lambda i, j: (i, 0))]
    args = [x]
    if normalize:
        in_specs.append(pl.BlockSpec((1, k), lambda i, j: (0, 0)))
        args.append(g.reshape(1, k).astype(F32))
    in_specs.append(pl.BlockSpec((None, tn, k), lambda i, j: (layer, j, 0)) if w_transposed else
                    pl.BlockSpec((None, k, tn), lambda i, j: (layer, 0, j)))
    args.append(w)
    out_shapes = [jax.ShapeDtypeStruct(out_shape or (n_tok, n), F32)]
    out_specs = [pl.BlockSpec((tm, tn), out_index)]
    if small:
        in_specs.append(pl.BlockSpec(w_small.shape, lambda i, j: (0, 0)))
        args.append(w_small)
        if small == "cols":
            out_shapes.append(jax.ShapeDtypeStruct((LANES, n_tok), F32))
            out_specs.append(pl.BlockSpec((LANES, tm), lambda i, j: (0, i)))
        else:
            out_shapes.append(jax.ShapeDtypeStruct((n_tok, LANES), F32))
            out_specs.append(pl.BlockSpec((tm, LANES), lambda i, j: (i, 0)))
    outs = pl.pallas_call(
        functools.partial(_in_proj_kernel, normalize=normalize, small=small, w_transposed=w_transposed),
        out_shape=out_shapes, grid=(n_tok // tm, n // tn), in_specs=in_specs, out_specs=out_specs,
        scratch_shapes=[pltpu.VMEM((tm, k), BF16)] if normalize else [],
        compiler_params=_params("parallel", "arbitrary"), name=name)(*args)
    return outs if small else outs[0]


def _matmul_res_kernel(a_ref, w_ref, res_ref, o_ref):
    o_ref[...] = res_ref[...] + _dot(a_ref[...], w_ref[...])


def matmul_residual(a, w, layer, res, *, tm, tn, name):
    t, k = a.shape
    n = w.shape[2]
    return pl.pallas_call(
        _matmul_res_kernel, out_shape=jax.ShapeDtypeStruct((t, n), F32),
        grid=(t // tm, n // tn),
        in_specs=[pl.BlockSpec((tm, k), lambda i, j: (i, 0)),
                  pl.BlockSpec((None, k, tn), lambda i, j: (layer, 0, j)),
                  pl.BlockSpec((tm, tn), lambda i, j: (i, j))],
        out_specs=pl.BlockSpec((tm, tn), lambda i, j: (i, j)),
        compiler_params=_params("parallel", "arbitrary"), name=name)(a, w, res)


def _rms_matmul_res_kernel(a_ref, ss_ref, w_ref, res_ref, o_ref):
    k = a_ref.shape[1]
    ss = ss_ref[...]
    tot = ss[:, :LANES]
    for q in range(1, ss.shape[1] // LANES):
        tot = tot + ss[:, q * LANES:(q + 1) * LANES]
    inv = lax.rsqrt(tot * (1.0 / k) + NORM_EPS)
    y = _dot(a_ref[...], w_ref[...])
    o_ref[...] = res_ref[...] + jnp.concatenate([inv] * (y.shape[1] // LANES), axis=1) * y


def rms_matmul_residual(a, ss, w, layer, res, *, tm, tn, name):
    t, k = a.shape
    n = w.shape[2]
    return pl.pallas_call(
        _rms_matmul_res_kernel, out_shape=jax.ShapeDtypeStruct((t, n), F32),
        grid=(t // tm, n // tn),
        in_specs=[pl.BlockSpec((tm, k), lambda i, j: (i, 0)),
                  pl.BlockSpec((tm, ss.shape[1]), lambda i, j: (i, 0)),
                  pl.BlockSpec((None, k, tn), lambda i, j: (layer, 0, j)),
                  pl.BlockSpec((tm, tn), lambda i, j: (i, j))],
        out_specs=pl.BlockSpec((tm, tn), lambda i, j: (i, j)),
        compiler_params=_params("parallel", "arbitrary"), name=name)(a, ss, w, res)


def _glu_res_kernel(a_ref, wv_ref, wg_ref, res_ref, o_ref):
    a = a_ref[...]
    val = _dot(a, wv_ref[...])
    gate = _dot(a, wg_ref[...])
    o_ref[...] = res_ref[...] + val * _sigmoid(gate)


def glu_matmul_residual(a_tm, w, layer, res, *, batch, tm, tn, name):
    s = a_tm.shape[0]
    k = w.shape[1]
    n = w.shape[2] // 2
    nt = s // tm
    return pl.pallas_call(
        _glu_res_kernel, out_shape=jax.ShapeDtypeStruct((batch * s, n), F32),
        grid=(batch * nt, n // tn),
        in_specs=[pl.BlockSpec((tm, k), lambda i, j: (i % nt, i // nt)),
                  pl.BlockSpec((None, k, tn), lambda i, j: (layer, 0, j)),
                  pl.BlockSpec((None, k, tn), lambda i, j: (layer, 0, j + n // tn)),
                  pl.BlockSpec((tm, tn), lambda i, j: (i, j))],
        out_specs=pl.BlockSpec((tm, tn), lambda i, j: (i, j)),
        compiler_params=_params("parallel", "arbitrary"), name=name)(a_tm, w, w, res)


def _mlp_kernel(x_ref, g_ref, w1_ref, w2_ref, gn_ref, o_ref, *rest, final):
    xn_ref = rest[-1]
    f = pl.program_id(1)

    @pl.when(f == 0)
    def _():
        x = x_ref[...]
        xn_ref[...] = _rms_scale(x, g_ref[...]).astype(BF16)
        o_ref[...] = x

    h1 = jnp.maximum(_dot(xn_ref[...], w1_ref[...]), 0.0)
    o_ref[...] += _dot((h1 * h1).astype(BF16), w2_ref[...])

    @pl.when(f == pl.num_programs(1) - 1)
    def _():
        normed = _rms_scale(o_ref[...], gn_ref[...])
        if final:
            o_ref[...] = normed
        else:
            rest[0][...] = normed.astype(BF16)


def mlp_block(x, g, w1, w2, layer, g_next, *, final, tm, tf, name):
    t, d = x.shape
    dff = w1.shape[2]
    row_spec = pl.BlockSpec((tm, d), lambda i, f: (i, 0))
    out_shape = [jax.ShapeDtypeStruct((t, d), F32)] + ([] if final else [jax.ShapeDtypeStruct((t, d), BF16)])
    outs = pl.pallas_call(
        functools.partial(_mlp_kernel, final=final),
        out_shape=out_shape, grid=(t // tm, dff // tf),
        in_specs=[row_spec,
                  pl.BlockSpec((1, d), lambda i, f: (0, 0)),
                  pl.BlockSpec((None, d, tf), lambda i, f: (layer, 0, f)),
                  pl.BlockSpec((None, tf, d), lambda i, f: (layer, f, 0)),
                  pl.BlockSpec((1, d), lambda i, f: (0, 0))],
        out_specs=[row_spec] * len(out_shape),
        scratch_shapes=[pltpu.VMEM((tm, d), BF16)],
        compiler_params=_params("parallel", "arbitrary"), name=name,
    )(x, g.reshape(1, d), w1, w2, g_next.reshape(1, d))
    return outs[0] if final else outs


def _causal_conv_silu(buf_ref, cur, w, b):
    l = cur.shape[0]
    buf_ref[CONV_HALO:CONV_HALO + l, :] = cur
    acc = b + w[SSD_CONV - 1:SSD_CONV, :] * cur
    for k in range(SSD_CONV - 1):
        start = CONV_HALO - (SSD_CONV - 1) + k
        acc = acc + w[k:k + 1, :] * buf_ref[start:start + l, :]
    buf_ref[0:CONV_HALO, :] = cur[l - CONV_HALO:, :]
    return _silu(acc)


def _ssd_head_rows(dt_raw_t, head_p, rows_ref):
    dtt = _softplus(dt_raw_t + head_p[:, 0:1])
    rows_ref[0:SSD_HPG, :] = _cumsum_cols(dtt * (-LOG2E * jnp.exp(head_p[:, 1:2])))
    rows_ref[SSD_HPG:, :] = dtt


def _ssd_kernel(z_ref, x_ref, b_ref, c_ref, dtt_ref, dttn_ref, cp_ref, hp_ref, r_ref,
                o_ref, ss_ref, xbuf, bbuf, cbuf, state_ref, rows_ref, wide_ref, xs_ref, bm_ref, cm_ref):
    gw, ns = SSD_GROUP_W, SSD_STATE

    @pl.when(pl.program_id(2) == 0)
    def _():
        state_ref[...] = jnp.zeros_like(state_ref)
        for buf in (xbuf, bbuf, cbuf):
            buf[:, 0:CONV_HALO, :] = jnp.zeros((buf.shape[0], CONV_HALO, buf.shape[2]), F32)
        for gi in range(SSD_GROUPS_PER_STEP):
            _ssd_head_rows(dtt_ref[gi * SSD_HPG:(gi + 1) * SSD_HPG, :], hp_ref[gi], rows_ref.at[gi])

    def prepare(gi):
        x_cols, bc_cols = pl.ds(gi * gw, gw), pl.ds(gi * ns, ns)
        return _ssd_group_prepare(
            x_ref.at[0, :, x_cols], b_ref.at[0, :, bc_cols], c_ref.at[0, :, bc_cols],
            dttn_ref.at[pl.ds(gi * SSD_HPG, SSD_HPG), :], cp_ref.at[gi], hp_ref.at[gi], r_ref,
            xbuf.at[gi], bbuf.at[gi], cbuf.at[gi], rows_ref.at[gi], wide_ref.at[gi],
            xs_ref.at[gi], bm_ref.at[gi], cm_ref.at[gi])

    for gi in range(SSD_GROUPS_PER_STEP):
        acst = prepare(gi)
        x_cols = pl.ds(gi * gw, gw)
        _ssd_group_main(acst, z_ref.at[0, :, x_cols], cp_ref.at[gi], o_ref.at[0, :, x_cols],
                        ss_ref.at[0, :, pl.ds(gi * LANES, LANES)], state_ref.at[gi], wide_ref.at[gi],
                        xs_ref.at[gi], bm_ref.at[gi], cm_ref.at[gi])


def _ssd_group_prepare(x_ref, b_ref, c_ref, dttn_ref, cp_ref, hp_ref, r_ref,
                       xbuf, bbuf, cbuf, rows_ref, wide_ref, xs_ref, bm_ref, cm_ref):
    gw, ns = SSD_GROUP_W, SSD_STATE
    acst = rows_ref[0:SSD_HPG, :]
    wide_ref[...] = _dot_tn(jnp.concatenate(_split3(rows_ref[...]), axis=0), r_ref[...])

    cp = cp_ref[...]
    cw, cbias = cp[0:SSD_CONV, :], cp[SSD_CONV:SSD_CONV + 1, :]
    xs_ref[...] = _causal_conv_silu(xbuf, x_ref[...], cw[:, :gw], cbias[:, :gw])
    bm_ref[...] = _causal_conv_silu(bbuf, b_ref[...], cw[:, gw:gw + ns], cbias[:, gw:gw + ns]).astype(BF16)
    cm_ref[...] = _causal_conv_silu(cbuf, c_ref[...], cw[:, gw + ns:], cbias[:, gw + ns:]).astype(BF16)

    _ssd_head_rows(dttn_ref[...], hp_ref[...], rows_ref)
    return acst


def _ssd_group_main(acst, z_ref, cp_ref, o_ref, ss_ref, state_ref, wide_ref, xs_ref, bm_ref, cm_ref):
    l = xs_ref.shape[0]
    sub = LANES
    nsub = l // sub
    hw = SSD_HEAD_DIM
    gw = SSD_GROUP_W
    n_a = SSD_HPG * sub + gw
    cp = cp_ref[...]
    xs, bm16, cm16 = xs_ref[...], bm_ref[...], cm_ref[...]
    acs_blk = wide_ref[:, :SSD_HPG * sub]
    acs64 = wide_ref[:, SSD_HPG * sub:n_a]
    dt64 = wide_ref[:, n_a:]
    last64 = acs64[l - 1:l, :]

    xdt = xs * dt64
    xdt16 = xdt.astype(BF16)
    st = state_ref[...]
    y = _dot(cm16, st.astype(BF16)) * jnp.exp2(acs64)
    state_ref[...] = st * jnp.exp2(last64) + _dot_tn(bm16, (xdt * jnp.exp2(last64 - acs64)).astype(BF16))
    y = y + cp[SSD_CONV + 1:SSD_CONV + 2, :gw] * xs

    cb = _dot_nt(cm16, bm16)
    diag_mask = _causal(sub)
    cb_blk = [[cb[r * sub:(r + 1) * sub, c * sub:(c + 1) * sub] if c < r else
               jnp.where(diag_mask, cb[r * sub:(r + 1) * sub, c * sub:(c + 1) * sub], 0.0)
               for c in range(r + 1)] for r in range(nsub)]
    lane = lax.broadcasted_iota(jnp.int32, (l, sub), 1)
    pair_out = []
    for p in range(SSD_HPG // 2):
        xp = xdt16[:, p * sub:(p + 1) * sub]
        rhs = [jnp.where(lane < hw, xp, jnp.zeros_like(xp)), jnp.where(lane >= hw, xp, jnp.zeros_like(xp))]
        rows = []
        for r in range(nsub):
            lhs_parts, rhs_parts = [], []
            for h in range(2):
                j = 2 * p + h
                col = acs_blk[r * sub:(r + 1) * sub, j * sub:(j + 1) * sub]
                for c in range(r + 1):
                    seg = col - acst[j:j + 1, c * sub:(c + 1) * sub]
                    if c == r:
                        seg = jnp.minimum(seg, 0.0)
                    lhs_parts.append((cb_blk[r][c] * jnp.exp2(seg)).astype(BF16))
                rhs_parts.append(rhs[h][:(r + 1) * sub, :])
            rows.append(_dot(jnp.concatenate(lhs_parts, axis=1), jnp.concatenate(rhs_parts, axis=0)))
        pair_out.append(jnp.concatenate(rows, axis=0))
    y = y + jnp.concatenate(pair_out, axis=1)
    y = y * _silu(z_ref[...])
    ss_ref[...] = jnp.broadcast_to(jnp.sum(y * y, axis=1, keepdims=True), (l, sub))
    o_ref[...] = (y * cp[SSD_CONV + 2:SSD_CONV + 3, :gw]).astype(o_ref.dtype)


def _head_expand(lanes_per_head, n_heads):
    return jnp.repeat(jnp.eye(n_heads, dtype=F32), lanes_per_head, axis=1)


def ssd_core(proj, dt_t, conv_w, conv_b, dt_bias, a_log, d_skip, norm_g, *, batch, seq):
    l = min(SSD_CHUNK, seq)
    nc = seq // l
    g_, j_ = SSD_GROUPS, SSD_HPG
    inner = g_ * SSD_GROUP_W
    nx = inner // SSD_GROUP_W
    nb = 2 * inner // SSD_STATE
    expand64 = _head_expand(SSD_HEAD_DIM, j_)
    r16 = jnp.concatenate([
        jnp.concatenate([_head_expand(LANES, j_), expand64, jnp.zeros_like(expand64)], axis=1),
        jnp.concatenate([jnp.zeros((j_, j_ * LANES + SSD_GROUP_W), F32), expand64], axis=1)], axis=0)
    r = jnp.concatenate([r16, r16, r16], axis=0).astype(BF16)
    bc_w = g_ * SSD_STATE

    def per_group(p, rows):
        parts = [p[:, :inner].reshape(rows, g_, SSD_GROUP_W), p[:, inner:inner + bc_w].reshape(rows, g_, SSD_STATE),
                 p[:, inner + bc_w:].reshape(rows, g_, SSD_STATE)]
        return jnp.concatenate(parts, axis=2).transpose(1, 0, 2)

    cp_w = SSD_GROUP_W + 2 * SSD_STATE
    x_only = ((0, 0), (0, 0), (0, cp_w - SSD_GROUP_W))
    dskip = jnp.pad(jnp.repeat(d_skip.reshape(g_, 1, j_), SSD_HEAD_DIM, axis=2), x_only)
    gain = jnp.pad(norm_g.reshape(g_, 1, SSD_GROUP_W), x_only)
    conv_p = jnp.concatenate([per_group(conv_w, SSD_CONV), per_group(conv_b.reshape(1, -1), 1), dskip, gain,
                              jnp.zeros((g_, 8 - SSD_CONV - 3, cp_w), F32)], axis=1).astype(F32)
    head_p = jnp.pad(jnp.stack([dt_bias.reshape(g_, j_), a_log.reshape(g_, j_)], axis=2),
                     ((0, 0), (0, 0), (0, LANES - 2))).astype(F32)
    gps = SSD_GROUPS_PER_STEP
    xw, bw = gps * SSD_GROUP_W, gps * SSD_STATE
    in_specs = [
        pl.BlockSpec((1, l, xw), lambda b, g, c: (b, c, g)),
        pl.BlockSpec((1, l, xw), lambda b, g, c: (b, c, nx // gps + g)),
        pl.BlockSpec((1, l, bw), lambda b, g, c: (b, c, nb // gps + g)),
        pl.BlockSpec((1, l, bw), lambda b, g, c: (b, c, (nb + g_) // gps + g)),
        pl.BlockSpec((gps * j_, l), lambda b, g, c: (g, b * nc + c)),
        pl.BlockSpec((gps * j_, l), lambda b, g, c: (g, b * nc + jnp.minimum(c + 1, nc - 1))),
        pl.BlockSpec((gps, 8, cp_w), lambda b, g, c: (g, 0, 0)),
        pl.BlockSpec((gps, 8, LANES), lambda b, g, c: (g, 0, 0)),
        pl.BlockSpec(r.shape, lambda b, g, c: (0, 0)),
    ]
    return pl.pallas_call(
        _ssd_kernel,
        out_shape=[jax.ShapeDtypeStruct((batch, seq, inner), BF16),
                   jax.ShapeDtypeStruct((batch, seq, g_ * LANES), F32)],
        grid=(batch, g_ // gps, nc), in_specs=in_specs,
        out_specs=[pl.BlockSpec((1, l, xw), lambda b, g, c: (b, c, g)),
                   pl.BlockSpec((1, l, gps * LANES), lambda b, g, c: (b, c, g))],
        scratch_shapes=[pltpu.VMEM((gps, CONV_HALO + l, SSD_GROUP_W), F32),
                        pltpu.VMEM((gps, CONV_HALO + l, SSD_STATE), F32),
                        pltpu.VMEM((gps, CONV_HALO + l, SSD_STATE), F32),
                        pltpu.VMEM((gps, SSD_STATE, SSD_GROUP_W), F32),
                        pltpu.VMEM((gps, 2 * j_, l), F32),
                        pltpu.VMEM((gps, l, r.shape[1]), F32),
                        pltpu.VMEM((gps, l, SSD_GROUP_W), F32),
                        pltpu.VMEM((gps, l, SSD_STATE), BF16),
                        pltpu.VMEM((gps, l, SSD_STATE), BF16)],
        compiler_params=_params("parallel", "parallel", "arbitrary"), name="ssd_core",
    )(proj, proj, proj, proj, dt_t, dt_t, conv_p, head_p, r)


def _mlstm_kernel(q_ref, k_ref, v_ref, o_ref, gc_ref, gr_ref, gbc_ref, gbr_ref, hg_ref,
                  out_ref, c_ref, n_ref, m_ref):
    @pl.when(pl.program_id(1) == 0)
    def _():
        c_ref[...] = jnp.zeros_like(c_ref)
        n_ref[...] = jnp.zeros_like(n_ref)
        m_ref[...] = jnp.zeros_like(m_ref)

    for hi in range(MLSTM_HEADS):
        qk_cols, v_cols = pl.ds(hi * MLSTM_DQK, MLSTM_DQK), pl.ds(hi * MLSTM_DV, MLSTM_DV)
        _mlstm_head(q_ref.at[0, :, qk_cols], k_ref.at[0, :, qk_cols], v_ref.at[0, :, v_cols], o_ref.at[0, :, v_cols],
                    gc_ref.at[0, hi], gr_ref.at[0, hi], gbc_ref.at[hi], gbr_ref.at[hi], hg_ref.at[:, v_cols],
                    out_ref.at[0, :, v_cols], c_ref.at[hi], n_ref.at[hi], m_ref.at[hi])


def _mlstm_head(q_ref, k_ref, v_ref, o_ref, gc_ref, gr_ref, gbc_ref, gbr_ref, hg_ref,
                out_ref, c_ref, n_ref, m_ref):
    l = q_ref.shape[0]
    q = q_ref[...] * (MLSTM_DQK ** -0.5)
    k = k_ref[...]
    v16 = v_ref[...].astype(BF16)
    q16 = q.astype(BF16)
    gcol = gc_ref[...] + gbc_ref[...]
    grow = gr_ref[...] + gbr_ref[...]
    i_col, lf_col = gcol[:, 0:1], _log_sigmoid(gcol[:, 1:2])
    i_row, lf_row = grow[0:1, :], _log_sigmoid(grow[1:2, :])
    bcum = _cumsum_rows(lf_col)
    bcum_row = _cumsum_cols(lf_row)
    m_st = m_ref[0:1, 0:1]

    causal = _causal(l)
    intra = jnp.where(causal, bcum - bcum_row + i_row, -jnp.inf)
    inter = bcum + m_st
    m_t = jnp.maximum(inter, jnp.max(intra, axis=1, keepdims=True))
    w = jnp.exp(intra - m_t)
    scale_inter = jnp.exp(inter - m_t)
    qk = _dot_nt(q16, k.astype(BF16)) * w
    num = _dot(qk.astype(BF16), v16) + scale_inter * _dot(q16, c_ref[...].astype(BF16))
    qn = jnp.sum(q * n_ref[...], axis=1, keepdims=True)
    den = jnp.sum(qk, axis=1, keepdims=True) + scale_inter * qn
    h = num / jnp.maximum(jnp.abs(den), jnp.exp(-m_t))

    b_last = bcum[l - 1:l, :]
    tail = b_last - bcum + i_col
    m_new = jnp.maximum(b_last + m_st, jnp.max(tail, axis=0, keepdims=True))
    carry_scale = jnp.exp(b_last + m_st - m_new)
    wk = k * jnp.exp(tail - m_new)
    c_ref[...] = carry_scale * c_ref[...] + _dot_tn(wk.astype(BF16), v16)
    n_ref[...] = carry_scale * n_ref[...] + jnp.sum(wk, axis=0, keepdims=True)
    m_ref[...] = jnp.broadcast_to(m_new, m_ref.shape)

    hn = _rms_scale(h, hg_ref[...])
    out_ref[...] = (_sigmoid(o_ref[...]) * hn).astype(out_ref.dtype)


def mlstm_core(proj, gates, gate_b, head_g, *, batch, seq):
    l = MLSTM_CHUNK
    h_ = MLSTM_HEADS
    qk_w = h_ * MLSTM_DQK
    v_w = h_ * MLSTM_DV
    gsplit = jnp.stack([gates[..., :h_], gates[..., h_:2 * h_]], axis=-1)
    gcol = gsplit.transpose(0, 2, 1, 3)
    grow = gsplit.transpose(0, 2, 3, 1)
    gb = jnp.stack([gate_b[:h_], gate_b[h_:]], axis=-1).astype(F32)
    return pl.pallas_call(
        _mlstm_kernel, out_shape=jax.ShapeDtypeStruct((batch, seq, v_w), BF16),
        grid=(batch, seq // l),
        in_specs=[pl.BlockSpec((1, l, qk_w), lambda b, c: (b, c, 0)),
                  pl.BlockSpec((1, l, qk_w), lambda b, c: (b, c, 1)),
                  pl.BlockSpec((1, l, v_w), lambda b, c: (b, c, 2 * qk_w // v_w)),
                  pl.BlockSpec((1, l, v_w), lambda b, c: (b, c, 2 * qk_w // v_w + 1)),
                  pl.BlockSpec((1, h_, l, 2), lambda b, c: (b, 0, c, 0)),
                  pl.BlockSpec((1, h_, 2, l), lambda b, c: (b, 0, 0, c)),
                  pl.BlockSpec((h_, 1, 2), lambda b, c: (0, 0, 0)),
                  pl.BlockSpec((h_, 2, 1), lambda b, c: (0, 0, 0)),
                  pl.BlockSpec((1, v_w), lambda b, c: (0, 0))],
        out_specs=pl.BlockSpec((1, l, v_w), lambda b, c: (b, c, 0)),
        scratch_shapes=[pltpu.VMEM((h_, MLSTM_DQK, MLSTM_DV), F32),
                        pltpu.VMEM((h_, 1, MLSTM_DQK), F32),
                        pltpu.VMEM((h_, 8, LANES), F32)],
        compiler_params=_params("parallel", "arbitrary"), name="mlstm_core",
    )(proj, proj, proj, proj, gcol, grow, gb.reshape(h_, 1, 2), gb.reshape(h_, 2, 1),
      head_g.reshape(1, v_w).astype(F32))


def _s5_discretize_kernel(logdt_ref, are_ref, aim_ref, bre_ref, bim_ref,
                          abre_ref, abim_ref, bbre_ref, bbim_ref):
    ar, ai = are_ref[...], aim_ref[...]
    dt = jnp.exp(logdt_ref[...])
    mag = jnp.exp(ar * dt)
    abar_re, abar_im = mag * jnp.cos(ai * dt), mag * jnp.sin(ai * dt)
    den = ar * ar + ai * ai
    zoh_re = ((abar_re - 1.0) * ar + abar_im * ai) / den
    zoh_im = (abar_im * ar - (abar_re - 1.0) * ai) / den
    abre_ref[...] = abar_re
    abim_ref[...] = abar_im
    for c in range(S5_GROUP):
        br, bi = bre_ref[c], bim_ref[c]
        bbre_ref[c] = zoh_re * br - zoh_im * bi
        bbim_ref[c] = zoh_re * bi + zoh_im * br


def s5_discretize(log_dt, a_re, a_im, b_re, b_im):
    g_, p_ = a_re.shape
    sds = jax.ShapeDtypeStruct
    return pl.pallas_call(
        _s5_discretize_kernel,
        out_shape=[sds((g_, p_), F32), sds((g_, p_), F32),
                   sds((S5_GROUP, g_, p_), F32), sds((S5_GROUP, g_, p_), F32)],
        name="s5_discretize",
    )(log_dt.reshape(g_, 1).astype(F32), a_re.astype(F32), a_im.astype(F32),
      b_re.astype(F32).transpose(2, 0, 1), b_im.astype(F32).transpose(2, 0, 1))


def _s5_kernel(u_ref, bw_ref, cw_ref, a_ref, d_ref, y_ref, st_ref, bu_ref, *, batch):
    ts = u_ref.shape[0]
    half = S5_BLOCK_STATE
    nblk = S5_BLOCKS_PER_STEP

    @pl.when(pl.program_id(1) == 0)
    def _():
        st_ref[...] = jnp.zeros_like(st_ref)

    def lanes(n):
        return slice(n * LANES, (n + 1) * LANES)

    def project_in(n):
        u = u_ref[:, :, lanes(n)].reshape(ts * batch, LANES)
        bu_ref[n] = _dot(u.astype(BF16), bw_ref[n])

    def scan(n):
        a = a_ref[n]
        a_re = jnp.broadcast_to(a[:, :half], (batch, half))
        a_im = jnp.broadcast_to(a[:, half:], (batch, half))
        s_re, s_im = st_ref[n, :, 0:half], st_ref[n, :, half:2 * half]
        for t in range(ts):
            rows = slice(t * batch, (t + 1) * batch)
            s_re, s_im = (a_re * s_re - a_im * s_im + bu_ref[n, rows, 0:half],
                          a_re * s_im + a_im * s_re + bu_ref[n, rows, half:2 * half])
            bu_ref[n, rows, 0:half] = s_re
            bu_ref[n, rows, half:2 * half] = s_im
        st_ref[n, :, 0:half] = s_re
        st_ref[n, :, half:2 * half] = s_im

    def project_out(n):
        u = u_ref[:, :, lanes(n)].reshape(ts * batch, LANES)
        y = _dot(bu_ref[n].astype(BF16), cw_ref[n]) + d_ref[:, lanes(n)] * u
        y_ref[:, :, lanes(n)] = jax.nn.gelu(y).astype(y_ref.dtype).reshape(ts, batch, LANES)

    project_in(0)
    for n in range(nblk):
        if n + 1 < nblk:
            project_in(n + 1)
        scan(n)
        if n > 0:
            project_out(n - 1)
    project_out(nblk - 1)


def s5_core(u_tm, bw, cw, a_vec, d_skip, *, batch, seq):
    width = u_tm.shape[2]
    nblk = width // LANES
    ts = min(S5_TIME_BLOCK, seq)
    per = S5_BLOCKS_PER_STEP
    return pl.pallas_call(
        functools.partial(_s5_kernel, batch=batch),
        out_shape=jax.ShapeDtypeStruct((seq, batch, width), BF16),
        grid=(nblk // per, seq // ts),
        in_specs=[pl.BlockSpec((ts, batch, per * LANES), lambda j, i: (i, 0, j)),
                  pl.BlockSpec((per, LANES, 2 * S5_BLOCK_STATE), lambda j, i: (j, 0, 0)),
                  pl.BlockSpec((per, 2 * S5_BLOCK_STATE, LANES), lambda j, i: (j, 0, 0)),
                  pl.BlockSpec((per, 1, 2 * S5_BLOCK_STATE), lambda j, i: (j, 0, 0)),
                  pl.BlockSpec((1, per * LANES), lambda j, i: (0, j))],
        out_specs=pl.BlockSpec((ts, batch, per * LANES), lambda j, i: (i, 0, j)),
        scratch_shapes=[pltpu.VMEM((per, batch, 2 * S5_BLOCK_STATE), F32),
                        pltpu.VMEM((per, ts * batch, 2 * S5_BLOCK_STATE), F32)],
        compiler_params=_params("parallel", "arbitrary"), name="s5_core",
    )(u_tm, bw, cw, a_vec, d_skip.reshape(1, width).astype(F32))


def _s5_block_weights(abar_re, abar_im, bbar_re, bbar_im, c_re, c_im):
    g_, p_ = abar_re.shape
    gb = S5_GROUPS_PER_BLOCK
    nblk = g_ // gb
    eye = jnp.eye(gb, dtype=F32)

    def in_blocks(bbar):
        b = bbar.reshape(S5_GROUP, nblk, gb, p_).transpose(1, 2, 0, 3)
        return jnp.einsum("ngcp,gh->ngchp", b, eye).reshape(nblk, gb * S5_GROUP, gb * p_)

    def out_blocks(cmat):
        c = cmat.reshape(nblk, gb, S5_GROUP, p_).transpose(0, 1, 3, 2)
        return jnp.einsum("ngpc,gh->ngphc", c, eye).reshape(nblk, gb * p_, gb * S5_GROUP)

    bw = jnp.concatenate([in_blocks(bbar_re), in_blocks(bbar_im)], axis=2).astype(BF16)
    cw = jnp.concatenate([out_blocks(c_re.astype(F32)), out_blocks(-c_im.astype(F32))], axis=1).astype(BF16)
    a_vec = jnp.concatenate([abar_re.reshape(nblk, 1, gb * p_), abar_im.reshape(nblk, 1, gb * p_)], axis=2)
    return bw, cw, a_vec


SSD_MAIN_COLS = 2 * SSD_GROUPS * SSD_GROUP_W + 2 * SSD_GROUPS * SSD_STATE
MLSTM_MAIN_COLS = 2 * MLSTM_HEADS * (MLSTM_DQK + MLSTM_DV)
TOKEN_TILE = 1024
COL_TILE = 1024
MLP_TOKEN_TILE = 1024
MLP_FF_TILE = 512


def _small_head(w_in, idx, main, transposed):
    w = jnp.pad(w_in[idx, :, main:], ((0, 0), (0, LANES - (w_in.shape[2] - main)))).astype(BF16)
    return w.T if transposed else w


def _ssd_layer(h, xin, g, w_in, w_in16, w_out16, idx, conv_w, conv_b, dt_bias, a_log, d_skip, norm_g, *, batch, seq):
    inner = SSD_GROUPS * SSD_GROUP_W
    main = SSD_MAIN_COLS
    proj, dt_t = in_proj(xin, g, w_in16, idx, n=main, w_transposed=True,
                         w_small=_small_head(w_in, idx, main, True), small="cols",
                         tm=TOKEN_TILE, tn=COL_TILE, name="ssd_in_proj")
    y16, ss = ssd_core(proj.reshape(batch, seq, main), dt_t, conv_w, conv_b, dt_bias, a_log, d_skip, norm_g,
                       batch=batch, seq=seq)
    return rms_matmul_residual(y16.reshape(batch * seq, inner), ss.reshape(batch * seq, -1), w_out16, idx, h,
                               tm=TOKEN_TILE, tn=COL_TILE // 2, name="ssd_out_proj")


def _mlstm_layer(h, xin, g, w_in, w_in16, w_out16, idx, gate_b, head_g, *, batch, seq):
    main = MLSTM_MAIN_COLS
    proj, gates = in_proj(xin, g, w_in16, idx, n=main, w_transposed=True,
                          w_small=_small_head(w_in, idx, main, False), small="rows",
                          tm=TOKEN_TILE, tn=COL_TILE, name="mlstm_in_proj")
    hs = mlstm_core(proj.reshape(batch, seq, main),
                    gates[:, :2 * MLSTM_HEADS].reshape(batch, seq, 2 * MLSTM_HEADS),
                    gate_b, head_g, batch=batch, seq=seq)
    return matmul_residual(hs.reshape(batch * seq, -1), w_out16, idx, h,
                           tm=TOKEN_TILE, tn=COL_TILE, name="mlstm_out_proj")


def _s5_layer(h, xin, g, w_in16, w_out16, idx, b_re, b_im, c_re, c_im, d_skip, log_dt, a_re, a_im, *, batch, seq):
    width = w_in16.shape[2]
    tm = min(TOKEN_TILE, seq)
    nt = seq // tm
    ncol = width // COL_TILE
    u_tm = in_proj(xin, g, w_in16, idx, n=width, tm=tm, tn=COL_TILE,
                   out_index=lambda i, j: (i % nt, (i // nt) * ncol + j),
                   out_shape=(seq, batch * width), name="s5_in_proj")
    abar_re, abar_im, bbar_re, bbar_im = s5_discretize(log_dt, a_re, a_im, b_re, b_im)
    bw, cw, a_vec = _s5_block_weights(abar_re, abar_im, bbar_re, bbar_im, c_re, c_im)
    y_tm = s5_core(u_tm.reshape(seq, batch, width), bw, cw, a_vec, d_skip, batch=batch, seq=seq)
    return glu_matmul_residual(y_tm.reshape(seq, batch * width), w_out16, idx, h,
                               batch=batch, tm=tm, tn=COL_TILE, name="s5_out_proj")


def kernel(x, norm_mix_g, norm_mlp_g, ssd_w_in, ssd_conv_w, ssd_conv_b, ssd_dt_bias, ssd_a_log, ssd_d, ssd_norm_g, ssd_w_out, mlstm_w_in, mlstm_gate_b, mlstm_head_g, mlstm_w_out, s5_w_in, s5_b_re, s5_b_im, s5_c_re, s5_c_im, s5_d, s5_log_dt, s5_a_re, s5_a_im, s5_w_out, mlp_w1, mlp_w2, final_norm_g):
    batch, seq, d = x.shape
    depth = norm_mix_g.shape[0]
    ssd_in16, ssd_out16 = jnp.swapaxes(ssd_w_in, 1, 2).astype(BF16), ssd_w_out.astype(BF16)
    mlstm_in16, mlstm_out16 = jnp.swapaxes(mlstm_w_in, 1, 2).astype(BF16), mlstm_w_out.astype(BF16)
    s5_in16, s5_out16 = s5_w_in.astype(BF16), s5_w_out.astype(BF16)
    mlp_w1_16, mlp_w2_16 = mlp_w1.astype(BF16), mlp_w2.astype(BF16)
    h = x.reshape(batch * seq, d)
    xin, g = h, norm_mix_g[0]
    for layer in range(depth):
        kind, idx = layer % 3, layer // 3
        if kind == 0:
            h = _ssd_layer(h, xin, g, ssd_w_in, ssd_in16, ssd_out16, idx, ssd_conv_w[idx], ssd_conv_b[idx],
                           ssd_dt_bias[idx], ssd_a_log[idx], ssd_d[idx], ssd_norm_g[idx], batch=batch, seq=seq)
        elif kind == 1:
            h = _mlstm_layer(h, xin, g, mlstm_w_in, mlstm_in16, mlstm_out16, idx, mlstm_gate_b[idx],
                             mlstm_head_g[idx], batch=batch, seq=seq)
        else:
            h = _s5_layer(h, xin, g, s5_in16, s5_out16, idx, s5_b_re[idx], s5_b_im[idx], s5_c_re[idx], s5_c_im[idx],
                          s5_d[idx], s5_log_dt[idx], s5_a_re[idx], s5_a_im[idx], batch=batch, seq=seq)
        final = layer == depth - 1
        out = mlp_block(h, norm_mlp_g[layer], mlp_w1_16, mlp_w2_16, layer,
                        final_norm_g if final else norm_mix_g[layer + 1], final=final,
                        tm=min(MLP_TOKEN_TILE, batch * seq), tf=MLP_FF_TILE, name=f"mlp_{layer}")
        if final:
            h = out
        else:
            h, xin = out
            g = None
    return h.reshape(batch, seq, d)
```

```python
import functools
import math

import jax
import jax.numpy as jnp
from jax import lax
from jax.experimental import pallas as pl
from jax.experimental.pallas import tpu as pltpu

F32 = jnp.float32
BF16 = jnp.bfloat16

NORM_EPS = 1e-5
LOG2E = math.log2(math.e)
LANES = 128
VMEM_LIMIT_BYTES = 56 * 1024 * 1024

SSD_HEAD_DIM = 64
SSD_GROUPS = 8
SSD_HPG = 8
SSD_STATE = 128
SSD_CONV = 4
SSD_GROUP_W = SSD_HPG * SSD_HEAD_DIM
SSD_CHUNK = 256
SSD_GROUPS_PER_STEP = 8
CONV_HALO = 8

MLSTM_HEADS = 4
MLSTM_DQK = 256
MLSTM_DV = 512
MLSTM_CHUNK = 256

S5_GROUP = 16
S5_STATE = 64
S5_GROUPS_PER_BLOCK = LANES // S5_GROUP
S5_BLOCK_STATE = S5_GROUPS_PER_BLOCK * S5_STATE
S5_TIME_BLOCK = 64
S5_BLOCKS_PER_STEP = 4


def _params(*semantics):
    return pltpu.CompilerParams(dimension_semantics=semantics, vmem_limit_bytes=VMEM_LIMIT_BYTES)


def _dot(a, b):
    return jnp.dot(a, b, preferred_element_type=F32)


def _dot_nt(a, b):
    return lax.dot_general(a, b, (((1,), (1,)), ((), ())), preferred_element_type=F32)


def _dot_tn(a, b):
    return lax.dot_general(a, b, (((0,), (0,)), ((), ())), preferred_element_type=F32)


def _split3(x):
    hi = x.astype(BF16)
    r1 = x - hi.astype(F32)
    mid = r1.astype(BF16)
    lo = (r1 - mid.astype(F32)).astype(BF16)
    return hi, mid, lo


def _tri(n, upper):
    r = lax.broadcasted_iota(jnp.int32, (n, n), 0)
    c = lax.broadcasted_iota(jnp.int32, (n, n), 1)
    keep = (r <= c) if upper else (c <= r)
    return jnp.where(keep, 1.0, 0.0).astype(BF16)


def _cumsum_rows(x):
    t = _tri(x.shape[0], upper=False)
    hi, mid, lo = _split3(x)
    return _dot(t, hi) + _dot(t, mid) + _dot(t, lo)


def _cumsum_cols(x):
    t = _tri(x.shape[1], upper=True)
    hi, mid, lo = _split3(x)
    return _dot(hi, t) + _dot(mid, t) + _dot(lo, t)


def _causal(n):
    r = lax.broadcasted_iota(jnp.int32, (n, n), 0)
    c = lax.broadcasted_iota(jnp.int32, (n, n), 1)
    return c <= r


def _softplus(x):
    return jnp.maximum(x, 0.0) + jnp.log1p(jnp.exp(-jnp.abs(x)))


def _sigmoid(x):
    return 1.0 / (1.0 + jnp.exp(-x))


def _silu(x):
    return x * _sigmoid(x)


def _log_sigmoid(x):
    return -_softplus(-x)


def _rms_scale(x, g):
    ms = jnp.mean(x * x, axis=-1, keepdims=True)
    return x * lax.rsqrt(ms + NORM_EPS) * g


def _in_proj_kernel(*refs, normalize, small):
    it = iter(refs)
    x_ref = next(it)
    g_ref = next(it) if normalize else None
    w_ref = next(it)
    ws_ref = next(it) if small else None
    o_ref = next(it)
    os_ref = next(it) if small else None
    xn_ref = next(it) if normalize else x_ref

    if normalize or small:
        @pl.when(pl.program_id(1) == 0)
        def _():
            if normalize:
                xn_ref[...] = _rms_scale(x_ref[...], g_ref[...]).astype(BF16)
            if small == "rows":
                os_ref[...] = _dot(xn_ref[...], ws_ref[...])
            elif small == "cols":
                os_ref[...] = _dot_nt(ws_ref[...], xn_ref[...])

    o_ref[...] = _dot(xn_ref[...], w_ref[...])


def in_proj(x, g, w, layer, *, n, w_small=None, small=None, tm, tn, out_index=None, out_shape=None, name):
    n_tok, k = x.shape
    assert n_tok % tm == 0 and n % tn == 0
    normalize = g is not None
    out_index = out_index or (lambda i, j: (i, j))
    in_specs = [pl.BlockSpec((tm, k), lambda i, j: (i, 0))]
    args = [x]
    if normalize:
        in_specs.append(pl.BlockSpec((1, k), lambda i, j: (0, 0)))
        args.append(g.reshape(1, k).astype(F32))
    in_specs.append(pl.BlockSpec((None, k, tn), lambda i, j: (layer, 0, j)))
    args.append(w)
    out_shapes = [jax.ShapeDtypeStruct(out_shape or (n_tok, n), F32)]
    out_specs = [pl.BlockSpec((tm, tn), out_index)]
    if small:
        in_specs.append(pl.BlockSpec(w_small.shape, lambda i, j: (0, 0)))
        args.append(w_small)
        if small == "cols":
            out_shapes.append(jax.ShapeDtypeStruct((LANES, n_tok), F32))
            out_specs.append(pl.BlockSpec((LANES, tm), lambda i, j: (0, i)))
        else:
            out_shapes.append(jax.ShapeDtypeStruct((n_tok, LANES), F32))
            out_specs.append(pl.BlockSpec((tm, LANES), lambda i, j: (i, 0)))
    outs = pl.pallas_call(
        functools.partial(_in_proj_kernel, normalize=normalize, small=small),
        out_shape=out_shapes, grid=(n_tok // tm, n // tn), in_specs=in_specs, out_specs=out_specs,
        scratch_shapes=[pltpu.VMEM((tm, k), BF16)] if normalize else [],
        compiler_params=_params("parallel", "arbitrary"), name=name)(*args)
    return outs if small else outs[0]


def _matmul_res_kernel(a_ref, w_ref, res_ref, o_ref):
    o_ref[...] = res_ref[...] + _dot(a_ref[...], w_ref[...])


def matmul_residual(a, w, layer, res, *, tm, tn, name):
    t, k = a.shape
    n = w.shape[2]
    return pl.pallas_call(
        _matmul_res_kernel, out_shape=jax.ShapeDtypeStruct((t, n), F32),
        grid=(t // tm, n // tn),
        in_specs=[pl.BlockSpec((tm, k), lambda i, j: (i, 0)),
                  pl.BlockSpec((None, k, tn), lambda i, j: (layer, 0, j)),
                  pl.BlockSpec((tm, tn), lambda i, j: (i, j))],
        out_specs=pl.BlockSpec((tm, tn), lambda i, j: (i, j)),
        compiler_params=_params("parallel", "arbitrary"), name=name)(a, w, res)


def _rms_matmul_res_kernel(a_ref, ss_ref, w_ref, res_ref, o_ref):
    k = a_ref.shape[1]
    ss = ss_ref[...]
    tot = ss[:, :LANES]
    for q in range(1, ss.shape[1] // LANES):
        tot = tot + ss[:, q * LANES:(q + 1) * LANES]
    inv = lax.rsqrt(tot * (1.0 / k) + NORM_EPS)
    y = _dot(a_ref[...], w_ref[...])
    o_ref[...] = res_ref[...] + jnp.concatenate([inv] * (y.shape[1] // LANES), axis=1) * y


def rms_matmul_residual(a, ss, w, layer, res, *, tm, tn, name):
    t, k = a.shape
    n = w.shape[2]
    return pl.pallas_call(
        _rms_matmul_res_kernel, out_shape=jax.ShapeDtypeStruct((t, n), F32),
        grid=(t // tm, n // tn),
        in_specs=[pl.BlockSpec((tm, k), lambda i, j: (i, 0)),
                  pl.BlockSpec((tm, ss.shape[1]), lambda i, j: (i, 0)),
                  pl.BlockSpec((None, k, tn), lambda i, j: (layer, 0, j)),
                  pl.BlockSpec((tm, tn), lambda i, j: (i, j))],
        out_specs=pl.BlockSpec((tm, tn), lambda i, j: (i, j)),
        compiler_params=_params("parallel", "arbitrary"), name=name)(a, ss, w, res)


def _glu_res_kernel(a_ref, wv_ref, wg_ref, res_ref, o_ref):
    a = a_ref[...]
    val = _dot(a, wv_ref[...])
    gate = _dot(a, wg_ref[...])
    o_ref[...] = res_ref[...] + val * _sigmoid(gate)


def glu_matmul_residual(a_tm, w, layer, res, *, batch, tm, tn, name):
    s = a_tm.shape[0]
    k = w.shape[1]
    n = w.shape[2] // 2
    nt = s // tm
    return pl.pallas_call(
        _glu_res_kernel, out_shape=jax.ShapeDtypeStruct((batch * s, n), F32),
        grid=(batch * nt, n // tn),
        in_specs=[pl.BlockSpec((tm, k), lambda i, j: (i % nt, i // nt)),
                  pl.BlockSpec((None, k, tn), lambda i, j: (layer, 0, j)),
                  pl.BlockSpec((None, k, tn), lambda i, j: (layer, 0, j + n // tn)),
                  pl.BlockSpec((tm, tn), lambda i, j: (i, j))],
        out_specs=pl.BlockSpec((tm, tn), lambda i, j: (i, j)),
        compiler_params=_params("parallel", "arbitrary"), name=name)(a_tm, w, w, res)


def _mlp_kernel(x_ref, g_ref, w1_ref, w2_ref, gn_ref, o_ref, *rest, final):
    xn_ref = rest[-1]
    f = pl.program_id(1)

    @pl.when(f == 0)
    def _():
        x = x_ref[...]
        xn_ref[...] = _rms_scale(x, g_ref[...]).astype(BF16)
        o_ref[...] = x

    h1 = jnp.maximum(_dot(xn_ref[...], w1_ref[...]), 0.0)
    o_ref[...] += _dot((h1 * h1).astype(BF16), w2_ref[...])

    @pl.when(f == pl.num_programs(1) - 1)
    def _():
        normed = _rms_scale(o_ref[...], gn_ref[...])
        if final:
            o_ref[...] = normed
        else:
            rest[0][...] = normed.astype(BF16)


def mlp_block(x, g, w1, w2, layer, g_next, *, final, tm, tf, name):
    t, d = x.shape
    dff = w1.shape[2]
    row_spec = pl.BlockSpec((tm, d), lambda i, f: (i, 0))
    out_shape = [jax.ShapeDtypeStruct((t, d), F32)] + ([] if final else [jax.ShapeDtypeStruct((t, d), BF16)])
    outs = pl.pallas_call(
        functools.partial(_mlp_kernel, final=final),
        out_shape=out_shape, grid=(t // tm, dff // tf),
        in_specs=[row_spec,
                  pl.BlockSpec((1, d), lambda i, f: (0, 0)),
                  pl.BlockSpec((None, d, tf), lambda i, f: (layer, 0, f)),
                  pl.BlockSpec((None, tf, d), lambda i, f: (layer, f, 0)),
                  pl.BlockSpec((1, d), lambda i, f: (0, 0))],
        out_specs=[row_spec] * len(out_shape),
        scratch_shapes=[pltpu.VMEM((tm, d), BF16)],
        compiler_params=_params("parallel", "arbitrary"), name=name,
    )(x, g.reshape(1, d), w1, w2, g_next.reshape(1, d))
    return outs[0] if final else outs


def _causal_conv_silu(buf_ref, cur, w, b):
    l = cur.shape[0]
    buf_ref[CONV_HALO:CONV_HALO + l, :] = cur
    acc = b + w[SSD_CONV - 1:SSD_CONV, :] * cur
    for k in range(SSD_CONV - 1):
        start = CONV_HALO - (SSD_CONV - 1) + k
        acc = acc + w[k:k + 1, :] * buf_ref[start:start + l, :]
    buf_ref[0:CONV_HALO, :] = cur[l - CONV_HALO:, :]
    return _silu(acc)


def _ssd_head_rows(dt_raw_t, head_p, rows_ref):
    dtt = _softplus(dt_raw_t + head_p[:, 0:1])
    rows_ref[0:SSD_HPG, :] = _cumsum_cols(dtt * (-LOG2E * jnp.exp(head_p[:, 1:2])))
    rows_ref[SSD_HPG:, :] = dtt


def _ssd_kernel(z_ref, x_ref, b_ref, c_ref, dtt_ref, dttn_ref, cp_ref, hp_ref, r_ref,
                o_ref, ss_ref, xbuf, bbuf, cbuf, state_ref, rows_ref, wide_ref, xs_ref, bm_ref, cm_ref):
    gw, ns = SSD_GROUP_W, SSD_STATE

    @pl.when(pl.program_id(2) == 0)
    def _():
        state_ref[...] = jnp.zeros_like(state_ref)
        for buf in (xbuf, bbuf, cbuf):
            buf[:, 0:CONV_HALO, :] = jnp.zeros((buf.shape[0], CONV_HALO, buf.shape[2]), F32)
        for gi in range(SSD_GROUPS_PER_STEP):
            _ssd_head_rows(dtt_ref[gi * SSD_HPG:(gi + 1) * SSD_HPG, :], hp_ref[gi], rows_ref.at[gi])

    def prepare(gi):
        x_cols, bc_cols = pl.ds(gi * gw, gw), pl.ds(gi * ns, ns)
        return _ssd_group_prepare(
            x_ref.at[0, :, x_cols], b_ref.at[0, :, bc_cols], c_ref.at[0, :, bc_cols],
            dttn_ref.at[pl.ds(gi * SSD_HPG, SSD_HPG), :], cp_ref.at[gi], hp_ref.at[gi], r_ref,
            xbuf.at[gi], bbuf.at[gi], cbuf.at[gi], rows_ref.at[gi], wide_ref.at[gi],
            xs_ref.at[gi], bm_ref.at[gi], cm_ref.at[gi])

    for gi in range(SSD_GROUPS_PER_STEP):
        acst = prepare(gi)
        x_cols = pl.ds(gi * gw, gw)
        _ssd_group_main(acst, z_ref.at[0, :, x_cols], cp_ref.at[gi], o_ref.at[0, :, x_cols],
                        ss_ref.at[0, :, pl.ds(gi * LANES, LANES)], state_ref.at[gi], wide_ref.at[gi],
                        xs_ref.at[gi], bm_ref.at[gi], cm_ref.at[gi])


def _ssd_group_prepare(x_ref, b_ref, c_ref, dttn_ref, cp_ref, hp_ref, r_ref,
                       xbuf, bbuf, cbuf, rows_ref, wide_ref, xs_ref, bm_ref, cm_ref):
    gw, ns = SSD_GROUP_W, SSD_STATE
    acst = rows_ref[0:SSD_HPG, :]
    wide_ref[...] = _dot_tn(jnp.concatenate(_split3(rows_ref[...]), axis=0), r_ref[...])

    cp = cp_ref[...]
    cw, cbias = cp[0:SSD_CONV, :], cp[SSD_CONV:SSD_CONV + 1, :]
    xs_ref[...] = _causal_conv_silu(xbuf, x_ref[...], cw[:, :gw], cbias[:, :gw])
    bm_ref[...] = _causal_conv_silu(bbuf, b_ref[...], cw[:, gw:gw + ns], cbias[:, gw:gw + ns]).astype(BF16)
    cm_ref[...] = _causal_conv_silu(cbuf, c_ref[...], cw[:, gw + ns:], cbias[:, gw + ns:]).astype(BF16)

    _ssd_head_rows(dttn_ref[...], hp_ref[...], rows_ref)
    return acst


def _ssd_group_main(acst, z_ref, cp_ref, o_ref, ss_ref, state_ref, wide_ref, xs_ref, bm_ref, cm_ref):
    l = xs_ref.shape[0]
    sub = LANES
    nsub = l // sub
    hw = SSD_HEAD_DIM
    gw = SSD_GROUP_W
    n_a = SSD_HPG * sub + gw
    cp = cp_ref[...]
    xs, bm16, cm16 = xs_ref[...], bm_ref[...], cm_ref[...]
    acs_blk = wide_ref[:, :SSD_HPG * sub]
    acs64 = wide_ref[:, SSD_HPG * sub:n_a]
    dt64 = wide_ref[:, n_a:]
    last64 = acs64[l - 1:l, :]

    xdt = xs * dt64
    xdt16 = xdt.astype(BF16)
    st = state_ref[...]
    y = _dot(cm16, st.astype(BF16)) * jnp.exp2(acs64)
    state_ref[...] = st * jnp.exp2(last64) + _dot_tn(bm16, (xdt * jnp.exp2(last64 - acs64)).astype(BF16))
    y = y + cp[SSD_CONV + 1:SSD_CONV + 2, :gw] * xs

    cb = _dot_nt(cm16, bm16)
    diag_mask = _causal(sub)
    cb_blk = [[cb[r * sub:(r + 1) * sub, c * sub:(c + 1) * sub] if c < r else
               jnp.where(diag_mask, cb[r * sub:(r + 1) * sub, c * sub:(c + 1) * sub], 0.0)
               for c in range(r + 1)] for r in range(nsub)]
    lane = lax.broadcasted_iota(jnp.int32, (l, sub), 1)
    pair_out = []
    for p in range(SSD_HPG // 2):
        xp = xdt16[:, p * sub:(p + 1) * sub]
        rhs = [jnp.where(lane < hw, xp, jnp.zeros_like(xp)), jnp.where(lane >= hw, xp, jnp.zeros_like(xp))]
        rows = []
        for r in range(nsub):
            lhs_parts, rhs_parts = [], []
            for h in range(2):
                j = 2 * p + h
                col = acs_blk[r * sub:(r + 1) * sub, j * sub:(j + 1) * sub]
                for c in range(r + 1):
                    seg = col - acst[j:j + 1, c * sub:(c + 1) * sub]
                    if c == r:
                        seg = jnp.minimum(seg, 0.0)
                    lhs_parts.append((cb_blk[r][c] * jnp.exp2(seg)).astype(BF16))
                rhs_parts.append(rhs[h][:(r + 1) * sub, :])
            rows.append(_dot(jnp.concatenate(lhs_parts, axis=1), jnp.concatenate(rhs_parts, axis=0)))
        pair_out.append(jnp.concatenate(rows, axis=0))
    y = y + jnp.concatenate(pair_out, axis=1)
    y = y * _silu(z_ref[...])
    ss_ref[...] = jnp.broadcast_to(jnp.sum(y * y, axis=1, keepdims=True), (l, sub))
    o_ref[...] = (y * cp[SSD_CONV + 2:SSD_CONV + 3, :gw]).astype(o_ref.dtype)


def _head_expand(lanes_per_head, n_heads):
    return jnp.repeat(jnp.eye(n_heads, dtype=F32), lanes_per_head, axis=1)


def ssd_core(proj, dt_t, conv_w, conv_b, dt_bias, a_log, d_skip, norm_g, *, batch, seq):
    l = min(SSD_CHUNK, seq)
    nc = seq // l
    g_, j_ = SSD_GROUPS, SSD_HPG
    inner = g_ * SSD_GROUP_W
    nx = inner // SSD_GROUP_W
    nb = 2 * inner // SSD_STATE
    expand64 = _head_expand(SSD_HEAD_DIM, j_)
    r16 = jnp.concatenate([
        jnp.concatenate([_head_expand(LANES, j_), expand64, jnp.zeros_like(expand64)], axis=1),
        jnp.concatenate([jnp.zeros((j_, j_ * LANES + SSD_GROUP_W), F32), expand64], axis=1)], axis=0)
    r = jnp.concatenate([r16, r16, r16], axis=0).astype(BF16)
    bc_w = g_ * SSD_STATE

    def per_group(p, rows):
        parts = [p[:, :inner].reshape(rows, g_, SSD_GROUP_W), p[:, inner:inner + bc_w].reshape(rows, g_, SSD_STATE),
                 p[:, inner + bc_w:].reshape(rows, g_, SSD_STATE)]
        return jnp.concatenate(parts, axis=2).transpose(1, 0, 2)

    cp_w = SSD_GROUP_W + 2 * SSD_STATE
    x_only = ((0, 0), (0, 0), (0, cp_w - SSD_GROUP_W))
    dskip = jnp.pad(jnp.repeat(d_skip.reshape(g_, 1, j_), SSD_HEAD_DIM, axis=2), x_only)
    gain = jnp.pad(norm_g.reshape(g_, 1, SSD_GROUP_W), x_only)
    conv_p = jnp.concatenate([per_group(conv_w, SSD_CONV), per_group(conv_b.reshape(1, -1), 1), dskip, gain,
                              jnp.zeros((g_, 8 - SSD_CONV - 3, cp_w), F32)], axis=1).astype(F32)
    head_p = jnp.pad(jnp.stack([dt_bias.reshape(g_, j_), a_log.reshape(g_, j_)], axis=2),
                     ((0, 0), (0, 0), (0, LANES - 2))).astype(F32)
    gps = SSD_GROUPS_PER_STEP
    xw, bw = gps * SSD_GROUP_W, gps * SSD_STATE
    in_specs = [
        pl.BlockSpec((1, l, xw), lambda b, g, c: (b, c, g)),
        pl.BlockSpec((1, l, xw), lambda b, g, c: (b, c, nx // gps + g)),
        pl.BlockSpec((1, l, bw), lambda b, g, c: (b, c, nb // gps + g)),
        pl.BlockSpec((1, l, bw), lambda b, g, c: (b, c, (nb + g_) // gps + g)),
        pl.BlockSpec((gps * j_, l), lambda b, g, c: (g, b * nc + c)),
        pl.BlockSpec((gps * j_, l), lambda b, g, c: (g, b * nc + jnp.minimum(c + 1, nc - 1))),
        pl.BlockSpec((gps, 8, cp_w), lambda b, g, c: (g, 0, 0)),
        pl.BlockSpec((gps, 8, LANES), lambda b, g, c: (g, 0, 0)),
        pl.BlockSpec(r.shape, lambda b, g, c: (0, 0)),
    ]
    return pl.pallas_call(
        _ssd_kernel,
        out_shape=[jax.ShapeDtypeStruct((batch, seq, inner), BF16),
                   jax.ShapeDtypeStruct((batch, seq, g_ * LANES), F32)],
        grid=(batch, g_ // gps, nc), in_specs=in_specs,
        out_specs=[pl.BlockSpec((1, l, xw), lambda b, g, c: (b, c, g)),
                   pl.BlockSpec((1, l, gps * LANES), lambda b, g, c: (b, c, g))],
        scratch_shapes=[pltpu.VMEM((gps, CONV_HALO + l, SSD_GROUP_W), F32),
                        pltpu.VMEM((gps, CONV_HALO + l, SSD_STATE), F32),
                        pltpu.VMEM((gps, CONV_HALO + l, SSD_STATE), F32),
                        pltpu.VMEM((gps, SSD_STATE, SSD_GROUP_W), F32),
                        pltpu.VMEM((gps, 2 * j_, l), F32),
                        pltpu.VMEM((gps, l, r.shape[1]), F32),
                        pltpu.VMEM((gps, l, SSD_GROUP_W), F32),
                        pltpu.VMEM((gps, l, SSD_STATE), BF16),
                        pltpu.VMEM((gps, l, SSD_STATE), BF16)],
        compiler_params=_params("parallel", "parallel", "arbitrary"), name="ssd_core",
    )(proj, proj, proj, proj, dt_t, dt_t, conv_p, head_p, r)


def _mlstm_kernel(q_ref, k_ref, v_ref, o_ref, gc_ref, gr_ref, gbc_ref, gbr_ref, hg_ref,
                  out_ref, c_ref, n_ref, m_ref):
    @pl.when(pl.program_id(1) == 0)
    def _():
        c_ref[...] = jnp.zeros_like(c_ref)
        n_ref[...] = jnp.zeros_like(n_ref)
        m_ref[...] = jnp.zeros_like(m_ref)

    for hi in range(MLSTM_HEADS):
        qk_cols, v_cols = pl.ds(hi * MLSTM_DQK, MLSTM_DQK), pl.ds(hi * MLSTM_DV, MLSTM_DV)
        _mlstm_head(q_ref.at[0, :, qk_cols], k_ref.at[0, :, qk_cols], v_ref.at[0, :, v_cols], o_ref.at[0, :, v_cols],
                    gc_ref.at[0, hi], gr_ref.at[0, hi], gbc_ref.at[hi], gbr_ref.at[hi], hg_ref.at[:, v_cols],
                    out_ref.at[0, :, v_cols], c_ref.at[hi], n_ref.at[hi], m_ref.at[hi])


def _mlstm_head(q_ref, k_ref, v_ref, o_ref, gc_ref, gr_ref, gbc_ref, gbr_ref, hg_ref,
                out_ref, c_ref, n_ref, m_ref):
    l = q_ref.shape[0]
    q = q_ref[...] * (MLSTM_DQK ** -0.5)
    k = k_ref[...]
    v16 = v_ref[...].astype(BF16)
    q16 = q.astype(BF16)
    gcol = gc_ref[...] + gbc_ref[...]
    grow = gr_ref[...] + gbr_ref[...]
    i_col, lf_col = gcol[:, 0:1], _log_sigmoid(gcol[:, 1:2])
    i_row, lf_row = grow[0:1, :], _log_sigmoid(grow[1:2, :])
    bcum = _cumsum_rows(lf_col)
    bcum_row = _cumsum_cols(lf_row)
    m_st = m_ref[0:1, 0:1]

    causal = _causal(l)
    intra = jnp.where(causal, bcum - bcum_row + i_row, -jnp.inf)
    inter = bcum + m_st
    m_t = jnp.maximum(inter, jnp.max(intra, axis=1, keepdims=True))
    w = jnp.exp(intra - m_t)
    scale_inter = jnp.exp(inter - m_t)
    qk = _dot_nt(q16, k.astype(BF16)) * w
    num = _dot(qk.astype(BF16), v16) + scale_inter * _dot(q16, c_ref[...].astype(BF16))
    qn = jnp.sum(q * n_ref[...], axis=1, keepdims=True)
    den = jnp.sum(qk, axis=1, keepdims=True) + scale_inter * qn
    h = num / jnp.maximum(jnp.abs(den), jnp.exp(-m_t))

    b_last = bcum[l - 1:l, :]
    tail = b_last - bcum + i_col
    m_new = jnp.maximum(b_last + m_st, jnp.max(tail, axis=0, keepdims=True))
    carry_scale = jnp.exp(b_last + m_st - m_new)
    wk = k * jnp.exp(tail - m_new)
    c_ref[...] = carry_scale * c_ref[...] + _dot_tn(wk.astype(BF16), v16)
    n_ref[...] = carry_scale * n_ref[...] + jnp.sum(wk, axis=0, keepdims=True)
    m_ref[...] = jnp.broadcast_to(m_new, m_ref.shape)

    hn = _rms_scale(h, hg_ref[...])
    out_ref[...] = (_sigmoid(o_ref[...]) * hn).astype(out_ref.dtype)


def mlstm_core(proj, gates, gate_b, head_g, *, batch, seq):
    l = MLSTM_CHUNK
    h_ = MLSTM_HEADS
    qk_w = h_ * MLSTM_DQK
    v_w = h_ * MLSTM_DV
    gsplit = jnp.stack([gates[..., :h_], gates[..., h_:2 * h_]], axis=-1)
    gcol = gsplit.transpose(0, 2, 1, 3)
    grow = gsplit.transpose(0, 2, 3, 1)
    gb = jnp.stack([gate_b[:h_], gate_b[h_:]], axis=-1).astype(F32)
    return pl.pallas_call(
        _mlstm_kernel, out_shape=jax.ShapeDtypeStruct((batch, seq, v_w), BF16),
        grid=(batch, seq // l),
        in_specs=[pl.BlockSpec((1, l, qk_w), lambda b, c: (b, c, 0)),
                  pl.BlockSpec((1, l, qk_w), lambda b, c: (b, c, 1)),
                  pl.BlockSpec((1, l, v_w), lambda b, c: (b, c, 2 * qk_w // v_w)),
                  pl.BlockSpec((1, l, v_w), lambda b, c: (b, c, 2 * qk_w // v_w + 1)),
                  pl.BlockSpec((1, h_, l, 2), lambda b, c: (b, 0, c, 0)),
                  pl.BlockSpec((1, h_, 2, l), lambda b, c: (b, 0, 0, c)),
                  pl.BlockSpec((h_, 1, 2), lambda b, c: (0, 0, 0)),
                  pl.BlockSpec((h_, 2, 1), lambda b, c: (0, 0, 0)),
                  pl.BlockSpec((1, v_w), lambda b, c: (0, 0))],
        out_specs=pl.BlockSpec((1, l, v_w), lambda b, c: (b, c, 0)),
        scratch_shapes=[pltpu.VMEM((h_, MLSTM_DQK, MLSTM_DV), F32),
                        pltpu.VMEM((h_, 1, MLSTM_DQK), F32),
                        pltpu.VMEM((h_, 8, LANES), F32)],
        compiler_params=_params("parallel", "arbitrary"), name="mlstm_core",
    )(proj, proj, proj, proj, gcol, grow, gb.reshape(h_, 1, 2), gb.reshape(h_, 2, 1),
      head_g.reshape(1, v_w).astype(F32))


def _s5_discretize_kernel(logdt_ref, are_ref, aim_ref, bre_ref, bim_ref,
                          abre_ref, abim_ref, bbre_ref, bbim_ref):
    ar, ai = are_ref[...], aim_ref[...]
    dt = jnp.exp(logdt_ref[...])
    mag = jnp.exp(ar * dt)
    abar_re, abar_im = mag * jnp.cos(ai * dt), mag * jnp.sin(ai * dt)
    den = ar * ar + ai * ai
    zoh_re = ((abar_re - 1.0) * ar + abar_im * ai) / den
    zoh_im = (abar_im * ar - (abar_re - 1.0) * ai) / den
    abre_ref[...] = abar_re
    abim_ref[...] = abar_im
    for c in range(S5_GROUP):
        br, bi = bre_ref[c], bim_ref[c]
        bbre_ref[c] = zoh_re * br - zoh_im * bi
        bbim_ref[c] = zoh_re * bi + zoh_im * br


def s5_discretize(log_dt, a_re, a_im, b_re, b_im):
    g_, p_ = a_re.shape
    sds = jax.ShapeDtypeStruct
    return pl.pallas_call(
        _s5_discretize_kernel,
        out_shape=[sds((g_, p_), F32), sds((g_, p_), F32),
                   sds((S5_GROUP, g_, p_), F32), sds((S5_GROUP, g_, p_), F32)],
        name="s5_discretize",
    )(log_dt.reshape(g_, 1).astype(F32), a_re.astype(F32), a_im.astype(F32),
      b_re.astype(F32).transpose(2, 0, 1), b_im.astype(F32).transpose(2, 0, 1))


def _s5_kernel(u_ref, bw_ref, cw_ref, a_ref, d_ref, y_ref, st_ref, bu_ref, *, batch):
    ts = u_ref.shape[0]
    half = S5_BLOCK_STATE
    nblk = S5_BLOCKS_PER_STEP

    @pl.when(pl.program_id(1) == 0)
    def _():
        st_ref[...] = jnp.zeros_like(st_ref)

    def lanes(n):
        return slice(n * LANES, (n + 1) * LANES)

    def project_in(n):
        u = u_ref[:, :, lanes(n)].reshape(ts * batch, LANES)
        bu_ref[n] = _dot(u.astype(BF16), bw_ref[n])

    def scan(n):
        a = a_ref[n]
        a_re = jnp.broadcast_to(a[:, :half], (batch, half))
        a_im = jnp.broadcast_to(a[:, half:], (batch, half))
        s_re, s_im = st_ref[n, :, 0:half], st_ref[n, :, half:2 * half]
        for t in range(ts):
            rows = slice(t * batch, (t + 1) * batch)
            s_re, s_im = (a_re * s_re - a_im * s_im + bu_ref[n, rows, 0:half],
                          a_re * s_im + a_im * s_re + bu_ref[n, rows, half:2 * half])
            bu_ref[n, rows, 0:half] = s_re
            bu_ref[n, rows, half:2 * half] = s_im
        st_ref[n, :, 0:half] = s_re
        st_ref[n, :, half:2 * half] = s_im

    def project_out(n):
        u = u_ref[:, :, lanes(n)].reshape(ts * batch, LANES)
        y = _dot(bu_ref[n].astype(BF16), cw_ref[n]) + d_ref[:, lanes(n)] * u
        y_ref[:, :, lanes(n)] = jax.nn.gelu(y).astype(y_ref.dtype).reshape(ts, batch, LANES)

    project_in(0)
    for n in range(nblk):
        if n + 1 < nblk:
            project_in(n + 1)
        scan(n)
        if n > 0:
            project_out(n - 1)
    project_out(nblk - 1)


def s5_core(u_tm, bw, cw, a_vec, d_skip, *, batch, seq):
    width = u_tm.shape[2]
    nblk = width // LANES
    ts = min(S5_TIME_BLOCK, seq)
    per = S5_BLOCKS_PER_STEP
    return pl.pallas_call(
        functools.partial(_s5_kernel, batch=batch),
        out_shape=jax.ShapeDtypeStruct((seq, batch, width), BF16),
        grid=(nblk // per, seq // ts),
        in_specs=[pl.BlockSpec((ts, batch, per * LANES), lambda j, i: (i, 0, j)),
                  pl.BlockSpec((per, LANES, 2 * S5_BLOCK_STATE), lambda j, i: (j, 0, 0)),
                  pl.BlockSpec((per, 2 * S5_BLOCK_STATE, LANES), lambda j, i: (j, 0, 0)),
                  pl.BlockSpec((per, 1, 2 * S5_BLOCK_STATE), lambda j, i: (j, 0, 0)),
                  pl.BlockSpec((1, per * LANES), lambda j, i: (0, j))],
        out_specs=pl.BlockSpec((ts, batch, per * LANES), lambda j, i: (i, 0, j)),
        scratch_shapes=[pltpu.VMEM((per, batch, 2 * S5_BLOCK_STATE), F32),
                        pltpu.VMEM((per, ts * batch, 2 * S5_BLOCK_STATE), F32)],
        compiler_params=_params("parallel", "arbitrary"), name="s5_core",
    )(u_tm, bw, cw, a_vec, d_skip.reshape(1, width).astype(F32))


def _s5_block_weights(abar_re, abar_im, bbar_re, bbar_im, c_re, c_im):
    g_, p_ = abar_re.shape
    gb = S5_GROUPS_PER_BLOCK
    nblk = g_ // gb
    eye = jnp.eye(gb, dtype=F32)

    def in_blocks(bbar):
        b = bbar.reshape(S5_GROUP, nblk, gb, p_).transpose(1, 2, 0, 3)
        return jnp.einsum("ngcp,gh->ngchp", b, eye).reshape(nblk, gb * S5_GROUP, gb * p_)

    def out_blocks(cmat):
        c = cmat.reshape(nblk, gb, S5_GROUP, p_).transpose(0, 1, 3, 2)
        return jnp.einsum("ngpc,gh->ngphc", c, eye).reshape(nblk, gb * p_, gb * S5_GROUP)

    bw = jnp.concatenate([in_blocks(bbar_re), in_blocks(bbar_im)], axis=2).astype(BF16)
    cw = jnp.concatenate([out_blocks(c_re.astype(F32)), out_blocks(-c_im.astype(F32))], axis=1).astype(BF16)
    a_vec = jnp.concatenate([abar_re.reshape(nblk, 1, gb * p_), abar_im.reshape(nblk, 1, gb * p_)], axis=2)
    return bw, cw, a_vec


SSD_MAIN_COLS = 2 * SSD_GROUPS * SSD_GROUP_W + 2 * SSD_GROUPS * SSD_STATE
MLSTM_MAIN_COLS = 2 * MLSTM_HEADS * (MLSTM_DQK + MLSTM_DV)
TOKEN_TILE = 1024
COL_TILE = 1024
MLP_TOKEN_TILE = 1024
MLP_FF_TILE = 512


def _small_head(w_in, idx, main, transposed):
    w = jnp.pad(w_in[idx, :, main:], ((0, 0), (0, LANES - (w_in.shape[2] - main)))).astype(BF16)
    return w.T if transposed else w


def _ssd_layer(h, xin, g, w_in, w_in16, w_out16, idx, conv_w, conv_b, dt_bias, a_log, d_skip, norm_g, *, batch, seq):
    inner = SSD_GROUPS * SSD_GROUP_W
    main = SSD_MAIN_COLS
    proj, dt_t = in_proj(xin, g, w_in16, idx, n=main, w_small=_small_head(w_in, idx, main, True), small="cols",
                         tm=TOKEN_TILE, tn=COL_TILE, name="ssd_in_proj")
    y16, ss = ssd_core(proj.reshape(batch, seq, main), dt_t, conv_w, conv_b, dt_bias, a_log, d_skip, norm_g,
                       batch=batch, seq=seq)
    return rms_matmul_residual(y16.reshape(batch * seq, inner), ss.reshape(batch * seq, -1), w_out16, idx, h,
                               tm=TOKEN_TILE, tn=COL_TILE // 2, name="ssd_out_proj")


def _mlstm_layer(h, xin, g, w_in, w_in16, w_out16, idx, gate_b, head_g, *, batch, seq):
    main = MLSTM_MAIN_COLS
    proj, gates = in_proj(xin, g, w_in16, idx, n=main, w_small=_small_head(w_in, idx, main, False), small="rows",
                          tm=TOKEN_TILE, tn=COL_TILE, name="mlstm_in_proj")
    hs = mlstm_core(proj.reshape(batch, seq, main),
                    gates[:, :2 * MLSTM_HEADS].reshape(batch, seq, 2 * MLSTM_HEADS),
                    gate_b, head_g, batch=batch, seq=seq)
    return matmul_residual(hs.reshape(batch * seq, -1), w_out16, idx, h,
                           tm=TOKEN_TILE // 2, tn=w_out16.shape[2], name="mlstm_out_proj")


def _s5_layer(h, xin, g, w_in16, w_out16, idx, b_re, b_im, c_re, c_im, d_skip, log_dt, a_re, a_im, *, batch, seq):
    width = w_in16.shape[2]
    tm = min(TOKEN_TILE, seq)
    nt = seq // tm
    ncol = width // COL_TILE
    u_tm = in_proj(xin, g, w_in16, idx, n=width, tm=tm, tn=COL_TILE,
                   out_index=lambda i, j: (i % nt, (i // nt) * ncol + j),
                   out_shape=(seq, batch * width), name="s5_in_proj")
    abar_re, abar_im, bbar_re, bbar_im = s5_discretize(log_dt, a_re, a_im, b_re, b_im)
    bw, cw, a_vec = _s5_block_weights(abar_re, abar_im, bbar_re, bbar_im, c_re, c_im)
    y_tm = s5_core(u_tm.reshape(seq, batch, width), bw, cw, a_vec, d_skip, batch=batch, seq=seq)
    return glu_matmul_residual(y_tm.reshape(seq, batch * width), w_out16, idx, h,
                               batch=batch, tm=tm, tn=COL_TILE, name="s5_out_proj")


def kernel(x, norm_mix_g, norm_mlp_g, ssd_w_in, ssd_conv_w, ssd_conv_b, ssd_dt_bias, ssd_a_log, ssd_d, ssd_norm_g, ssd_w_out, mlstm_w_in, mlstm_gate_b, mlstm_head_g, mlstm_w_out, s5_w_in, s5_b_re, s5_b_im, s5_c_re, s5_c_im, s5_d, s5_log_dt, s5_a_re, s5_a_im, s5_w_out, mlp_w1, mlp_w2, final_norm_g):
    batch, seq, d = x.shape
    depth = norm_mix_g.shape[0]
    ssd_in16, ssd_out16 = ssd_w_in.astype(BF16), ssd_w_out.astype(BF16)
    mlstm_in16, mlstm_out16 = mlstm_w_in.astype(BF16), mlstm_w_out.astype(BF16)
    s5_in16, s5_out16 = s5_w_in.astype(BF16), s5_w_out.astype(BF16)
    mlp_w1_16, mlp_w2_16 = mlp_w1.astype(BF16), mlp_w2.astype(BF16)
    h = x.reshape(batch * seq, d)
    xin, g = h, norm_mix_g[0]
    for layer in range(depth):
        kind, idx = layer % 3, layer // 3
        if kind == 0:
            h = _ssd_layer(h, xin, g, ssd_w_in, ssd_in16, ssd_out16, idx, ssd_conv_w[idx], ssd_conv_b[idx],
                           ssd_dt_bias[idx], ssd_a_log[idx], ssd_d[idx], ssd_norm_g[idx], batch=batch, seq=seq)
        elif kind == 1:
            h = _mlstm_layer(h, xin, g, mlstm_w_in, mlstm_in16, mlstm_out16, idx, mlstm_gate_b[idx],
                             mlstm_head_g[idx], batch=batch, seq=seq)
        else:
            h = _s5_layer(h, xin, g, s5_in16, s5_out16, idx, s5_b_re[idx], s5_b_im[idx], s5_c_re[idx], s5_c_im[idx],
                          s5_d[idx], s5_log_dt[idx], s5_a_re[idx], s5_a_im[idx], batch=batch, seq=seq)
        final = layer == depth - 1
        out = mlp_block(h, norm_mlp_g[layer], mlp_w1_16, mlp_w2_16, layer,
                        final_norm_g if final else norm_mix_g[layer + 1], final=final,
                        tm=min(MLP_TOKEN_TILE, batch * seq), tf=MLP_FF_TILE, name=f"mlp_{layer}")
        if final:
            h = out
        else:
            h, xin = out
            g = None
    return h.reshape(batch, seq, d)
```

```python
import functools
import math

import jax
import jax.numpy as jnp
from jax import lax
from jax.experimental import pallas as pl
from jax.experimental.pallas import tpu as pltpu

F32 = jnp.float32
BF16 = jnp.bfloat16

NORM_EPS = 1e-5
LOG2E = math.log2(math.e)
LANES = 128
VMEM_LIMIT_BYTES = 56 * 1024 * 1024

SSD_HEAD_DIM = 64
SSD_GROUPS = 8
SSD_HPG = 8
SSD_STATE = 128
SSD_CONV = 4
SSD_GROUP_W = SSD_HPG * SSD_HEAD_DIM
SSD_CHUNK = 256
SSD_GROUPS_PER_STEP = 8
CONV_HALO = 8

MLSTM_HEADS = 4
MLSTM_DQK = 256
MLSTM_DV = 512
MLSTM_CHUNK = 256

S5_GROUP = 16
S5_STATE = 64
S5_GROUPS_PER_BLOCK = LANES // S5_GROUP
S5_BLOCK_STATE = S5_GROUPS_PER_BLOCK * S5_STATE
S5_TIME_BLOCK = 64
S5_BLOCKS_PER_STEP = 4


def _params(*semantics):
    return pltpu.CompilerParams(dimension_semantics=semantics, vmem_limit_bytes=VMEM_LIMIT_BYTES)


def _dot(a, b):
    return jnp.dot(a, b, preferred_element_type=F32)


def _dot_nt(a, b):
    return lax.dot_general(a, b, (((1,), (1,)), ((), ())), preferred_element_type=F32)


def _dot_tn(a, b):
    return lax.dot_general(a, b, (((0,), (0,)), ((), ())), preferred_element_type=F32)


def _split3(x):
    hi = x.astype(BF16)
    r1 = x - hi.astype(F32)
    mid = r1.astype(BF16)
    lo = (r1 - mid.astype(F32)).astype(BF16)
    return hi, mid, lo


def _tri(n, upper):
    r = lax.broadcasted_iota(jnp.int32, (n, n), 0)
    c = lax.broadcasted_iota(jnp.int32, (n, n), 1)
    keep = (r <= c) if upper else (c <= r)
    return jnp.where(keep, 1.0, 0.0).astype(BF16)


def _cumsum_rows(x):
    t = _tri(x.shape[0], upper=False)
    hi, mid, lo = _split3(x)
    return _dot(t, hi) + _dot(t, mid) + _dot(t, lo)


def _cumsum_cols(x):
    t = _tri(x.shape[1], upper=True)
    hi, mid, lo = _split3(x)
    return _dot(hi, t) + _dot(mid, t) + _dot(lo, t)


def _causal(n):
    r = lax.broadcasted_iota(jnp.int32, (n, n), 0)
    c = lax.broadcasted_iota(jnp.int32, (n, n), 1)
    return c <= r


def _softplus(x):
    return jnp.maximum(x, 0.0) + jnp.log1p(jnp.exp(-jnp.abs(x)))


def _sigmoid(x):
    return 1.0 / (1.0 + jnp.exp(-x))


def _silu(x):
    return x * _sigmoid(x)


def _log_sigmoid(x):
    return -_softplus(-x)


def _rms_scale(x, g):
    ms = jnp.mean(x * x, axis=-1, keepdims=True)
    return x * lax.rsqrt(ms + NORM_EPS) * g


def _in_proj_kernel(*refs, normalize, small):
    it = iter(refs)
    x_ref = next(it)
    g_ref = next(it) if normalize else None
    w_ref = next(it)
    ws_ref = next(it) if small else None
    o_ref = next(it)
    os_ref = next(it) if small else None
    xn_ref = next(it) if normalize else x_ref

    if normalize or small:
        @pl.when(pl.program_id(1) == 0)
        def _():
            if normalize:
                xn_ref[...] = _rms_scale(x_ref[...], g_ref[...]).astype(BF16)
            if small == "rows":
                os_ref[...] = _dot(xn_ref[...], ws_ref[...])
            elif small == "cols":
                os_ref[...] = _dot_nt(ws_ref[...], xn_ref[...])

    o_ref[...] = _dot(xn_ref[...], w_ref[...])


def in_proj(x, g, w, layer, *, n, w_small=None, small=None, tm, tn, out_index=None, out_shape=None, name):
    n_tok, k = x.shape
    assert n_tok % tm == 0 and n % tn == 0
    normalize = g is not None
    out_index = out_index or (lambda i, j: (i, j))
    in_specs = [pl.BlockSpec((tm, k), lambda i, j: (i, 0))]
    args = [x]
    if normalize:
        in_specs.append(pl.BlockSpec((1, k), lambda i, j: (0, 0)))
        args.append(g.reshape(1, k).astype(F32))
    in_specs.append(pl.BlockSpec((None, k, tn), lambda i, j: (layer, 0, j)))
    args.append(w)
    out_shapes = [jax.ShapeDtypeStruct(out_shape or (n_tok, n), F32)]
    out_specs = [pl.BlockSpec((tm, tn), out_index)]
    if small:
        in_specs.append(pl.BlockSpec(w_small.shape, lambda i, j: (0, 0)))
        args.append(w_small)
        if small == "cols":
            out_shapes.append(jax.ShapeDtypeStruct((LANES, n_tok), F32))
            out_specs.append(pl.BlockSpec((LANES, tm), lambda i, j: (0, i)))
        else:
            out_shapes.append(jax.ShapeDtypeStruct((n_tok, LANES), F32))
            out_specs.append(pl.BlockSpec((tm, LANES), lambda i, j: (i, 0)))
    outs = pl.pallas_call(
        functools.partial(_in_proj_kernel, normalize=normalize, small=small),
        out_shape=out_shapes, grid=(n_tok // tm, n // tn), in_specs=in_specs, out_specs=out_specs,
        scratch_shapes=[pltpu.VMEM((tm, k), BF16)] if normalize else [],
        compiler_params=_params("parallel", "arbitrary"), name=name)(*args)
    return outs if small else outs[0]


def _matmul_res_kernel(a_ref, w_ref, res_ref, o_ref):
    o_ref[...] = res_ref[...] + _dot(a_ref[...], w_ref[...])


def matmul_residual(a, w, layer, res, *, tm, tn, name):
    t, k = a.shape
    n = w.shape[2]
    return pl.pallas_call(
        _matmul_res_kernel, out_shape=jax.ShapeDtypeStruct((t, n), F32),
        grid=(t // tm, n // tn),
        in_specs=[pl.BlockSpec((tm, k), lambda i, j: (i, 0)),
                  pl.BlockSpec((None, k, tn), lambda i, j: (layer, 0, j)),
                  pl.BlockSpec((tm, tn), lambda i, j: (i, j))],
        out_specs=pl.BlockSpec((tm, tn), lambda i, j: (i, j)),
        compiler_params=_params("parallel", "arbitrary"), name=name)(a, w, res)


def _rms_matmul_res_kernel(a_ref, ss_ref, w_ref, res_ref, o_ref):
    k = a_ref.shape[1]
    ss = ss_ref[...]
    tot = ss[:, :LANES]
    for q in range(1, ss.shape[1] // LANES):
        tot = tot + ss[:, q * LANES:(q + 1) * LANES]
    inv = lax.rsqrt(tot * (1.0 / k) + NORM_EPS)
    y = _dot(a_ref[...], w_ref[...])
    o_ref[...] = res_ref[...] + jnp.concatenate([inv] * (y.shape[1] // LANES), axis=1) * y


def rms_matmul_residual(a, ss, w, layer, res, *, tm, tn, name):
    t, k = a.shape
    n = w.shape[2]
    return pl.pallas_call(
        _rms_matmul_res_kernel, out_shape=jax.ShapeDtypeStruct((t, n), F32),
        grid=(t // tm, n // tn),
        in_specs=[pl.BlockSpec((tm, k), lambda i, j: (i, 0)),
                  pl.BlockSpec((tm, ss.shape[1]), lambda i, j: (i, 0)),
                  pl.BlockSpec((None, k, tn), lambda i, j: (layer, 0, j)),
                  pl.BlockSpec((tm, tn), lambda i, j: (i, j))],
        out_specs=pl.BlockSpec((tm, tn), lambda i, j: (i, j)),
        compiler_params=_params("parallel", "arbitrary"), name=name)(a, ss, w, res)


def _glu_res_kernel(a_ref, wv_ref, wg_ref, res_ref, o_ref):
    a = a_ref[...]
    val = _dot(a, wv_ref[...])
    gate = _dot(a, wg_ref[...])
    o_ref[...] = res_ref[...] + val * _sigmoid(gate)


def glu_matmul_residual(a_tm, w, layer, res, *, batch, tm, tn, name):
    s = a_tm.shape[0]
    k = w.shape[1]
    n = w.shape[2] // 2
    nt = s // tm
    return pl.pallas_call(
        _glu_res_kernel, out_shape=jax.ShapeDtypeStruct((batch * s, n), F32),
        grid=(batch * nt, n // tn),
        in_specs=[pl.BlockSpec((tm, k), lambda i, j: (i % nt, i // nt)),
                  pl.BlockSpec((None, k, tn), lambda i, j: (layer, 0, j)),
                  pl.BlockSpec((None, k, tn), lambda i, j: (layer, 0, j + n // tn)),
                  pl.BlockSpec((tm, tn), lambda i, j: (i, j))],
        out_specs=pl.BlockSpec((tm, tn), lambda i, j: (i, j)),
        compiler_params=_params("parallel", "arbitrary"), name=name)(a_tm, w, w, res)


def _mlp_kernel(x_ref, g_ref, w1_ref, w2_ref, gn_ref, o_ref, *rest, final):
    xn_ref = rest[-1]
    f = pl.program_id(1)

    @pl.when(f == 0)
    def _():
        x = x_ref[...]
        xn_ref[...] = _rms_scale(x, g_ref[...]).astype(BF16)
        o_ref[...] = x

    h1 = jnp.maximum(_dot(xn_ref[...], w1_ref[...]), 0.0)
    o_ref[...] += _dot((h1 * h1).astype(BF16), w2_ref[...])

    @pl.when(f == pl.num_programs(1) - 1)
    def _():
        normed = _rms_scale(o_ref[...], gn_ref[...])
        if final:
            o_ref[...] = normed
        else:
            rest[0][...] = normed.astype(BF16)


def mlp_block(x, g, w1, w2, layer, g_next, *, final, tm, tf, name):
    t, d = x.shape
    dff = w1.shape[2]
    row_spec = pl.BlockSpec((tm, d), lambda i, f: (i, 0))
    out_shape = [jax.ShapeDtypeStruct((t, d), F32)] + ([] if final else [jax.ShapeDtypeStruct((t, d), BF16)])
    outs = pl.pallas_call(
        functools.partial(_mlp_kernel, final=final),
        out_shape=out_shape, grid=(t // tm, dff // tf),
        in_specs=[row_spec,
                  pl.BlockSpec((1, d), lambda i, f: (0, 0)),
                  pl.BlockSpec((None, d, tf), lambda i, f: (layer, 0, f)),
                  pl.BlockSpec((None, tf, d), lambda i, f: (layer, f, 0)),
                  pl.BlockSpec((1, d), lambda i, f: (0, 0))],
        out_specs=[row_spec] * len(out_shape),
        scratch_shapes=[pltpu.VMEM((tm, d), BF16)],
        compiler_params=_params("parallel", "arbitrary"), name=name,
    )(x, g.reshape(1, d), w1, w2, g_next.reshape(1, d))
    return outs[0] if final else outs


def _causal_conv_silu(buf_ref, cur, w, b):
    l = cur.shape[0]
    buf_ref[CONV_HALO:CONV_HALO + l, :] = cur
    acc = b + w[SSD_CONV - 1:SSD_CONV, :] * cur
    for k in range(SSD_CONV - 1):
        start = CONV_HALO - (SSD_CONV - 1) + k
        acc = acc + w[k:k + 1, :] * buf_ref[start:start + l, :]
    buf_ref[0:CONV_HALO, :] = cur[l - CONV_HALO:, :]
    return _silu(acc)


def _ssd_head_rows(dt_raw_t, head_p, rows_ref):
    dtt = _softplus(dt_raw_t + head_p[:, 0:1])
    rows_ref[0:SSD_HPG, :] = _cumsum_cols(dtt * (-LOG2E * jnp.exp(head_p[:, 1:2])))
    rows_ref[SSD_HPG:, :] = dtt


def _ssd_kernel(z_ref, x_ref, b_ref, c_ref, dtt_ref, dttn_ref, cp_ref, hp_ref, r_ref,
                o_ref, ss_ref, xbuf, bbuf, cbuf, state_ref, rows_ref, wide_ref, xs_ref, bm_ref, cm_ref):
    gw, ns = SSD_GROUP_W, SSD_STATE

    @pl.when(pl.program_id(2) == 0)
    def _():
        state_ref[...] = jnp.zeros_like(state_ref)
        for buf in (xbuf, bbuf, cbuf):
            buf[:, 0:CONV_HALO, :] = jnp.zeros((buf.shape[0], CONV_HALO, buf.shape[2]), F32)
        for gi in range(SSD_GROUPS_PER_STEP):
            _ssd_head_rows(dtt_ref[gi * SSD_HPG:(gi + 1) * SSD_HPG, :], hp_ref[gi], rows_ref.at[gi])

    def prepare(gi):
        x_cols, bc_cols = pl.ds(gi * gw, gw), pl.ds(gi * ns, ns)
        return _ssd_group_prepare(
            x_ref.at[0, :, x_cols], b_ref.at[0, :, bc_cols], c_ref.at[0, :, bc_cols],
            dttn_ref.at[pl.ds(gi * SSD_HPG, SSD_HPG), :], cp_ref.at[gi], hp_ref.at[gi], r_ref,
            xbuf.at[gi], bbuf.at[gi], cbuf.at[gi], rows_ref.at[gi], wide_ref.at[gi],
            xs_ref.at[gi], bm_ref.at[gi], cm_ref.at[gi])

    for gi in range(SSD_GROUPS_PER_STEP):
        acst = prepare(gi)
        x_cols = pl.ds(gi * gw, gw)
        _ssd_group_main(acst, z_ref.at[0, :, x_cols], cp_ref.at[gi], o_ref.at[0, :, x_cols],
                        ss_ref.at[0, :, pl.ds(gi * LANES, LANES)], state_ref.at[gi], wide_ref.at[gi],
                        xs_ref.at[gi], bm_ref.at[gi], cm_ref.at[gi])


def _ssd_group_prepare(x_ref, b_ref, c_ref, dttn_ref, cp_ref, hp_ref, r_ref,
                       xbuf, bbuf, cbuf, rows_ref, wide_ref, xs_ref, bm_ref, cm_ref):
    gw, ns = SSD_GROUP_W, SSD_STATE
    acst = rows_ref[0:SSD_HPG, :]
    wide_ref[...] = _dot_tn(jnp.concatenate(_split3(rows_ref[...]), axis=0), r_ref[...])

    cp = cp_ref[...]
    cw, cbias = cp[0:SSD_CONV, :], cp[SSD_CONV:SSD_CONV + 1, :]
    xs_ref[...] = _causal_conv_silu(xbuf, x_ref[...], cw[:, :gw], cbias[:, :gw])
    bm_ref[...] = _causal_conv_silu(bbuf, b_ref[...], cw[:, gw:gw + ns], cbias[:, gw:gw + ns]).astype(BF16)
    cm_ref[...] = _causal_conv_silu(cbuf, c_ref[...], cw[:, gw + ns:], cbias[:, gw + ns:]).astype(BF16)

    _ssd_head_rows(dttn_ref[...], hp_ref[...], rows_ref)
    return acst


def _ssd_group_main(acst, z_ref, cp_ref, o_ref, ss_ref, state_ref, wide_ref, xs_ref, bm_ref, cm_ref):
    l = xs_ref.shape[0]
    sub = LANES
    nsub = l // sub
    hw = SSD_HEAD_DIM
    gw = SSD_GROUP_W
    n_a = SSD_HPG * sub + gw
    cp = cp_ref[...]
    xs, bm16, cm16 = xs_ref[...], bm_ref[...], cm_ref[...]
    acs_blk = wide_ref[:, :SSD_HPG * sub]
    acs64 = wide_ref[:, SSD_HPG * sub:n_a]
    dt64 = wide_ref[:, n_a:]
    last64 = acs64[l - 1:l, :]

    xdt = xs * dt64
    xdt16 = xdt.astype(BF16)
    st = state_ref[...]
    y = _dot(cm16, st.astype(BF16)) * jnp.exp2(acs64)
    state_ref[...] = st * jnp.exp2(last64) + _dot_tn(bm16, (xdt * jnp.exp2(last64 - acs64)).astype(BF16))
    y = y + cp[SSD_CONV + 1:SSD_CONV + 2, :gw] * xs

    cb = _dot_nt(cm16, bm16)
    diag_mask = _causal(sub)
    cb_blk = [[cb[r * sub:(r + 1) * sub, c * sub:(c + 1) * sub] if c < r else
               jnp.where(diag_mask, cb[r * sub:(r + 1) * sub, c * sub:(c + 1) * sub], 0.0)
               for c in range(r + 1)] for r in range(nsub)]
    lane = lax.broadcasted_iota(jnp.int32, (l, sub), 1)
    pair_out = []
    for p in range(SSD_HPG // 2):
        xp = xdt16[:, p * sub:(p + 1) * sub]
        rhs = [jnp.where(lane < hw, xp, jnp.zeros_like(xp)), jnp.where(lane >= hw, xp, jnp.zeros_like(xp))]
        rows = []
        for r in range(nsub):
            lhs_parts, rhs_parts = [], []
            for h in range(2):
                j = 2 * p + h
                col = acs_blk[r * sub:(r + 1) * sub, j * sub:(j + 1) * sub]
                for c in range(r + 1):
                    seg = col - acst[j:j + 1, c * sub:(c + 1) * sub]
                    if c == r:
                        seg = jnp.minimum(seg, 0.0)
                    lhs_parts.append((cb_blk[r][c] * jnp.exp2(seg)).astype(BF16))
                rhs_parts.append(rhs[h][:(r + 1) * sub, :])
            rows.append(_dot(jnp.concatenate(lhs_parts, axis=1), jnp.concatenate(rhs_parts, axis=0)))
        pair_out.append(jnp.concatenate(rows, axis=0))
    y = y + jnp.concatenate(pair_out, axis=1)
    y = y * _silu(z_ref[...])
    ss_ref[...] = jnp.broadcast_to(jnp.sum(y * y, axis=1, keepdims=True), (l, sub))
    o_ref[...] = (y * cp[SSD_CONV + 2:SSD_CONV + 3, :gw]).astype(o_ref.dtype)


def _head_expand(lanes_per_head, n_heads):
    return jnp.repeat(jnp.eye(n_heads, dtype=F32), lanes_per_head, axis=1)


def ssd_core(proj, dt_t, conv_w, conv_b, dt_bias, a_log, d_skip, norm_g, *, batch, seq):
    l = min(SSD_CHUNK, seq)
    nc = seq // l
    g_, j_ = SSD_GROUPS, SSD_HPG
    inner = g_ * SSD_GROUP_W
    nx = inner // SSD_GROUP_W
    nb = 2 * inner // SSD_STATE
    expand64 = _head_expand(SSD_HEAD_DIM, j_)
    r16 = jnp.concatenate([
        jnp.concatenate([_head_expand(LANES, j_), expand64, jnp.zeros_like(expand64)], axis=1),
        jnp.concatenate([jnp.zeros((j_, j_ * LANES + SSD_GROUP_W), F32), expand64], axis=1)], axis=0)
    r = jnp.concatenate([r16, r16, r16], axis=0).astype(BF16)
    bc_w = g_ * SSD_STATE

    def per_group(p, rows):
        parts = [p[:, :inner].reshape(rows, g_, SSD_GROUP_W), p[:, inner:inner + bc_w].reshape(rows, g_, SSD_STATE),
                 p[:, inner + bc_w:].reshape(rows, g_, SSD_STATE)]
        return jnp.concatenate(parts, axis=2).transpose(1, 0, 2)

    cp_w = SSD_GROUP_W + 2 * SSD_STATE
    x_only = ((0, 0), (0, 0), (0, cp_w - SSD_GROUP_W))
    dskip = jnp.pad(jnp.repeat(d_skip.reshape(g_, 1, j_), SSD_HEAD_DIM, axis=2), x_only)
    gain = jnp.pad(norm_g.reshape(g_, 1, SSD_GROUP_W), x_only)
    conv_p = jnp.concatenate([per_group(conv_w, SSD_CONV), per_group(conv_b.reshape(1, -1), 1), dskip, gain,
                              jnp.zeros((g_, 8 - SSD_CONV - 3, cp_w), F32)], axis=1).astype(F32)
    head_p = jnp.pad(jnp.stack([dt_bias.reshape(g_, j_), a_log.reshape(g_, j_)], axis=2),
                     ((0, 0), (0, 0), (0, LANES - 2))).astype(F32)
    gps = SSD_GROUPS_PER_STEP
    xw, bw = gps * SSD_GROUP_W, gps * SSD_STATE
    in_specs = [
        pl.BlockSpec((1, l, xw), lambda b, g, c: (b, c, g)),
        pl.BlockSpec((1, l, xw), lambda b, g, c: (b, c, nx // gps + g)),
        pl.BlockSpec((1, l, bw), lambda b, g, c: (b, c, nb // gps + g)),
        pl.BlockSpec((1, l, bw), lambda b, g, c: (b, c, (nb + g_) // gps + g)),
        pl.BlockSpec((gps * j_, l), lambda b, g, c: (g, b * nc + c)),
        pl.BlockSpec((gps * j_, l), lambda b, g, c: (g, b * nc + jnp.minimum(c + 1, nc - 1))),
        pl.BlockSpec((gps, 8, cp_w), lambda b, g, c: (g, 0, 0)),
        pl.BlockSpec((gps, 8, LANES), lambda b, g, c: (g, 0, 0)),
        pl.BlockSpec(r.shape, lambda b, g, c: (0, 0)),
    ]
    return pl.pallas_call(
        _ssd_kernel,
        out_shape=[jax.ShapeDtypeStruct((batch, seq, inner), BF16),
                   jax.ShapeDtypeStruct((batch, seq, g_ * LANES), F32)],
        grid=(batch, g_ // gps, nc), in_specs=in_specs,
        out_specs=[pl.BlockSpec((1, l, xw), lambda b, g, c: (b, c, g)),
                   pl.BlockSpec((1, l, gps * LANES), lambda b, g, c: (b, c, g))],
        scratch_shapes=[pltpu.VMEM((gps, CONV_HALO + l, SSD_GROUP_W), F32),
                        pltpu.VMEM((gps, CONV_HALO + l, SSD_STATE), F32),
                        pltpu.VMEM((gps, CONV_HALO + l, SSD_STATE), F32),
                        pltpu.VMEM((gps, SSD_STATE, SSD_GROUP_W), F32),
                        pltpu.VMEM((gps, 2 * j_, l), F32),
                        pltpu.VMEM((gps, l, r.shape[1]), F32),
                        pltpu.VMEM((gps, l, SSD_GROUP_W), F32),
                        pltpu.VMEM((gps, l, SSD_STATE), BF16),
                        pltpu.VMEM((gps, l, SSD_STATE), BF16)],
        compiler_params=_params("parallel", "parallel", "arbitrary"), name="ssd_core",
    )(proj, proj, proj, proj, dt_t, dt_t, conv_p, head_p, r)


def _mlstm_kernel(q_ref, k_ref, v_ref, o_ref, gc_ref, gr_ref, gbc_ref, gbr_ref, hg_ref,
                  out_ref, c_ref, n_ref, m_ref):
    @pl.when(pl.program_id(1) == 0)
    def _():
        c_ref[...] = jnp.zeros_like(c_ref)
        n_ref[...] = jnp.zeros_like(n_ref)
        m_ref[...] = jnp.zeros_like(m_ref)

    for hi in range(MLSTM_HEADS):
        qk_cols, v_cols = pl.ds(hi * MLSTM_DQK, MLSTM_DQK), pl.ds(hi * MLSTM_DV, MLSTM_DV)
        _mlstm_head(q_ref.at[0, :, qk_cols], k_ref.at[0, :, qk_cols], v_ref.at[0, :, v_cols], o_ref.at[0, :, v_cols],
                    gc_ref.at[0, hi], gr_ref.at[0, hi], gbc_ref.at[hi], gbr_ref.at[hi], hg_ref.at[:, v_cols],
                    out_ref.at[0, :, v_cols], c_ref.at[hi], n_ref.at[hi], m_ref.at[hi])


def _mlstm_head(q_ref, k_ref, v_ref, o_ref, gc_ref, gr_ref, gbc_ref, gbr_ref, hg_ref,
                out_ref, c_ref, n_ref, m_ref):
    l = q_ref.shape[0]
    q = q_ref[...] * (MLSTM_DQK ** -0.5)
    k = k_ref[...]
    v16 = v_ref[...].astype(BF16)
    q16 = q.astype(BF16)
    gcol = gc_ref[...] + gbc_ref[...]
    grow = gr_ref[...] + gbr_ref[...]
    i_col, lf_col = gcol[:, 0:1], _log_sigmoid(gcol[:, 1:2])
    i_row, lf_row = grow[0:1, :], _log_sigmoid(grow[1:2, :])
    bcum = _cumsum_rows(lf_col)
    bcum_row = _cumsum_cols(lf_row)
    m_st = m_ref[0:1, 0:1]

    causal = _causal(l)
    intra = jnp.where(causal, bcum - bcum_row + i_row, -jnp.inf)
    inter = bcum + m_st
    m_t = jnp.maximum(inter, jnp.max(intra, axis=1, keepdims=True))
    w = jnp.exp(intra - m_t)
    scale_inter = jnp.exp(inter - m_t)
    qk = _dot_nt(q16, k.astype(BF16)) * w
    num = _dot(qk.astype(BF16), v16) + scale_inter * _dot(q16, c_ref[...].astype(BF16))
    qn = jnp.sum(q * n_ref[...], axis=1, keepdims=True)
    den = jnp.sum(qk, axis=1, keepdims=True) + scale_inter * qn
    h = num / jnp.maximum(jnp.abs(den), jnp.exp(-m_t))

    b_last = bcum[l - 1:l, :]
    tail = b_last - bcum + i_col
    m_new = jnp.maximum(b_last + m_st, jnp.max(tail, axis=0, keepdims=True))
    carry_scale = jnp.exp(b_last + m_st - m_new)
    wk = k * jnp.exp(tail - m_new)
    c_ref[...] = carry_scale * c_ref[...] + _dot_tn(wk.astype(BF16), v16)
    n_ref[...] = carry_scale * n_ref[...] + jnp.sum(wk, axis=0, keepdims=True)
    m_ref[...] = jnp.broadcast_to(m_new, m_ref.shape)

    hn = _rms_scale(h, hg_ref[...])
    out_ref[...] = (_sigmoid(o_ref[...]) * hn).astype(out_ref.dtype)


def mlstm_core(proj, gates, gate_b, head_g, *, batch, seq):
    l = MLSTM_CHUNK
    h_ = MLSTM_HEADS
    qk_w = h_ * MLSTM_DQK
    v_w = h_ * MLSTM_DV
    gsplit = jnp.stack([gates[..., :h_], gates[..., h_:2 * h_]], axis=-1)
    gcol = gsplit.transpose(0, 2, 1, 3)
    grow = gsplit.transpose(0, 2, 3, 1)
    gb = jnp.stack([gate_b[:h_], gate_b[h_:]], axis=-1).astype(F32)
    return pl.pallas_call(
        _mlstm_kernel, out_shape=jax.ShapeDtypeStruct((batch, seq, v_w), BF16),
        grid=(batch, seq // l),
        in_specs=[pl.BlockSpec((1, l, qk_w), lambda b, c: (b, c, 0)),
                  pl.BlockSpec((1, l, qk_w), lambda b, c: (b, c, 1)),
                  pl.BlockSpec((1, l, v_w), lambda b, c: (b, c, 2 * qk_w // v_w)),
                  pl.BlockSpec((1, l, v_w), lambda b, c: (b, c, 2 * qk_w // v_w + 1)),
                  pl.BlockSpec((1, h_, l, 2), lambda b, c: (b, 0, c, 0)),
                  pl.BlockSpec((1, h_, 2, l), lambda b, c: (b, 0, 0, c)),
                  pl.BlockSpec((h_, 1, 2), lambda b, c: (0, 0, 0)),
                  pl.BlockSpec((h_, 2, 1), lambda b, c: (0, 0, 0)),
                  pl.BlockSpec((1, v_w), lambda b, c: (0, 0))],
        out_specs=pl.BlockSpec((1, l, v_w), lambda b, c: (b, c, 0)),
        scratch_shapes=[pltpu.VMEM((h_, MLSTM_DQK, MLSTM_DV), F32),
                        pltpu.VMEM((h_, 1, MLSTM_DQK), F32),
                        pltpu.VMEM((h_, 8, LANES), F32)],
        compiler_params=_params("parallel", "arbitrary"), name="mlstm_core",
    )(proj, proj, proj, proj, gcol, grow, gb.reshape(h_, 1, 2), gb.reshape(h_, 2, 1),
      head_g.reshape(1, v_w).astype(F32))


def _s5_discretize_kernel(logdt_ref, are_ref, aim_ref, bre_ref, bim_ref,
                          abre_ref, abim_ref, bbre_ref, bbim_ref):
    ar, ai = are_ref[...], aim_ref[...]
    dt = jnp.exp(logdt_ref[...])
    mag = jnp.exp(ar * dt)
    abar_re, abar_im = mag * jnp.cos(ai * dt), mag * jnp.sin(ai * dt)
    den = ar * ar + ai * ai
    zoh_re = ((abar_re - 1.0) * ar + abar_im * ai) / den
    zoh_im = (abar_im * ar - (abar_re - 1.0) * ai) / den
    abre_ref[...] = abar_re
    abim_ref[...] = abar_im
    for c in range(S5_GROUP):
        br, bi = bre_ref[c], bim_ref[c]
        bbre_ref[c] = zoh_re * br - zoh_im * bi
        bbim_ref[c] = zoh_re * bi + zoh_im * br


def s5_discretize(log_dt, a_re, a_im, b_re, b_im):
    g_, p_ = a_re.shape
    sds = jax.ShapeDtypeStruct
    return pl.pallas_call(
        _s5_discretize_kernel,
        out_shape=[sds((g_, p_), F32), sds((g_, p_), F32),
                   sds((S5_GROUP, g_, p_), F32), sds((S5_GROUP, g_, p_), F32)],
        name="s5_discretize",
    )(log_dt.reshape(g_, 1).astype(F32), a_re.astype(F32), a_im.astype(F32),
      b_re.astype(F32).transpose(2, 0, 1), b_im.astype(F32).transpose(2, 0, 1))


def _s5_kernel(u_ref, bw_ref, cw_ref, a_ref, d_ref, y_ref, st_ref, bu_ref, *, batch):
    ts = u_ref.shape[0]
    half = S5_BLOCK_STATE
    nblk = S5_BLOCKS_PER_STEP

    @pl.when(pl.program_id(1) == 0)
    def _():
        st_ref[...] = jnp.zeros_like(st_ref)

    def lanes(n):
        return slice(n * LANES, (n + 1) * LANES)

    def project_in(n):
        u = u_ref[:, :, lanes(n)].reshape(ts * batch, LANES)
        bu_ref[n] = _dot(u.astype(BF16), bw_ref[n])

    def scan(n):
        a = a_ref[n]
        a_re = jnp.broadcast_to(a[:, :half], (batch, half))
        a_im = jnp.broadcast_to(a[:, half:], (batch, half))
        s_re, s_im = st_ref[n, :, 0:half], st_ref[n, :, half:2 * half]
        for t in range(ts):
            rows = slice(t * batch, (t + 1) * batch)
            s_re, s_im = (a_re * s_re - a_im * s_im + bu_ref[n, rows, 0:half],
                          a_re * s_im + a_im * s_re + bu_ref[n, rows, half:2 * half])
            bu_ref[n, rows, 0:half] = s_re
            bu_ref[n, rows, half:2 * half] = s_im
        st_ref[n, :, 0:half] = s_re
        st_ref[n, :, half:2 * half] = s_im

    def project_out(n):
        u = u_ref[:, :, lanes(n)].reshape(ts * batch, LANES)
        y = _dot(bu_ref[n].astype(BF16), cw_ref[n]) + d_ref[:, lanes(n)] * u
        y_ref[:, :, lanes(n)] = jax.nn.gelu(y).astype(y_ref.dtype).reshape(ts, batch, LANES)

    project_in(0)
    for n in range(nblk):
        if n + 1 < nblk:
            project_in(n + 1)
        scan(n)
        if n > 0:
            project_out(n - 1)
    project_out(nblk - 1)


def s5_core(u_tm, bw, cw, a_vec, d_skip, *, batch, seq):
    width = u_tm.shape[2]
    nblk = width // LANES
    ts = min(S5_TIME_BLOCK, seq)
    per = S5_BLOCKS_PER_STEP
    return pl.pallas_call(
        functools.partial(_s5_kernel, batch=batch),
        out_shape=jax.ShapeDtypeStruct((seq, batch, width), BF16),
        grid=(nblk // per, seq // ts),
        in_specs=[pl.BlockSpec((ts, batch, per * LANES), lambda j, i: (i, 0, j)),
                  pl.BlockSpec((per, LANES, 2 * S5_BLOCK_STATE), lambda j, i: (j, 0, 0)),
                  pl.BlockSpec((per, 2 * S5_BLOCK_STATE, LANES), lambda j, i: (j, 0, 0)),
                  pl.BlockSpec((per, 1, 2 * S5_BLOCK_STATE), lambda j, i: (j, 0, 0)),
                  pl.BlockSpec((1, per * LANES), lambda j, i: (0, j))],
        out_specs=pl.BlockSpec((ts, batch, per * LANES), lambda j, i: (i, 0, j)),
        scratch_shapes=[pltpu.VMEM((per, batch, 2 * S5_BLOCK_STATE), F32),
                        pltpu.VMEM((per, ts * batch, 2 * S5_BLOCK_STATE), F32)],
        compiler_params=_params("parallel", "arbitrary"), name="s5_core",
    )(u_tm, bw, cw, a_vec, d_skip.reshape(1, width).astype(F32))


def _s5_block_weights(abar_re, abar_im, bbar_re, bbar_im, c_re, c_im):
    g_, p_ = abar_re.shape
    gb = S5_GROUPS_PER_BLOCK
    nblk = g_ // gb
    eye = jnp.eye(gb, dtype=F32)

    def in_blocks(bbar):
        b = bbar.reshape(S5_GROUP, nblk, gb, p_).transpose(1, 2, 0, 3)
        return jnp.einsum("ngcp,gh->ngchp", b, eye).reshape(nblk, gb * S5_GROUP, gb * p_)

    def out_blocks(cmat):
        c = cmat.reshape(nblk, gb, S5_GROUP, p_).transpose(0, 1, 3, 2)
        return jnp.einsum("ngpc,gh->ngphc", c, eye).reshape(nblk, gb * p_, gb * S5_GROUP)

    bw = jnp.concatenate([in_blocks(bbar_re), in_blocks(bbar_im)], axis=2).astype(BF16)
    cw = jnp.concatenate([out_blocks(c_re.astype(F32)), out_blocks(-c_im.astype(F32))], axis=1).astype(BF16)
    a_vec = jnp.concatenate([abar_re.reshape(nblk, 1, gb * p_), abar_im.reshape(nblk, 1, gb * p_)], axis=2)
    return bw, cw, a_vec


SSD_MAIN_COLS = 2 * SSD_GROUPS * SSD_GROUP_W + 2 * SSD_GROUPS * SSD_STATE
MLSTM_MAIN_COLS = 2 * MLSTM_HEADS * (MLSTM_DQK + MLSTM_DV)
TOKEN_TILE = 1024
COL_TILE = 1024
IN_PROJ_COL_TILE = 2048


def _in_proj_col_tile(g):
    return IN_PROJ_COL_TILE if g is None else COL_TILE
MLP_TOKEN_TILE = 1024
MLP_FF_TILE = 512


def _small_head(w_in, idx, main, transposed):
    w = jnp.pad(w_in[idx, :, main:], ((0, 0), (0, LANES - (w_in.shape[2] - main)))).astype(BF16)
    return w.T if transposed else w


def _ssd_layer(h, xin, g, w_in, w_in16, w_out16, idx, conv_w, conv_b, dt_bias, a_log, d_skip, norm_g, *, batch, seq):
    inner = SSD_GROUPS * SSD_GROUP_W
    main = SSD_MAIN_COLS
    proj, dt_t = in_proj(xin, g, w_in16, idx, n=main, w_small=_small_head(w_in, idx, main, True), small="cols",
                         tm=TOKEN_TILE, tn=_in_proj_col_tile(g), name="ssd_in_proj")
    y16, ss = ssd_core(proj.reshape(batch, seq, main), dt_t, conv_w, conv_b, dt_bias, a_log, d_skip, norm_g,
                       batch=batch, seq=seq)
    return rms_matmul_residual(y16.reshape(batch * seq, inner), ss.reshape(batch * seq, -1), w_out16, idx, h,
                               tm=TOKEN_TILE, tn=COL_TILE // 2, name="ssd_out_proj")


def _mlstm_layer(h, xin, g, w_in, w_in16, w_out16, idx, gate_b, head_g, *, batch, seq):
    main = MLSTM_MAIN_COLS
    proj, gates = in_proj(xin, g, w_in16, idx, n=main, w_small=_small_head(w_in, idx, main, False), small="rows",
                          tm=TOKEN_TILE, tn=_in_proj_col_tile(g), name="mlstm_in_proj")
    hs = mlstm_core(proj.reshape(batch, seq, main),
                    gates[:, :2 * MLSTM_HEADS].reshape(batch, seq, 2 * MLSTM_HEADS),
                    gate_b, head_g, batch=batch, seq=seq)
    return matmul_residual(hs.reshape(batch * seq, -1), w_out16, idx, h,
                           tm=TOKEN_TILE // 2, tn=w_out16.shape[2], name="mlstm_out_proj")


def _s5_layer(h, xin, g, w_in16, w_out16, idx, b_re, b_im, c_re, c_im, d_skip, log_dt, a_re, a_im, *, batch, seq):
    width = w_in16.shape[2]
    tm = min(TOKEN_TILE, seq)
    nt = seq // tm
    tn = _in_proj_col_tile(g)
    ncol = width // tn
    u_tm = in_proj(xin, g, w_in16, idx, n=width, tm=tm, tn=tn,
                   out_index=lambda i, j: (i % nt, (i // nt) * ncol + j),
                   out_shape=(seq, batch * width), name="s5_in_proj")
    abar_re, abar_im, bbar_re, bbar_im = s5_discretize(log_dt, a_re, a_im, b_re, b_im)
    bw, cw, a_vec = _s5_block_weights(abar_re, abar_im, bbar_re, bbar_im, c_re, c_im)
    y_tm = s5_core(u_tm.reshape(seq, batch, width), bw, cw, a_vec, d_skip, batch=batch, seq=seq)
    return glu_matmul_residual(y_tm.reshape(seq, batch * width), w_out16, idx, h,
                               batch=batch, tm=tm, tn=COL_TILE, name="s5_out_proj")


def kernel(x, norm_mix_g, norm_mlp_g, ssd_w_in, ssd_conv_w, ssd_conv_b, ssd_dt_bias, ssd_a_log, ssd_d, ssd_norm_g, ssd_w_out, mlstm_w_in, mlstm_gate_b, mlstm_head_g, mlstm_w_out, s5_w_in, s5_b_re, s5_b_im, s5_c_re, s5_c_im, s5_d, s5_log_dt, s5_a_re, s5_a_im, s5_w_out, mlp_w1, mlp_w2, final_norm_g):
    batch, seq, d = x.shape
    depth = norm_mix_g.shape[0]
    ssd_in16, ssd_out16 = ssd_w_in.astype(BF16), ssd_w_out.astype(BF16)
    mlstm_in16, mlstm_out16 = mlstm_w_in.astype(BF16), mlstm_w_out.astype(BF16)
    s5_in16, s5_out16 = s5_w_in.astype(BF16), s5_w_out.astype(BF16)
    mlp_w1_16, mlp_w2_16 = mlp_w1.astype(BF16), mlp_w2.astype(BF16)
    h = x.reshape(batch * seq, d)
    xin, g = h, norm_mix_g[0]
    for layer in range(depth):
        kind, idx = layer % 3, layer // 3
        if kind == 0:
            h = _ssd_layer(h, xin, g, ssd_w_in, ssd_in16, ssd_out16, idx, ssd_conv_w[idx], ssd_conv_b[idx],
                           ssd_dt_bias[idx], ssd_a_log[idx], ssd_d[idx], ssd_norm_g[idx], batch=batch, seq=seq)
        elif kind == 1:
            h = _mlstm_layer(h, xin, g, mlstm_w_in, mlstm_in16, mlstm_out16, idx, mlstm_gate_b[idx],
                             mlstm_head_g[idx], batch=batch, seq=seq)
        else:
            h = _s5_layer(h, xin, g, s5_in16, s5_out16, idx, s5_b_re[idx], s5_b_im[idx], s5_c_re[idx], s5_c_im[idx],
                          s5_d[idx], s5_log_dt[idx], s5_a_re[idx], s5_a_im[idx], batch=batch, seq=seq)
        final = layer == depth - 1
        out = mlp_block(h, norm_mlp_g[layer], mlp_w1_16, mlp_w2_16, layer,
                        final_norm_g if final else norm_mix_g[layer + 1], final=final,
                        tm=min(MLP_TOKEN_TILE, batch * seq), tf=MLP_FF_TILE, name=f"mlp_{layer}")
        if final:
            h = out
        else:
            h, xin = out
            g = None
    return h.reshape(batch, seq, d)
```

```python
import functools
import math

import jax
import jax.numpy as jnp
from jax import lax
from jax.experimental import pallas as pl
from jax.experimental.pallas import tpu as pltpu

F32 = jnp.float32
BF16 = jnp.bfloat16

NORM_EPS = 1e-5
LOG2E = math.log2(math.e)
LANES = 128
VMEM_LIMIT_BYTES = 56 * 1024 * 1024

SSD_HEAD_DIM = 64
SSD_GROUPS = 8
SSD_HPG = 8
SSD_STATE = 128
SSD_CONV = 4
SSD_GROUP_W = SSD_HPG * SSD_HEAD_DIM
SSD_CHUNK = 256
SSD_GROUPS_PER_STEP = 8
CONV_HALO = 8

MLSTM_HEADS = 4
MLSTM_DQK = 256
MLSTM_DV = 512
MLSTM_CHUNK = 256

S5_GROUP = 16
S5_STATE = 64
S5_GROUPS_PER_BLOCK = LANES // S5_GROUP
S5_BLOCK_STATE = S5_GROUPS_PER_BLOCK * S5_STATE
S5_TIME_BLOCK = 64
S5_BLOCKS_PER_STEP = 4


def _params(*semantics):
    return pltpu.CompilerParams(dimension_semantics=semantics, vmem_limit_bytes=VMEM_LIMIT_BYTES)


def _dot(a, b):
    return jnp.dot(a, b, preferred_element_type=F32)


def _dot_nt(a, b):
    return lax.dot_general(a, b, (((1,), (1,)), ((), ())), preferred_element_type=F32)


def _dot_tn(a, b):
    return lax.dot_general(a, b, (((0,), (0,)), ((), ())), preferred_element_type=F32)


def _split3(x):
    hi = x.astype(BF16)
    r1 = x - hi.astype(F32)
    mid = r1.astype(BF16)
    lo = (r1 - mid.astype(F32)).astype(BF16)
    return hi, mid, lo


def _tri(n, upper):
    r = lax.broadcasted_iota(jnp.int32, (n, n), 0)
    c = lax.broadcasted_iota(jnp.int32, (n, n), 1)
    keep = (r <= c) if upper else (c <= r)
    return jnp.where(keep, 1.0, 0.0).astype(BF16)


def _cumsum_rows(x):
    t = _tri(x.shape[0], upper=False)
    hi, mid, lo = _split3(x)
    return _dot(t, hi) + _dot(t, mid) + _dot(t, lo)


def _cumsum_cols(x):
    t = _tri(x.shape[1], upper=True)
    hi, mid, lo = _split3(x)
    return _dot(hi, t) + _dot(mid, t) + _dot(lo, t)


def _causal(n):
    r = lax.broadcasted_iota(jnp.int32, (n, n), 0)
    c = lax.broadcasted_iota(jnp.int32, (n, n), 1)
    return c <= r


def _softplus(x):
    return jnp.maximum(x, 0.0) + jnp.log1p(jnp.exp(-jnp.abs(x)))


def _sigmoid(x):
    return 1.0 / (1.0 + jnp.exp(-x))


def _silu(x):
    return x * _sigmoid(x)


def _log_sigmoid(x):
    return -_softplus(-x)


def _rms_scale(x, g):
    ms = jnp.mean(x * x, axis=-1, keepdims=True)
    return x * lax.rsqrt(ms + NORM_EPS) * g


def _in_proj_kernel(*refs, normalize, small):
    it = iter(refs)
    x_ref = next(it)
    g_ref = next(it) if normalize else None
    w_ref = next(it)
    ws_ref = next(it) if small else None
    o_ref = next(it)
    os_ref = next(it) if small else None
    xn_ref = next(it) if normalize else x_ref

    if normalize or small:
        @pl.when(pl.program_id(1) == 0)
        def _():
            if normalize:
                xn_ref[...] = _rms_scale(x_ref[...], g_ref[...]).astype(BF16)
            if small == "rows":
                os_ref[...] = _dot(xn_ref[...], ws_ref[...])
            elif small == "cols":
                os_ref[...] = _dot_nt(ws_ref[...], xn_ref[...])

    o_ref[...] = _dot(xn_ref[...], w_ref[...])


def in_proj(x, g, w, layer, *, n, w_small=None, small=None, tm, tn, out_index=None, out_shape=None, name):
    n_tok, k = x.shape
    assert n_tok % tm == 0 and n % tn == 0
    normalize = g is not None
    out_index = out_index or (lambda i, j: (i, j))
    in_specs = [pl.BlockSpec((tm, k), lambda i, j: (i, 0))]
    args = [x]
    if normalize:
        in_specs.append(pl.BlockSpec((1, k), lambda i, j: (0, 0)))
        args.append(g.reshape(1, k).astype(F32))
    in_specs.append(pl.BlockSpec((None, k, tn), lambda i, j: (layer, 0, j)))
    args.append(w)
    out_shapes = [jax.ShapeDtypeStruct(out_shape or (n_tok, n), F32)]
    out_specs = [pl.BlockSpec((tm, tn), out_index)]
    if small:
        in_specs.append(pl.BlockSpec(w_small.shape, lambda i, j: (0, 0)))
        args.append(w_small)
        if small == "cols":
            out_shapes.append(jax.ShapeDtypeStruct((LANES, n_tok), F32))
            out_specs.append(pl.BlockSpec((LANES, tm), lambda i, j: (0, i)))
        else:
            out_shapes.append(jax.ShapeDtypeStruct((n_tok, LANES), F32))
            out_specs.append(pl.BlockSpec((tm, LANES), lambda i, j: (i, 0)))
    outs = pl.pallas_call(
        functools.partial(_in_proj_kernel, normalize=normalize, small=small),
        out_shape=out_shapes, grid=(n_tok // tm, n // tn), in_specs=in_specs, out_specs=out_specs,
        scratch_shapes=[pltpu.VMEM((tm, k), BF16)] if normalize else [],
        compiler_params=_params("parallel", "arbitrary"), name=name)(*args)
    return outs if small else outs[0]


def _matmul_res_kernel(a_ref, w_ref, res_ref, o_ref):
    o_ref[...] = res_ref[...] + _dot(a_ref[...], w_ref[...])


def matmul_residual(a, w, layer, res, *, tm, tn, name):
    t, k = a.shape
    n = w.shape[2]
    return pl.pallas_call(
        _matmul_res_kernel, out_shape=jax.ShapeDtypeStruct((t, n), F32),
        grid=(t // tm, n // tn),
        in_specs=[pl.BlockSpec((tm, k), lambda i, j: (i, 0)),
                  pl.BlockSpec((None, k, tn), lambda i, j: (layer, 0, j)),
                  pl.BlockSpec((tm, tn), lambda i, j: (i, j))],
        out_specs=pl.BlockSpec((tm, tn), lambda i, j: (i, j)),
        compiler_params=_params("parallel", "arbitrary"), name=name)(a, w, res)


def _rms_matmul_res_kernel(a_ref, ss_ref, w_ref, res_ref, o_ref):
    k = a_ref.shape[1]
    ss = ss_ref[...]
    tot = ss[:, :LANES]
    for q in range(1, ss.shape[1] // LANES):
        tot = tot + ss[:, q * LANES:(q + 1) * LANES]
    inv = lax.rsqrt(tot * (1.0 / k) + NORM_EPS)
    y = _dot(a_ref[...], w_ref[...])
    o_ref[...] = res_ref[...] + jnp.concatenate([inv] * (y.shape[1] // LANES), axis=1) * y


def rms_matmul_residual(a, ss, w, layer, res, *, tm, tn, name):
    t, k = a.shape
    n = w.shape[2]
    w_mode = dict(pipeline_mode=pl.Buffered(1)) if tn == n else {}
    return pl.pallas_call(
        _rms_matmul_res_kernel, out_shape=jax.ShapeDtypeStruct((t, n), F32),
        grid=(t // tm, n // tn),
        in_specs=[pl.BlockSpec((tm, k), lambda i, j: (i, 0)),
                  pl.BlockSpec((tm, ss.shape[1]), lambda i, j: (i, 0)),
                  pl.BlockSpec((None, k, tn), lambda i, j: (layer, 0, j), **w_mode),
                  pl.BlockSpec((tm, tn), lambda i, j: (i, j))],
        out_specs=pl.BlockSpec((tm, tn), lambda i, j: (i, j)),
        compiler_params=_params("parallel", "arbitrary"), name=name)(a, ss, w, res)


def _glu_res_kernel(a_ref, wv_ref, wg_ref, res_ref, o_ref):
    a = a_ref[...]
    val = _dot(a, wv_ref[...])
    gate = _dot(a, wg_ref[...])
    o_ref[...] = res_ref[...] + val * _sigmoid(gate)


def glu_matmul_residual(a_tm, w, layer, res, *, batch, tm, tn, name):
    s = a_tm.shape[0]
    k = w.shape[1]
    n = w.shape[2] // 2
    nt = s // tm
    return pl.pallas_call(
        _glu_res_kernel, out_shape=jax.ShapeDtypeStruct((batch * s, n), F32),
        grid=(batch * nt, n // tn),
        in_specs=[pl.BlockSpec((tm, k), lambda i, j: (i % nt, i // nt)),
                  pl.BlockSpec((None, k, tn), lambda i, j: (layer, 0, j)),
                  pl.BlockSpec((None, k, tn), lambda i, j: (layer, 0, j + n // tn)),
                  pl.BlockSpec((tm, tn), lambda i, j: (i, j))],
        out_specs=pl.BlockSpec((tm, tn), lambda i, j: (i, j)),
        compiler_params=_params("parallel", "arbitrary"), name=name)(a_tm, w, w, res)


def _mlp_kernel(x_ref, g_ref, w1_ref, w2_ref, gn_ref, o_ref, *rest, final):
    xn_ref = rest[-1]
    f = pl.program_id(1)

    @pl.when(f == 0)
    def _():
        x = x_ref[...]
        xn_ref[...] = _rms_scale(x, g_ref[...]).astype(BF16)
        o_ref[...] = x

    h1 = jnp.maximum(_dot(xn_ref[...], w1_ref[...]), 0.0)
    o_ref[...] += _dot((h1 * h1).astype(BF16), w2_ref[...])

    @pl.when(f == pl.num_programs(1) - 1)
    def _():
        normed = _rms_scale(o_ref[...], gn_ref[...])
        if final:
            o_ref[...] = normed
        else:
            rest[0][...] = normed.astype(BF16)


def mlp_block(x, g, w1, w2, layer, g_next, *, final, tm, tf, name):
    t, d = x.shape
    dff = w1.shape[2]
    row_spec = pl.BlockSpec((tm, d), lambda i, f: (i, 0))
    out_shape = [jax.ShapeDtypeStruct((t, d), F32)] + ([] if final else [jax.ShapeDtypeStruct((t, d), BF16)])
    outs = pl.pallas_call(
        functools.partial(_mlp_kernel, final=final),
        out_shape=out_shape, grid=(t // tm, dff // tf),
        in_specs=[row_spec,
                  pl.BlockSpec((1, d), lambda i, f: (0, 0)),
                  pl.BlockSpec((None, d, tf), lambda i, f: (layer, 0, f)),
                  pl.BlockSpec((None, tf, d), lambda i, f: (layer, f, 0)),
                  pl.BlockSpec((1, d), lambda i, f: (0, 0))],
        out_specs=[row_spec] * len(out_shape),
        scratch_shapes=[pltpu.VMEM((tm, d), BF16)],
        compiler_params=_params("parallel", "arbitrary"), name=name,
    )(x, g.reshape(1, d), w1, w2, g_next.reshape(1, d))
    return outs[0] if final else outs


def _causal_conv_silu(buf_ref, cur, w, b):
    l = cur.shape[0]
    buf_ref[CONV_HALO:CONV_HALO + l, :] = cur
    acc = b + w[SSD_CONV - 1:SSD_CONV, :] * cur
    for k in range(SSD_CONV - 1):
        start = CONV_HALO - (SSD_CONV - 1) + k
        acc = acc + w[k:k + 1, :] * buf_ref[start:start + l, :]
    buf_ref[0:CONV_HALO, :] = cur[l - CONV_HALO:, :]
    return _silu(acc)


def _ssd_head_rows(dt_raw_t, head_p, rows_ref):
    dtt = _softplus(dt_raw_t + head_p[:, 0:1])
    rows_ref[0:SSD_HPG, :] = _cumsum_cols(dtt * (-LOG2E * jnp.exp(head_p[:, 1:2])))
    rows_ref[SSD_HPG:, :] = dtt


def _ssd_kernel(z_ref, x_ref, b_ref, c_ref, dtt_ref, dttn_ref, cp_ref, hp_ref, r_ref,
                o_ref, ss_ref, xbuf, bbuf, cbuf, state_ref, rows_ref, wide_ref, xs_ref, bm_ref, cm_ref):
    gw, ns = SSD_GROUP_W, SSD_STATE

    @pl.when(pl.program_id(2) == 0)
    def _():
        state_ref[...] = jnp.zeros_like(state_ref)
        for buf in (xbuf, bbuf, cbuf):
            buf[:, 0:CONV_HALO, :] = jnp.zeros((buf.shape[0], CONV_HALO, buf.shape[2]), F32)
        for gi in range(SSD_GROUPS_PER_STEP):
            _ssd_head_rows(dtt_ref[gi * SSD_HPG:(gi + 1) * SSD_HPG, :], hp_ref[gi], rows_ref.at[gi])

    def prepare(gi):
        x_cols, bc_cols = pl.ds(gi * gw, gw), pl.ds(gi * ns, ns)
        return _ssd_group_prepare(
            x_ref.at[0, :, x_cols], b_ref.at[0, :, bc_cols], c_ref.at[0, :, bc_cols],
            dttn_ref.at[pl.ds(gi * SSD_HPG, SSD_HPG), :], cp_ref.at[gi], hp_ref.at[gi], r_ref,
            xbuf.at[gi], bbuf.at[gi], cbuf.at[gi], rows_ref.at[gi], wide_ref.at[gi],
            xs_ref.at[gi], bm_ref.at[gi], cm_ref.at[gi])

    for gi in range(SSD_GROUPS_PER_STEP):
        acst = prepare(gi)
        x_cols = pl.ds(gi * gw, gw)
        _ssd_group_main(acst, z_ref.at[0, :, x_cols], cp_ref.at[gi], o_ref.at[0, :, x_cols],
                        ss_ref.at[0, :, pl.ds(gi * LANES, LANES)], state_ref.at[gi], wide_ref.at[gi],
                        xs_ref.at[gi], bm_ref.at[gi], cm_ref.at[gi])


def _ssd_group_prepare(x_ref, b_ref, c_ref, dttn_ref, cp_ref, hp_ref, r_ref,
                       xbuf, bbuf, cbuf, rows_ref, wide_ref, xs_ref, bm_ref, cm_ref):
    gw, ns = SSD_GROUP_W, SSD_STATE
    acst = rows_ref[0:SSD_HPG, :]
    wide_ref[...] = _dot_tn(jnp.concatenate(_split3(rows_ref[...]), axis=0), r_ref[...])

    cp = cp_ref[...]
    cw, cbias = cp[0:SSD_CONV, :], cp[SSD_CONV:SSD_CONV + 1, :]
    xs_ref[...] = _causal_conv_silu(xbuf, x_ref[...], cw[:, :gw], cbias[:, :gw])
    bm_ref[...] = _causal_conv_silu(bbuf, b_ref[...], cw[:, gw:gw + ns], cbias[:, gw:gw + ns]).astype(BF16)
    cm_ref[...] = _causal_conv_silu(cbuf, c_ref[...], cw[:, gw + ns:], cbias[:, gw + ns:]).astype(BF16)

    _ssd_head_rows(dttn_ref[...], hp_ref[...], rows_ref)
    return acst


def _ssd_group_main(acst, z_ref, cp_ref, o_ref, ss_ref, state_ref, wide_ref, xs_ref, bm_ref, cm_ref):
    l = xs_ref.shape[0]
    sub = LANES
    nsub = l // sub
    hw = SSD_HEAD_DIM
    gw = SSD_GROUP_W
    n_a = SSD_HPG * sub + gw
    cp = cp_ref[...]
    xs, bm16, cm16 = xs_ref[...], bm_ref[...], cm_ref[...]
    acs_blk = wide_ref[:, :SSD_HPG * sub]
    acs64 = wide_ref[:, SSD_HPG * sub:n_a]
    dt64 = wide_ref[:, n_a:]
    last64 = acs64[l - 1:l, :]

    xdt = xs * dt64
    xdt16 = xdt.astype(BF16)
    st = state_ref[...]
    y = _dot(cm16, st.astype(BF16)) * jnp.exp2(acs64)
    state_ref[...] = st * jnp.exp2(last64) + _dot_tn(bm16, (xdt * jnp.exp2(last64 - acs64)).astype(BF16))
    y = y + cp[SSD_CONV + 1:SSD_CONV + 2, :gw] * xs

    cb = _dot_nt(cm16, bm16)
    diag_mask = _causal(sub)
    cb_blk = [[cb[r * sub:(r + 1) * sub, c * sub:(c + 1) * sub] if c < r else
               jnp.where(diag_mask, cb[r * sub:(r + 1) * sub, c * sub:(c + 1) * sub], 0.0)
               for c in range(r + 1)] for r in range(nsub)]
    lane = lax.broadcasted_iota(jnp.int32, (l, sub), 1)
    pair_out = []
    for p in range(SSD_HPG // 2):
        xp = xdt16[:, p * sub:(p + 1) * sub]
        rhs = [jnp.where(lane < hw, xp, jnp.zeros_like(xp)), jnp.where(lane >= hw, xp, jnp.zeros_like(xp))]
        rows = []
        for r in range(nsub):
            lhs_parts, rhs_parts = [], []
            for h in range(2):
                j = 2 * p + h
                col = acs_blk[r * sub:(r + 1) * sub, j * sub:(j + 1) * sub]
                for c in range(r + 1):
                    seg = col - acst[j:j + 1, c * sub:(c + 1) * sub]
                    if c == r:
                        seg = jnp.minimum(seg, 0.0)
                    lhs_parts.append((cb_blk[r][c] * jnp.exp2(seg)).astype(BF16))
                rhs_parts.append(rhs[h][:(r + 1) * sub, :])
            rows.append(_dot(jnp.concatenate(lhs_parts, axis=1), jnp.concatenate(rhs_parts, axis=0)))
        pair_out.append(jnp.concatenate(rows, axis=0))
    y = y + jnp.concatenate(pair_out, axis=1)
    y = y * _silu(z_ref[...])
    ss_ref[...] = jnp.broadcast_to(jnp.sum(y * y, axis=1, keepdims=True), (l, sub))
    o_ref[...] = (y * cp[SSD_CONV + 2:SSD_CONV + 3, :gw]).astype(o_ref.dtype)


def _head_expand(lanes_per_head, n_heads):
    return jnp.repeat(jnp.eye(n_heads, dtype=F32), lanes_per_head, axis=1)


def ssd_core(proj, dt_t, conv_w, conv_b, dt_bias, a_log, d_skip, norm_g, *, batch, seq):
    l = min(SSD_CHUNK, seq)
    nc = seq // l
    g_, j_ = SSD_GROUPS, SSD_HPG
    inner = g_ * SSD_GROUP_W
    nx = inner // SSD_GROUP_W
    nb = 2 * inner // SSD_STATE
    expand64 = _head_expand(SSD_HEAD_DIM, j_)
    r16 = jnp.concatenate([
        jnp.concatenate([_head_expand(LANES, j_), expand64, jnp.zeros_like(expand64)], axis=1),
        jnp.concatenate([jnp.zeros((j_, j_ * LANES + SSD_GROUP_W), F32), expand64], axis=1)], axis=0)
    r = jnp.concatenate([r16, r16, r16], axis=0).astype(BF16)
    bc_w = g_ * SSD_STATE

    def per_group(p, rows):
        parts = [p[:, :inner].reshape(rows, g_, SSD_GROUP_W), p[:, inner:inner + bc_w].reshape(rows, g_, SSD_STATE),
                 p[:, inner + bc_w:].reshape(rows, g_, SSD_STATE)]
        return jnp.concatenate(parts, axis=2).transpose(1, 0, 2)

    cp_w = SSD_GROUP_W + 2 * SSD_STATE
    x_only = ((0, 0), (0, 0), (0, cp_w - SSD_GROUP_W))
    dskip = jnp.pad(jnp.repeat(d_skip.reshape(g_, 1, j_), SSD_HEAD_DIM, axis=2), x_only)
    gain = jnp.pad(norm_g.reshape(g_, 1, SSD_GROUP_W), x_only)
    conv_p = jnp.concatenate([per_group(conv_w, SSD_CONV), per_group(conv_b.reshape(1, -1), 1), dskip, gain,
                              jnp.zeros((g_, 8 - SSD_CONV - 3, cp_w), F32)], axis=1).astype(F32)
    head_p = jnp.pad(jnp.stack([dt_bias.reshape(g_, j_), a_log.reshape(g_, j_)], axis=2),
                     ((0, 0), (0, 0), (0, LANES - 2))).astype(F32)
    gps = SSD_GROUPS_PER_STEP
    xw, bw = gps * SSD_GROUP_W, gps * SSD_STATE
    in_specs = [
        pl.BlockSpec((1, l, xw), lambda b, g, c: (b, c, g)),
        pl.BlockSpec((1, l, xw), lambda b, g, c: (b, c, nx // gps + g)),
        pl.BlockSpec((1, l, bw), lambda b, g, c: (b, c, nb // gps + g)),
        pl.BlockSpec((1, l, bw), lambda b, g, c: (b, c, (nb + g_) // gps + g)),
        pl.BlockSpec((gps * j_, l), lambda b, g, c: (g, b * nc + c)),
        pl.BlockSpec((gps * j_, l), lambda b, g, c: (g, b * nc + jnp.minimum(c + 1, nc - 1))),
        pl.BlockSpec((gps, 8, cp_w), lambda b, g, c: (g, 0, 0)),
        pl.BlockSpec((gps, 8, LANES), lambda b, g, c: (g, 0, 0)),
        pl.BlockSpec(r.shape, lambda b, g, c: (0, 0)),
    ]
    return pl.pallas_call(
        _ssd_kernel,
        out_shape=[jax.ShapeDtypeStruct((batch, seq, inner), BF16),
                   jax.ShapeDtypeStruct((batch, seq, g_ * LANES), F32)],
        grid=(batch, g_ // gps, nc), in_specs=in_specs,
        out_specs=[pl.BlockSpec((1, l, xw), lambda b, g, c: (b, c, g)),
                   pl.BlockSpec((1, l, gps * LANES), lambda b, g, c: (b, c, g))],
        scratch_shapes=[pltpu.VMEM((gps, CONV_HALO + l, SSD_GROUP_W), F32),
                        pltpu.VMEM((gps, CONV_HALO + l, SSD_STATE), F32),
                        pltpu.VMEM((gps, CONV_HALO + l, SSD_STATE), F32),
                        pltpu.VMEM((gps, SSD_STATE, SSD_GROUP_W), F32),
                        pltpu.VMEM((gps, 2 * j_, l), F32),
                        pltpu.VMEM((gps, l, r.shape[1]), F32),
                        pltpu.VMEM((gps, l, SSD_GROUP_W), F32),
                        pltpu.VMEM((gps, l, SSD_STATE), BF16),
                        pltpu.VMEM((gps, l, SSD_STATE), BF16)],
        compiler_params=_params("parallel", "parallel", "arbitrary"), name="ssd_core",
    )(proj, proj, proj, proj, dt_t, dt_t, conv_p, head_p, r)


def _mlstm_kernel(q_ref, k_ref, v_ref, o_ref, gc_ref, gr_ref, gbc_ref, gbr_ref, hg_ref,
                  out_ref, c_ref, n_ref, m_ref):
    @pl.when(pl.program_id(1) == 0)
    def _():
        c_ref[...] = jnp.zeros_like(c_ref)
        n_ref[...] = jnp.zeros_like(n_ref)
        m_ref[...] = jnp.zeros_like(m_ref)

    for hi in range(MLSTM_HEADS):
        qk_cols, v_cols = pl.ds(hi * MLSTM_DQK, MLSTM_DQK), pl.ds(hi * MLSTM_DV, MLSTM_DV)
        _mlstm_head(q_ref.at[0, :, qk_cols], k_ref.at[0, :, qk_cols], v_ref.at[0, :, v_cols], o_ref.at[0, :, v_cols],
                    gc_ref.at[0, hi], gr_ref.at[0, hi], gbc_ref.at[hi], gbr_ref.at[hi], hg_ref.at[:, v_cols],
                    out_ref.at[0, :, v_cols], c_ref.at[hi], n_ref.at[hi], m_ref.at[hi])


def _mlstm_head(q_ref, k_ref, v_ref, o_ref, gc_ref, gr_ref, gbc_ref, gbr_ref, hg_ref,
                out_ref, c_ref, n_ref, m_ref):
    l = q_ref.shape[0]
    q = q_ref[...] * (MLSTM_DQK ** -0.5)
    k = k_ref[...]
    v16 = v_ref[...].astype(BF16)
    q16 = q.astype(BF16)
    gcol = gc_ref[...] + gbc_ref[...]
    grow = gr_ref[...] + gbr_ref[...]
    i_col, lf_col = gcol[:, 0:1], _log_sigmoid(gcol[:, 1:2])
    i_row, lf_row = grow[0:1, :], _log_sigmoid(grow[1:2, :])
    bcum = _cumsum_rows(lf_col)
    bcum_row = _cumsum_cols(lf_row)
    m_st = m_ref[0:1, 0:1]

    causal = _causal(l)
    intra = jnp.where(causal, bcum - bcum_row + i_row, -jnp.inf)
    inter = bcum + m_st
    m_t = jnp.maximum(inter, jnp.max(intra, axis=1, keepdims=True))
    w = jnp.exp(intra - m_t)
    scale_inter = jnp.exp(inter - m_t)
    qk = _dot_nt(q16, k.astype(BF16)) * w
    num = _dot(qk.astype(BF16), v16) + scale_inter * _dot(q16, c_ref[...].astype(BF16))
    qn = jnp.sum(q * n_ref[...], axis=1, keepdims=True)
    den = jnp.sum(qk, axis=1, keepdims=True) + scale_inter * qn
    h = num / jnp.maximum(jnp.abs(den), jnp.exp(-m_t))

    b_last = bcum[l - 1:l, :]
    tail = b_last - bcum + i_col
    m_new = jnp.maximum(b_last + m_st, jnp.max(tail, axis=0, keepdims=True))
    carry_scale = jnp.exp(b_last + m_st - m_new)
    wk = k * jnp.exp(tail - m_new)
    c_ref[...] = carry_scale * c_ref[...] + _dot_tn(wk.astype(BF16), v16)
    n_ref[...] = carry_scale * n_ref[...] + jnp.sum(wk, axis=0, keepdims=True)
    m_ref[...] = jnp.broadcast_to(m_new, m_ref.shape)

    hn = _rms_scale(h, hg_ref[...])
    out_ref[...] = (_sigmoid(o_ref[...]) * hn).astype(out_ref.dtype)


def mlstm_core(proj, gates, gate_b, head_g, *, batch, seq):
    l = MLSTM_CHUNK
    h_ = MLSTM_HEADS
    qk_w = h_ * MLSTM_DQK
    v_w = h_ * MLSTM_DV
    gsplit = jnp.stack([gates[..., :h_], gates[..., h_:2 * h_]], axis=-1)
    gcol = gsplit.transpose(0, 2, 1, 3)
    grow = gsplit.transpose(0, 2, 3, 1)
    gb = jnp.stack([gate_b[:h_], gate_b[h_:]], axis=-1).astype(F32)
    return pl.pallas_call(
        _mlstm_kernel, out_shape=jax.ShapeDtypeStruct((batch, seq, v_w), BF16),
        grid=(batch, seq // l),
        in_specs=[pl.BlockSpec((1, l, qk_w), lambda b, c: (b, c, 0)),
                  pl.BlockSpec((1, l, qk_w), lambda b, c: (b, c, 1)),
                  pl.BlockSpec((1, l, v_w), lambda b, c: (b, c, 2 * qk_w // v_w)),
                  pl.BlockSpec((1, l, v_w), lambda b, c: (b, c, 2 * qk_w // v_w + 1)),
                  pl.BlockSpec((1, h_, l, 2), lambda b, c: (b, 0, c, 0)),
                  pl.BlockSpec((1, h_, 2, l), lambda b, c: (b, 0, 0, c)),
                  pl.BlockSpec((h_, 1, 2), lambda b, c: (0, 0, 0)),
                  pl.BlockSpec((h_, 2, 1), lambda b, c: (0, 0, 0)),
                  pl.BlockSpec((1, v_w), lambda b, c: (0, 0))],
        out_specs=pl.BlockSpec((1, l, v_w), lambda b, c: (b, c, 0)),
        scratch_shapes=[pltpu.VMEM((h_, MLSTM_DQK, MLSTM_DV), F32),
                        pltpu.VMEM((h_, 1, MLSTM_DQK), F32),
                        pltpu.VMEM((h_, 8, LANES), F32)],
        compiler_params=_params("parallel", "arbitrary"), name="mlstm_core",
    )(proj, proj, proj, proj, gcol, grow, gb.reshape(h_, 1, 2), gb.reshape(h_, 2, 1),
      head_g.reshape(1, v_w).astype(F32))


def _s5_discretize_kernel(logdt_ref, are_ref, aim_ref, bre_ref, bim_ref,
                          abre_ref, abim_ref, bbre_ref, bbim_ref):
    ar, ai = are_ref[...], aim_ref[...]
    dt = jnp.exp(logdt_ref[...])
    mag = jnp.exp(ar * dt)
    abar_re, abar_im = mag * jnp.cos(ai * dt), mag * jnp.sin(ai * dt)
    den = ar * ar + ai * ai
    zoh_re = ((abar_re - 1.0) * ar + abar_im * ai) / den
    zoh_im = (abar_im * ar - (abar_re - 1.0) * ai) / den
    abre_ref[...] = abar_re
    abim_ref[...] = abar_im
    for c in range(S5_GROUP):
        br, bi = bre_ref[c], bim_ref[c]
        bbre_ref[c] = zoh_re * br - zoh_im * bi
        bbim_ref[c] = zoh_re * bi + zoh_im * br


def s5_discretize(log_dt, a_re, a_im, b_re, b_im):
    g_, p_ = a_re.shape
    sds = jax.ShapeDtypeStruct
    return pl.pallas_call(
        _s5_discretize_kernel,
        out_shape=[sds((g_, p_), F32), sds((g_, p_), F32),
                   sds((S5_GROUP, g_, p_), F32), sds((S5_GROUP, g_, p_), F32)],
        name="s5_discretize",
    )(log_dt.reshape(g_, 1).astype(F32), a_re.astype(F32), a_im.astype(F32),
      b_re.astype(F32).transpose(2, 0, 1), b_im.astype(F32).transpose(2, 0, 1))


def _s5_kernel(u_ref, bw_ref, cw_ref, a_ref, d_ref, y_ref, st_ref, bu_ref, *, batch):
    ts = u_ref.shape[0]
    half = S5_BLOCK_STATE
    nblk = S5_BLOCKS_PER_STEP

    @pl.when(pl.program_id(1) == 0)
    def _():
        st_ref[...] = jnp.zeros_like(st_ref)

    def lanes(n):
        return slice(n * LANES, (n + 1) * LANES)

    def project_in(n):
        u = u_ref[:, :, lanes(n)].reshape(ts * batch, LANES)
        bu_ref[n] = _dot(u.astype(BF16), bw_ref[n])

    def scan(n):
        a = a_ref[n]
        a_re = jnp.broadcast_to(a[:, :half], (batch, half))
        a_im = jnp.broadcast_to(a[:, half:], (batch, half))
        s_re, s_im = st_ref[n, :, 0:half], st_ref[n, :, half:2 * half]
        for t in range(ts):
            rows = slice(t * batch, (t + 1) * batch)
            s_re, s_im = (a_re * s_re - a_im * s_im + bu_ref[n, rows, 0:half],
                          a_re * s_im + a_im * s_re + bu_ref[n, rows, half:2 * half])
            bu_ref[n, rows, 0:half] = s_re
            bu_ref[n, rows, half:2 * half] = s_im
        st_ref[n, :, 0:half] = s_re
        st_ref[n, :, half:2 * half] = s_im

    def project_out(n):
        u = u_ref[:, :, lanes(n)].reshape(ts * batch, LANES)
        y = _dot(bu_ref[n].astype(BF16), cw_ref[n]) + d_ref[:, lanes(n)] * u
        y_ref[:, :, lanes(n)] = jax.nn.gelu(y).astype(y_ref.dtype).reshape(ts, batch, LANES)

    project_in(0)
    for n in range(nblk):
        if n + 1 < nblk:
            project_in(n + 1)
        scan(n)
        if n > 0:
            project_out(n - 1)
    project_out(nblk - 1)


def s5_core(u_tm, bw, cw, a_vec, d_skip, *, batch, seq):
    width = u_tm.shape[2]
    nblk = width // LANES
    ts = min(S5_TIME_BLOCK, seq)
    per = S5_BLOCKS_PER_STEP
    return pl.pallas_call(
        functools.partial(_s5_kernel, batch=batch),
        out_shape=jax.ShapeDtypeStruct((seq, batch, width), BF16),
        grid=(nblk // per, seq // ts),
        in_specs=[pl.BlockSpec((ts, batch, per * LANES), lambda j, i: (i, 0, j)),
                  pl.BlockSpec((per, LANES, 2 * S5_BLOCK_STATE), lambda j, i: (j, 0, 0)),
                  pl.BlockSpec((per, 2 * S5_BLOCK_STATE, LANES), lambda j, i: (j, 0, 0)),
                  pl.BlockSpec((per, 1, 2 * S5_BLOCK_STATE), lambda j, i: (j, 0, 0)),
                  pl.BlockSpec((1, per * LANES), lambda j, i: (0, j))],
        out_specs=pl.BlockSpec((ts, batch, per * LANES), lambda j, i: (i, 0, j)),
        scratch_shapes=[pltpu.VMEM((per, batch, 2 * S5_BLOCK_STATE), F32),
                        pltpu.VMEM((per, ts * batch, 2 * S5_BLOCK_STATE), F32)],
        compiler_params=_params("parallel", "arbitrary"), name="s5_core",
    )(u_tm, bw, cw, a_vec, d_skip.reshape(1, width).astype(F32))


def _s5_block_weights(abar_re, abar_im, bbar_re, bbar_im, c_re, c_im):
    g_, p_ = abar_re.shape
    gb = S5_GROUPS_PER_BLOCK
    nblk = g_ // gb
    eye = jnp.eye(gb, dtype=F32)

    def in_blocks(bbar):
        b = bbar.reshape(S5_GROUP, nblk, gb, p_).transpose(1, 2, 0, 3)
        return jnp.einsum("ngcp,gh->ngchp", b, eye).reshape(nblk, gb * S5_GROUP, gb * p_)

    def out_blocks(cmat):
        c = cmat.reshape(nblk, gb, S5_GROUP, p_).transpose(0, 1, 3, 2)
        return jnp.einsum("ngpc,gh->ngphc", c, eye).reshape(nblk, gb * p_, gb * S5_GROUP)

    bw = jnp.concatenate([in_blocks(bbar_re), in_blocks(bbar_im)], axis=2).astype(BF16)
    cw = jnp.concatenate([out_blocks(c_re.astype(F32)), out_blocks(-c_im.astype(F32))], axis=1).astype(BF16)
    a_vec = jnp.concatenate([abar_re.reshape(nblk, 1, gb * p_), abar_im.reshape(nblk, 1, gb * p_)], axis=2)
    return bw, cw, a_vec


SSD_MAIN_COLS = 2 * SSD_GROUPS * SSD_GROUP_W + 2 * SSD_GROUPS * SSD_STATE
MLSTM_MAIN_COLS = 2 * MLSTM_HEADS * (MLSTM_DQK + MLSTM_DV)
TOKEN_TILE = 1024
COL_TILE = 1024
IN_PROJ_COL_TILE = 2048


def _in_proj_col_tile(g):
    return IN_PROJ_COL_TILE if g is None else COL_TILE
MLP_TOKEN_TILE = 1024
MLP_FF_TILE = 512


def _small_head(w_in, idx, main, transposed):
    w = jnp.pad(w_in[idx, :, main:], ((0, 0), (0, LANES - (w_in.shape[2] - main)))).astype(BF16)
    return w.T if transposed else w


def _ssd_layer(h, xin, g, w_in, w_in16, w_out16, idx, conv_w, conv_b, dt_bias, a_log, d_skip, norm_g, *, batch, seq):
    inner = SSD_GROUPS * SSD_GROUP_W
    main = SSD_MAIN_COLS
    proj, dt_t = in_proj(xin, g, w_in16, idx, n=main, w_small=_small_head(w_in, idx, main, True), small="cols",
                         tm=TOKEN_TILE, tn=_in_proj_col_tile(g), name="ssd_in_proj")
    y16, ss = ssd_core(proj.reshape(batch, seq, main), dt_t, conv_w, conv_b, dt_bias, a_log, d_skip, norm_g,
                       batch=batch, seq=seq)
    return rms_matmul_residual(y16.reshape(batch * seq, inner), ss.reshape(batch * seq, -1), w_out16, idx, h,
                               tm=TOKEN_TILE // 2, tn=w_out16.shape[2], name="ssd_out_proj")


def _mlstm_layer(h, xin, g, w_in, w_in16, w_out16, idx, gate_b, head_g, *, batch, seq):
    main = MLSTM_MAIN_COLS
    proj, gates = in_proj(xin, g, w_in16, idx, n=main, w_small=_small_head(w_in, idx, main, False), small="rows",
                          tm=TOKEN_TILE, tn=_in_proj_col_tile(g), name="mlstm_in_proj")
    hs = mlstm_core(proj.reshape(batch, seq, main),
                    gates[:, :2 * MLSTM_HEADS].reshape(batch, seq, 2 * MLSTM_HEADS),
                    gate_b, head_g, batch=batch, seq=seq)
    return matmul_residual(hs.reshape(batch * seq, -1), w_out16, idx, h,
                           tm=TOKEN_TILE // 2, tn=w_out16.shape[2], name="mlstm_out_proj")


def _s5_layer(h, xin, g, w_in16, w_out16, idx, b_re, b_im, c_re, c_im, d_skip, log_dt, a_re, a_im, *, batch, seq):
    width = w_in16.shape[2]
    tm = min(TOKEN_TILE, seq)
    nt = seq // tm
    tn = _in_proj_col_tile(g)
    ncol = width // tn
    u_tm = in_proj(xin, g, w_in16, idx, n=width, tm=tm, tn=tn,
                   out_index=lambda i, j: (i % nt, (i // nt) * ncol + j),
                   out_shape=(seq, batch * width), name="s5_in_proj")
    abar_re, abar_im, bbar_re, bbar_im = s5_discretize(log_dt, a_re, a_im, b_re, b_im)
    bw, cw, a_vec = _s5_block_weights(abar_re, abar_im, bbar_re, bbar_im, c_re, c_im)
    y_tm = s5_core(u_tm.reshape(seq, batch, width), bw, cw, a_vec, d_skip, batch=batch, seq=seq)
    return glu_matmul_residual(y_tm.reshape(seq, batch * width), w_out16, idx, h,
                               batch=batch, tm=tm, tn=COL_TILE, name="s5_out_proj")


def kernel(x, norm_mix_g, norm_mlp_g, ssd_w_in, ssd_conv_w, ssd_conv_b, ssd_dt_bias, ssd_a_log, ssd_d, ssd_norm_g, ssd_w_out, mlstm_w_in, mlstm_gate_b, mlstm_head_g, mlstm_w_out, s5_w_in, s5_b_re, s5_b_im, s5_c_re, s5_c_im, s5_d, s5_log_dt, s5_a_re, s5_a_im, s5_w_out, mlp_w1, mlp_w2, final_norm_g):
    batch, seq, d = x.shape
    depth = norm_mix_g.shape[0]
    ssd_in16, ssd_out16 = ssd_w_in.astype(BF16), ssd_w_out.astype(BF16)
    mlstm_in16, mlstm_out16 = mlstm_w_in.astype(BF16), mlstm_w_out.astype(BF16)
    s5_in16, s5_out16 = s5_w_in.astype(BF16), s5_w_out.astype(BF16)
    mlp_w1_16, mlp_w2_16 = mlp_w1.astype(BF16), mlp_w2.astype(BF16)
    h = x.reshape(batch * seq, d)
    xin, g = h, norm_mix_g[0]
    for layer in range(depth):
        kind, idx = layer % 3, layer // 3
        if kind == 0:
            h = _ssd_layer(h, xin, g, ssd_w_in, ssd_in16, ssd_out16, idx, ssd_conv_w[idx], ssd_conv_b[idx],
                           ssd_dt_bias[idx], ssd_a_log[idx], ssd_d[idx], ssd_norm_g[idx], batch=batch, seq=seq)
        elif kind == 1:
            h = _mlstm_layer(h, xin, g, mlstm_w_in, mlstm_in16, mlstm_out16, idx, mlstm_gate_b[idx],
                             mlstm_head_g[idx], batch=batch, seq=seq)
        else:
            h = _s5_layer(h, xin, g, s5_in16, s5_out16, idx, s5_b_re[idx], s5_b_im[idx], s5_c_re[idx], s5_c_im[idx],
                          s5_d[idx], s5_log_dt[idx], s5_a_re[idx], s5_a_im[idx], batch=batch, seq=seq)
        final = layer == depth - 1
        out = mlp_block(h, norm_mlp_g[layer], mlp_w1_16, mlp_w2_16, layer,
                        final_norm_g if final else norm_mix_g[layer + 1], final=final,
                        tm=min(MLP_TOKEN_TILE, batch * seq), tf=MLP_FF_TILE, name=f"mlp_{layer}")
        if final:
            h = out
        else:
            h, xin = out
            g = None
    return h.reshape(batch, seq, d)
```

```python
import functools
import math

import jax
import jax.numpy as jnp
from jax import lax
from jax.experimental import pallas as pl
from jax.experimental.pallas import tpu as pltpu

F32 = jnp.float32
BF16 = jnp.bfloat16

NORM_EPS = 1e-5
LOG2E = math.log2(math.e)
LANES = 128
VMEM_LIMIT_BYTES = 56 * 1024 * 1024

SSD_HEAD_DIM = 64
SSD_GROUPS = 8
SSD_HPG = 8
SSD_STATE = 128
SSD_CONV = 4
SSD_GROUP_W = SSD_HPG * SSD_HEAD_DIM
SSD_CHUNK = 256
SSD_GROUPS_PER_STEP = 8
CONV_HALO = 8

MLSTM_HEADS = 4
MLSTM_DQK = 256
MLSTM_DV = 512
MLSTM_CHUNK = 256

S5_GROUP = 16
S5_STATE = 64
S5_GROUPS_PER_BLOCK = LANES // S5_GROUP
S5_BLOCK_STATE = S5_GROUPS_PER_BLOCK * S5_STATE
S5_TIME_BLOCK = 64
S5_BLOCKS_PER_STEP = 4


def _params(*semantics):
    return pltpu.CompilerParams(dimension_semantics=semantics, vmem_limit_bytes=VMEM_LIMIT_BYTES)


def _dot(a, b):
    return jnp.dot(a, b, preferred_element_type=F32)


def _dot_nt(a, b):
    return lax.dot_general(a, b, (((1,), (1,)), ((), ())), preferred_element_type=F32)


def _dot_tn(a, b):
    return lax.dot_general(a, b, (((0,), (0,)), ((), ())), preferred_element_type=F32)


def _split3(x):
    hi = x.astype(BF16)
    r1 = x - hi.astype(F32)
    mid = r1.astype(BF16)
    lo = (r1 - mid.astype(F32)).astype(BF16)
    return hi, mid, lo


def _tri(n, upper):
    r = lax.broadcasted_iota(jnp.int32, (n, n), 0)
    c = lax.broadcasted_iota(jnp.int32, (n, n), 1)
    keep = (r <= c) if upper else (c <= r)
    return jnp.where(keep, 1.0, 0.0).astype(BF16)


def _cumsum_rows(x):
    t = _tri(x.shape[0], upper=False)
    hi, mid, lo = _split3(x)
    return _dot(t, hi) + _dot(t, mid) + _dot(t, lo)


def _cumsum_cols(x):
    t = _tri(x.shape[1], upper=True)
    hi, mid, lo = _split3(x)
    return _dot(hi, t) + _dot(mid, t) + _dot(lo, t)


def _causal(n):
    r = lax.broadcasted_iota(jnp.int32, (n, n), 0)
    c = lax.broadcasted_iota(jnp.int32, (n, n), 1)
    return c <= r


def _softplus(x):
    return jnp.maximum(x, 0.0) + jnp.log1p(jnp.exp(-jnp.abs(x)))


def _sigmoid(x):
    return 1.0 / (1.0 + jnp.exp(-x))


def _silu(x):
    return x * _sigmoid(x)


def _log_sigmoid(x):
    return -_softplus(-x)


def _rms_scale(x, g):
    ms = jnp.mean(x * x, axis=-1, keepdims=True)
    return x * lax.rsqrt(ms + NORM_EPS) * g


def _in_proj_kernel(*refs, normalize, small):
    it = iter(refs)
    x_ref = next(it)
    g_ref = next(it) if normalize else None
    w_ref = next(it)
    ws_ref = next(it) if small else None
    o_ref = next(it)
    os_ref = next(it) if small else None
    xn_ref = next(it) if normalize else x_ref

    if normalize or small:
        @pl.when(pl.program_id(1) == 0)
        def _():
            if normalize:
                xn_ref[...] = _rms_scale(x_ref[...], g_ref[...]).astype(BF16)
            if small == "rows":
                os_ref[...] = _dot(xn_ref[...], ws_ref[...])
            elif small == "cols":
                os_ref[...] = _dot_nt(ws_ref[...], xn_ref[...])

    o_ref[...] = _dot(xn_ref[...], w_ref[...])


def in_proj(x, g, w, layer, *, n, w_small=None, small=None, tm, tn, out_index=None, out_shape=None, name):
    n_tok, k = x.shape
    assert n_tok % tm == 0 and n % tn == 0
    normalize = g is not None
    out_index = out_index or (lambda i, j: (i, j))
    in_specs = [pl.BlockSpec((tm, k), lambda i, j: (i, 0))]
    args = [x]
    if normalize:
        in_specs.append(pl.BlockSpec((1, k), lambda i, j: (0, 0)))
        args.append(g.reshape(1, k).astype(F32))
    in_specs.append(pl.BlockSpec((None, k, tn), lambda i, j: (layer, 0, j)))
    args.append(w)
    out_shapes = [jax.ShapeDtypeStruct(out_shape or (n_tok, n), F32)]
    out_specs = [pl.BlockSpec((tm, tn), out_index)]
    if small:
        in_specs.append(pl.BlockSpec(w_small.shape, lambda i, j: (0, 0)))
        args.append(w_small)
        if small == "cols":
            out_shapes.append(jax.ShapeDtypeStruct((LANES, n_tok), F32))
            out_specs.append(pl.BlockSpec((LANES, tm), lambda i, j: (0, i)))
        else:
            out_shapes.append(jax.ShapeDtypeStruct((n_tok, LANES), F32))
            out_specs.append(pl.BlockSpec((tm, LANES), lambda i, j: (i, 0)))
    outs = pl.pallas_call(
        functools.partial(_in_proj_kernel, normalize=normalize, small=small),
        out_shape=out_shapes, grid=(n_tok // tm, n // tn), in_specs=in_specs, out_specs=out_specs,
        scratch_shapes=[pltpu.VMEM((tm, k), BF16)] if normalize else [],
        compiler_params=_params("parallel", "arbitrary"), name=name)(*args)
    return outs if small else outs[0]


def _matmul_res_kernel(a_ref, w_ref, res_ref, o_ref):
    o_ref[...] = res_ref[...] + _dot(a_ref[...], w_ref[...])


def matmul_residual(a, w, layer, res, *, tm, tn, name):
    t, k = a.shape
    n = w.shape[2]
    return pl.pallas_call(
        _matmul_res_kernel, out_shape=jax.ShapeDtypeStruct((t, n), F32),
        grid=(t // tm, n // tn),
        in_specs=[pl.BlockSpec((tm, k), lambda i, j: (i, 0)),
                  pl.BlockSpec((None, k, tn), lambda i, j: (layer, 0, j)),
                  pl.BlockSpec((tm, tn), lambda i, j: (i, j))],
        out_specs=pl.BlockSpec((tm, tn), lambda i, j: (i, j)),
        compiler_params=_params("parallel", "arbitrary"), name=name)(a, w, res)


def _rms_matmul_res_kernel(a_ref, ss_ref, w_ref, res_ref, o_ref):
    k = a_ref.shape[1]
    ss = ss_ref[...]
    tot = ss[:, :LANES]
    for q in range(1, ss.shape[1] // LANES):
        tot = tot + ss[:, q * LANES:(q + 1) * LANES]
    inv = lax.rsqrt(tot * (1.0 / k) + NORM_EPS)
    y = _dot(a_ref[...], w_ref[...])
    o_ref[...] = res_ref[...] + jnp.concatenate([inv] * (y.shape[1] // LANES), axis=1) * y


def rms_matmul_residual(a, ss, w, layer, res, *, tm, tn, name):
    t, k = a.shape
    n = w.shape[2]
    w_mode = dict(pipeline_mode=pl.Buffered(1)) if tn == n else {}
    return pl.pallas_call(
        _rms_matmul_res_kernel, out_shape=jax.ShapeDtypeStruct((t, n), F32),
        grid=(t // tm, n // tn),
        in_specs=[pl.BlockSpec((tm, k), lambda i, j: (i, 0)),
                  pl.BlockSpec((tm, ss.shape[1]), lambda i, j: (i, 0)),
                  pl.BlockSpec((None, k, tn), lambda i, j: (layer, 0, j), **w_mode),
                  pl.BlockSpec((tm, tn), lambda i, j: (i, j))],
        out_specs=pl.BlockSpec((tm, tn), lambda i, j: (i, j)),
        compiler_params=_params("parallel", "arbitrary"), name=name)(a, ss, w, res)


def _glu_res_kernel(a_ref, wv_ref, wg_ref, res_ref, o_ref):
    a = a_ref[...]
    val = _dot(a, wv_ref[...])
    gate = _dot(a, wg_ref[...])
    o_ref[...] = res_ref[...] + val * _sigmoid(gate)


def glu_matmul_residual(a_tm, w, layer, res, *, batch, tm, tn, name):
    s = a_tm.shape[0]
    k = w.shape[1]
    n = w.shape[2] // 2
    nt = s // tm
    w_mode = dict(pipeline_mode=pl.Buffered(1)) if tn == n else {}
    return pl.pallas_call(
        _glu_res_kernel, out_shape=jax.ShapeDtypeStruct((batch * s, n), F32),
        grid=(batch * nt, n // tn),
        in_specs=[pl.BlockSpec((tm, k), lambda i, j: (i % nt, i // nt)),
                  pl.BlockSpec((None, k, tn), lambda i, j: (layer, 0, j), **w_mode),
                  pl.BlockSpec((None, k, tn), lambda i, j: (layer, 0, j + n // tn), **w_mode),
                  pl.BlockSpec((tm, tn), lambda i, j: (i, j))],
        out_specs=pl.BlockSpec((tm, tn), lambda i, j: (i, j)),
        compiler_params=_params("parallel", "arbitrary"), name=name)(a_tm, w, w, res)


def _mlp_kernel(x_ref, g_ref, w1_ref, w2_ref, gn_ref, o_ref, *rest, final):
    xn_ref = rest[-1]
    f = pl.program_id(1)

    @pl.when(f == 0)
    def _():
        x = x_ref[...]
        xn_ref[...] = _rms_scale(x, g_ref[...]).astype(BF16)
        o_ref[...] = x

    h1 = jnp.maximum(_dot(xn_ref[...], w1_ref[...]), 0.0)
    o_ref[...] += _dot((h1 * h1).astype(BF16), w2_ref[...])

    @pl.when(f == pl.num_programs(1) - 1)
    def _():
        normed = _rms_scale(o_ref[...], gn_ref[...])
        if final:
            o_ref[...] = normed
        else:
            rest[0][...] = normed.astype(BF16)


def mlp_block(x, g, w1, w2, layer, g_next, *, final, tm, tf, name):
    t, d = x.shape
    dff = w1.shape[2]
    row_spec = pl.BlockSpec((tm, d), lambda i, f: (i, 0))
    out_shape = [jax.ShapeDtypeStruct((t, d), F32)] + ([] if final else [jax.ShapeDtypeStruct((t, d), BF16)])
    outs = pl.pallas_call(
        functools.partial(_mlp_kernel, final=final),
        out_shape=out_shape, grid=(t // tm, dff // tf),
        in_specs=[row_spec,
                  pl.BlockSpec((1, d), lambda i, f: (0, 0)),
                  pl.BlockSpec((None, d, tf), lambda i, f: (layer, 0, f)),
                  pl.BlockSpec((None, tf, d), lambda i, f: (layer, f, 0)),
                  pl.BlockSpec((1, d), lambda i, f: (0, 0))],
        out_specs=[row_spec] * len(out_shape),
        scratch_shapes=[pltpu.VMEM((tm, d), BF16)],
        compiler_params=_params("parallel", "arbitrary"), name=name,
    )(x, g.reshape(1, d), w1, w2, g_next.reshape(1, d))
    return outs[0] if final else outs


def _causal_conv_silu(buf_ref, cur, w, b):
    l = cur.shape[0]
    buf_ref[CONV_HALO:CONV_HALO + l, :] = cur
    acc = b + w[SSD_CONV - 1:SSD_CONV, :] * cur
    for k in range(SSD_CONV - 1):
        start = CONV_HALO - (SSD_CONV - 1) + k
        acc = acc + w[k:k + 1, :] * buf_ref[start:start + l, :]
    buf_ref[0:CONV_HALO, :] = cur[l - CONV_HALO:, :]
    return _silu(acc)


def _ssd_head_rows(dt_raw_t, head_p, rows_ref):
    dtt = _softplus(dt_raw_t + head_p[:, 0:1])
    rows_ref[0:SSD_HPG, :] = _cumsum_cols(dtt * (-LOG2E * jnp.exp(head_p[:, 1:2])))
    rows_ref[SSD_HPG:, :] = dtt


def _ssd_kernel(z_ref, x_ref, b_ref, c_ref, dtt_ref, dttn_ref, cp_ref, hp_ref, r_ref,
                o_ref, ss_ref, xbuf, bbuf, cbuf, state_ref, rows_ref, wide_ref, xs_ref, bm_ref, cm_ref):
    gw, ns = SSD_GROUP_W, SSD_STATE

    @pl.when(pl.program_id(2) == 0)
    def _():
        state_ref[...] = jnp.zeros_like(state_ref)
        for buf in (xbuf, bbuf, cbuf):
            buf[:, 0:CONV_HALO, :] = jnp.zeros((buf.shape[0], CONV_HALO, buf.shape[2]), F32)
        for gi in range(SSD_GROUPS_PER_STEP):
            _ssd_head_rows(dtt_ref[gi * SSD_HPG:(gi + 1) * SSD_HPG, :], hp_ref[gi], rows_ref.at[gi])

    def prepare(gi):
        x_cols, bc_cols = pl.ds(gi * gw, gw), pl.ds(gi * ns, ns)
        return _ssd_group_prepare(
            x_ref.at[0, :, x_cols], b_ref.at[0, :, bc_cols], c_ref.at[0, :, bc_cols],
            dttn_ref.at[pl.ds(gi * SSD_HPG, SSD_HPG), :], cp_ref.at[gi], hp_ref.at[gi], r_ref,
            xbuf.at[gi], bbuf.at[gi], cbuf.at[gi], rows_ref.at[gi], wide_ref.at[gi],
            xs_ref.at[gi], bm_ref.at[gi], cm_ref.at[gi])

    for gi in range(SSD_GROUPS_PER_STEP):
        acst = prepare(gi)
        x_cols = pl.ds(gi * gw, gw)
        _ssd_group_main(acst, z_ref.at[0, :, x_cols], cp_ref.at[gi], o_ref.at[0, :, x_cols],
                        ss_ref.at[0, :, pl.ds(gi * LANES, LANES)], state_ref.at[gi], wide_ref.at[gi],
                        xs_ref.at[gi], bm_ref.at[gi], cm_ref.at[gi])


def _ssd_group_prepare(x_ref, b_ref, c_ref, dttn_ref, cp_ref, hp_ref, r_ref,
                       xbuf, bbuf, cbuf, rows_ref, wide_ref, xs_ref, bm_ref, cm_ref):
    gw, ns = SSD_GROUP_W, SSD_STATE
    acst = rows_ref[0:SSD_HPG, :]
    wide_ref[...] = _dot_tn(jnp.concatenate(_split3(rows_ref[...]), axis=0), r_ref[...])

    cp = cp_ref[...]
    cw, cbias = cp[0:SSD_CONV, :], cp[SSD_CONV:SSD_CONV + 1, :]
    xs_ref[...] = _causal_conv_silu(xbuf, x_ref[...], cw[:, :gw], cbias[:, :gw])
    bm_ref[...] = _causal_conv_silu(bbuf, b_ref[...], cw[:, gw:gw + ns], cbias[:, gw:gw + ns]).astype(BF16)
    cm_ref[...] = _causal_conv_silu(cbuf, c_ref[...], cw[:, gw + ns:], cbias[:, gw + ns:]).astype(BF16)

    _ssd_head_rows(dttn_ref[...], hp_ref[...], rows_ref)
    return acst


def _ssd_group_main(acst, z_ref, cp_ref, o_ref, ss_ref, state_ref, wide_ref, xs_ref, bm_ref, cm_ref):
    l = xs_ref.shape[0]
    sub = LANES
    nsub = l // sub
    hw = SSD_HEAD_DIM
    gw = SSD_GROUP_W
    n_a = SSD_HPG * sub + gw
    cp = cp_ref[...]
    xs, bm16, cm16 = xs_ref[...], bm_ref[...], cm_ref[...]
    acs_blk = wide_ref[:, :SSD_HPG * sub]
    acs64 = wide_ref[:, SSD_HPG * sub:n_a]
    dt64 = wide_ref[:, n_a:]
    last64 = acs64[l - 1:l, :]

    xdt = xs * dt64
    xdt16 = xdt.astype(BF16)
    st = state_ref[...]
    y = _dot(cm16, st.astype(BF16)) * jnp.exp2(acs64)
    state_ref[...] = st * jnp.exp2(last64) + _dot_tn(bm16, (xdt * jnp.exp2(last64 - acs64)).astype(BF16))
    y = y + cp[SSD_CONV + 1:SSD_CONV + 2, :gw] * xs

    cb = _dot_nt(cm16, bm16)
    diag_mask = _causal(sub)
    cb_blk = [[cb[r * sub:(r + 1) * sub, c * sub:(c + 1) * sub] if c < r else
               jnp.where(diag_mask, cb[r * sub:(r + 1) * sub, c * sub:(c + 1) * sub], 0.0)
               for c in range(r + 1)] for r in range(nsub)]
    lane = lax.broadcasted_iota(jnp.int32, (l, sub), 1)
    pair_out = []
    for p in range(SSD_HPG // 2):
        xp = xdt16[:, p * sub:(p + 1) * sub]
        rhs = [jnp.where(lane < hw, xp, jnp.zeros_like(xp)), jnp.where(lane >= hw, xp, jnp.zeros_like(xp))]
        rows = []
        for r in range(nsub):
            lhs_parts, rhs_parts = [], []
            for h in range(2):
                j = 2 * p + h
                col = acs_blk[r * sub:(r + 1) * sub, j * sub:(j + 1) * sub]
                for c in range(r + 1):
                    seg = col - acst[j:j + 1, c * sub:(c + 1) * sub]
                    if c == r:
                        seg = jnp.minimum(seg, 0.0)
                    lhs_parts.append((cb_blk[r][c] * jnp.exp2(seg)).astype(BF16))
                rhs_parts.append(rhs[h][:(r + 1) * sub, :])
            rows.append(_dot(jnp.concatenate(lhs_parts, axis=1), jnp.concatenate(rhs_parts, axis=0)))
        pair_out.append(jnp.concatenate(rows, axis=0))
    y = y + jnp.concatenate(pair_out, axis=1)
    y = y * _silu(z_ref[...])
    ss_ref[...] = jnp.broadcast_to(jnp.sum(y * y, axis=1, keepdims=True), (l, sub))
    o_ref[...] = (y * cp[SSD_CONV + 2:SSD_CONV + 3, :gw]).astype(o_ref.dtype)


def _head_expand(lanes_per_head, n_heads):
    return jnp.repeat(jnp.eye(n_heads, dtype=F32), lanes_per_head, axis=1)


def ssd_core(proj, dt_t, conv_w, conv_b, dt_bias, a_log, d_skip, norm_g, *, batch, seq):
    l = min(SSD_CHUNK, seq)
    nc = seq // l
    g_, j_ = SSD_GROUPS, SSD_HPG
    inner = g_ * SSD_GROUP_W
    nx = inner // SSD_GROUP_W
    nb = 2 * inner // SSD_STATE
    expand64 = _head_expand(SSD_HEAD_DIM, j_)
    r16 = jnp.concatenate([
        jnp.concatenate([_head_expand(LANES, j_), expand64, jnp.zeros_like(expand64)], axis=1),
        jnp.concatenate([jnp.zeros((j_, j_ * LANES + SSD_GROUP_W), F32), expand64], axis=1)], axis=0)
    r = jnp.concatenate([r16, r16, r16], axis=0).astype(BF16)
    bc_w = g_ * SSD_STATE

    def per_group(p, rows):
        parts = [p[:, :inner].reshape(rows, g_, SSD_GROUP_W), p[:, inner:inner + bc_w].reshape(rows, g_, SSD_STATE),
                 p[:, inner + bc_w:].reshape(rows, g_, SSD_STATE)]
        return jnp.concatenate(parts, axis=2).transpose(1, 0, 2)

    cp_w = SSD_GROUP_W + 2 * SSD_STATE
    x_only = ((0, 0), (0, 0), (0, cp_w - SSD_GROUP_W))
    dskip = jnp.pad(jnp.repeat(d_skip.reshape(g_, 1, j_), SSD_HEAD_DIM, axis=2), x_only)
    gain = jnp.pad(norm_g.reshape(g_, 1, SSD_GROUP_W), x_only)
    conv_p = jnp.concatenate([per_group(conv_w, SSD_CONV), per_group(conv_b.reshape(1, -1), 1), dskip, gain,
                              jnp.zeros((g_, 8 - SSD_CONV - 3, cp_w), F32)], axis=1).astype(F32)
    head_p = jnp.pad(jnp.stack([dt_bias.reshape(g_, j_), a_log.reshape(g_, j_)], axis=2),
                     ((0, 0), (0, 0), (0, LANES - 2))).astype(F32)
    gps = SSD_GROUPS_PER_STEP
    xw, bw = gps * SSD_GROUP_W, gps * SSD_STATE
    in_specs = [
        pl.BlockSpec((1, l, xw), lambda b, g, c: (b, c, g)),
        pl.BlockSpec((1, l, xw), lambda b, g, c: (b, c, nx // gps + g)),
        pl.BlockSpec((1, l, bw), lambda b, g, c: (b, c, nb // gps + g)),
        pl.BlockSpec((1, l, bw), lambda b, g, c: (b, c, (nb + g_) // gps + g)),
        pl.BlockSpec((gps * j_, l), lambda b, g, c: (g, b * nc + c)),
        pl.BlockSpec((gps * j_, l), lambda b, g, c: (g, b * nc + jnp.minimum(c + 1, nc - 1))),
        pl.BlockSpec((gps, 8, cp_w), lambda b, g, c: (g, 0, 0)),
        pl.BlockSpec((gps, 8, LANES), lambda b, g, c: (g, 0, 0)),
        pl.BlockSpec(r.shape, lambda b, g, c: (0, 0)),
    ]
    return pl.pallas_call(
        _ssd_kernel,
        out_shape=[jax.ShapeDtypeStruct((batch, seq, inner), BF16),
                   jax.ShapeDtypeStruct((batch, seq, g_ * LANES), F32)],
        grid=(batch, g_ // gps, nc), in_specs=in_specs,
        out_specs=[pl.BlockSpec((1, l, xw), lambda b, g, c: (b, c, g)),
                   pl.BlockSpec((1, l, gps * LANES), lambda b, g, c: (b, c, g))],
        scratch_shapes=[pltpu.VMEM((gps, CONV_HALO + l, SSD_GROUP_W), F32),
                        pltpu.VMEM((gps, CONV_HALO + l, SSD_STATE), F32),
                        pltpu.VMEM((gps, CONV_HALO + l, SSD_STATE), F32),
                        pltpu.VMEM((gps, SSD_STATE, SSD_GROUP_W), F32),
                        pltpu.VMEM((gps, 2 * j_, l), F32),
                        pltpu.VMEM((gps, l, r.shape[1]), F32),
                        pltpu.VMEM((gps, l, SSD_GROUP_W), F32),
                        pltpu.VMEM((gps, l, SSD_STATE), BF16),
                        pltpu.VMEM((gps, l, SSD_STATE), BF16)],
        compiler_params=_params("parallel", "parallel", "arbitrary"), name="ssd_core",
    )(proj, proj, proj, proj, dt_t, dt_t, conv_p, head_p, r)


def _mlstm_kernel(q_ref, k_ref, v_ref, o_ref, gc_ref, gr_ref, gbc_ref, gbr_ref, hg_ref,
                  out_ref, c_ref, n_ref, m_ref):
    @pl.when(pl.program_id(1) == 0)
    def _():
        c_ref[...] = jnp.zeros_like(c_ref)
        n_ref[...] = jnp.zeros_like(n_ref)
        m_ref[...] = jnp.zeros_like(m_ref)

    for hi in range(MLSTM_HEADS):
        qk_cols, v_cols = pl.ds(hi * MLSTM_DQK, MLSTM_DQK), pl.ds(hi * MLSTM_DV, MLSTM_DV)
        _mlstm_head(q_ref.at[0, :, qk_cols], k_ref.at[0, :, qk_cols], v_ref.at[0, :, v_cols], o_ref.at[0, :, v_cols],
                    gc_ref.at[0, hi], gr_ref.at[0, hi], gbc_ref.at[hi], gbr_ref.at[hi], hg_ref.at[:, v_cols],
                    out_ref.at[0, :, v_cols], c_ref.at[hi], n_ref.at[hi], m_ref.at[hi])


def _mlstm_head(q_ref, k_ref, v_ref, o_ref, gc_ref, gr_ref, gbc_ref, gbr_ref, hg_ref,
                out_ref, c_ref, n_ref, m_ref):
    l = q_ref.shape[0]
    q = q_ref[...] * (MLSTM_DQK ** -0.5)
    k = k_ref[...]
    v16 = v_ref[...].astype(BF16)
    q16 = q.astype(BF16)
    gcol = gc_ref[...] + gbc_ref[...]
    grow = gr_ref[...] + gbr_ref[...]
    i_col, lf_col = gcol[:, 0:1], _log_sigmoid(gcol[:, 1:2])
    i_row, lf_row = grow[0:1, :], _log_sigmoid(grow[1:2, :])
    bcum = _cumsum_rows(lf_col)
    bcum_row = _cumsum_cols(lf_row)
    m_st = m_ref[0:1, 0:1]

    causal = _causal(l)
    intra = jnp.where(causal, bcum - bcum_row + i_row, -jnp.inf)
    inter = bcum + m_st
    m_t = jnp.maximum(inter, jnp.max(intra, axis=1, keepdims=True))
    w = jnp.exp(intra - m_t)
    scale_inter = jnp.exp(inter - m_t)
    qk = _dot_nt(q16, k.astype(BF16)) * w
    num = _dot(qk.astype(BF16), v16) + scale_inter * _dot(q16, c_ref[...].astype(BF16))
    qn = jnp.sum(q * n_ref[...], axis=1, keepdims=True)
    den = jnp.sum(qk, axis=1, keepdims=True) + scale_inter * qn
    h = num / jnp.maximum(jnp.abs(den), jnp.exp(-m_t))

    b_last = bcum[l - 1:l, :]
    tail = b_last - bcum + i_col
    m_new = jnp.maximum(b_last + m_st, jnp.max(tail, axis=0, keepdims=True))
    carry_scale = jnp.exp(b_last + m_st - m_new)
    wk = k * jnp.exp(tail - m_new)
    c_ref[...] = carry_scale * c_ref[...] + _dot_tn(wk.astype(BF16), v16)
    n_ref[...] = carry_scale * n_ref[...] + jnp.sum(wk, axis=0, keepdims=True)
    m_ref[...] = jnp.broadcast_to(m_new, m_ref.shape)

    hn = _rms_scale(h, hg_ref[...])
    out_ref[...] = (_sigmoid(o_ref[...]) * hn).astype(out_ref.dtype)


def mlstm_core(proj, gates, gate_b, head_g, *, batch, seq):
    l = MLSTM_CHUNK
    h_ = MLSTM_HEADS
    qk_w = h_ * MLSTM_DQK
    v_w = h_ * MLSTM_DV
    gsplit = jnp.stack([gates[..., :h_], gates[..., h_:2 * h_]], axis=-1)
    gcol = gsplit.transpose(0, 2, 1, 3)
    grow = gsplit.transpose(0, 2, 3, 1)
    gb = jnp.stack([gate_b[:h_], gate_b[h_:]], axis=-1).astype(F32)
    return pl.pallas_call(
        _mlstm_kernel, out_shape=jax.ShapeDtypeStruct((batch, seq, v_w), BF16),
        grid=(batch, seq // l),
        in_specs=[pl.BlockSpec((1, l, qk_w), lambda b, c: (b, c, 0)),
                  pl.BlockSpec((1, l, qk_w), lambda b, c: (b, c, 1)),
                  pl.BlockSpec((1, l, v_w), lambda b, c: (b, c, 2 * qk_w // v_w)),
                  pl.BlockSpec((1, l, v_w), lambda b, c: (b, c, 2 * qk_w // v_w + 1)),
                  pl.BlockSpec((1, h_, l, 2), lambda b, c: (b, 0, c, 0)),
                  pl.BlockSpec((1, h_, 2, l), lambda b, c: (b, 0, 0, c)),
                  pl.BlockSpec((h_, 1, 2), lambda b, c: (0, 0, 0)),
                  pl.BlockSpec((h_, 2, 1), lambda b, c: (0, 0, 0)),
                  pl.BlockSpec((1, v_w), lambda b, c: (0, 0))],
        out_specs=pl.BlockSpec((1, l, v_w), lambda b, c: (b, c, 0)),
        scratch_shapes=[pltpu.VMEM((h_, MLSTM_DQK, MLSTM_DV), F32),
                        pltpu.VMEM((h_, 1, MLSTM_DQK), F32),
                        pltpu.VMEM((h_, 8, LANES), F32)],
        compiler_params=_params("parallel", "arbitrary"), name="mlstm_core",
    )(proj, proj, proj, proj, gcol, grow, gb.reshape(h_, 1, 2), gb.reshape(h_, 2, 1),
      head_g.reshape(1, v_w).astype(F32))


def _s5_discretize_kernel(logdt_ref, are_ref, aim_ref, bre_ref, bim_ref,
                          abre_ref, abim_ref, bbre_ref, bbim_ref):
    ar, ai = are_ref[...], aim_ref[...]
    dt = jnp.exp(logdt_ref[...])
    mag = jnp.exp(ar * dt)
    abar_re, abar_im = mag * jnp.cos(ai * dt), mag * jnp.sin(ai * dt)
    den = ar * ar + ai * ai
    zoh_re = ((abar_re - 1.0) * ar + abar_im * ai) / den
    zoh_im = (abar_im * ar - (abar_re - 1.0) * ai) / den
    abre_ref[...] = abar_re
    abim_ref[...] = abar_im
    for c in range(S5_GROUP):
        br, bi = bre_ref[c], bim_ref[c]
        bbre_ref[c] = zoh_re * br - zoh_im * bi
        bbim_ref[c] = zoh_re * bi + zoh_im * br


def s5_discretize(log_dt, a_re, a_im, b_re, b_im):
    g_, p_ = a_re.shape
    sds = jax.ShapeDtypeStruct
    return pl.pallas_call(
        _s5_discretize_kernel,
        out_shape=[sds((g_, p_), F32), sds((g_, p_), F32),
                   sds((S5_GROUP, g_, p_), F32), sds((S5_GROUP, g_, p_), F32)],
        name="s5_discretize",
    )(log_dt.reshape(g_, 1).astype(F32), a_re.astype(F32), a_im.astype(F32),
      b_re.astype(F32).transpose(2, 0, 1), b_im.astype(F32).transpose(2, 0, 1))


def _s5_kernel(u_ref, bw_ref, cw_ref, a_ref, d_ref, y_ref, st_ref, bu_ref, *, batch):
    ts = u_ref.shape[0]
    half = S5_BLOCK_STATE
    nblk = S5_BLOCKS_PER_STEP

    @pl.when(pl.program_id(1) == 0)
    def _():
        st_ref[...] = jnp.zeros_like(st_ref)

    def lanes(n):
        return slice(n * LANES, (n + 1) * LANES)

    def project_in(n):
        u = u_ref[:, :, lanes(n)].reshape(ts * batch, LANES)
        bu_ref[n] = _dot(u.astype(BF16), bw_ref[n])

    def scan(n):
        a = a_ref[n]
        a_re = jnp.broadcast_to(a[:, :half], (batch, half))
        a_im = jnp.broadcast_to(a[:, half:], (batch, half))
        s_re, s_im = st_ref[n, :, 0:half], st_ref[n, :, half:2 * half]
        for t in range(ts):
            rows = slice(t * batch, (t + 1) * batch)
            s_re, s_im = (a_re * s_re - a_im * s_im + bu_ref[n, rows, 0:half],
                          a_re * s_im + a_im * s_re + bu_ref[n, rows, half:2 * half])
            bu_ref[n, rows, 0:half] = s_re
            bu_ref[n, rows, half:2 * half] = s_im
        st_ref[n, :, 0:half] = s_re
        st_ref[n, :, half:2 * half] = s_im

    def project_out(n):
        u = u_ref[:, :, lanes(n)].reshape(ts * batch, LANES)
        y = _dot(bu_ref[n].astype(BF16), cw_ref[n]) + d_ref[:, lanes(n)] * u
        y_ref[:, :, lanes(n)] = jax.nn.gelu(y).astype(y_ref.dtype).reshape(ts, batch, LANES)

    project_in(0)
    for n in range(nblk):
        if n + 1 < nblk:
            project_in(n + 1)
        scan(n)
        if n > 0:
            project_out(n - 1)
    project_out(nblk - 1)


def s5_core(u_tm, bw, cw, a_vec, d_skip, *, batch, seq):
    width = u_tm.shape[2]
    nblk = width // LANES
    ts = min(S5_TIME_BLOCK, seq)
    per = S5_BLOCKS_PER_STEP
    return pl.pallas_call(
        functools.partial(_s5_kernel, batch=batch),
        out_shape=jax.ShapeDtypeStruct((seq, batch, width), BF16),
        grid=(nblk // per, seq // ts),
        in_specs=[pl.BlockSpec((ts, batch, per * LANES), lambda j, i: (i, 0, j)),
                  pl.BlockSpec((per, LANES, 2 * S5_BLOCK_STATE), lambda j, i: (j, 0, 0)),
                  pl.BlockSpec((per, 2 * S5_BLOCK_STATE, LANES), lambda j, i: (j, 0, 0)),
                  pl.BlockSpec((per, 1, 2 * S5_BLOCK_STATE), lambda j, i: (j, 0, 0)),
                  pl.BlockSpec((1, per * LANES), lambda j, i: (0, j))],
        out_specs=pl.BlockSpec((ts, batch, per * LANES), lambda j, i: (i, 0, j)),
        scratch_shapes=[pltpu.VMEM((per, batch, 2 * S5_BLOCK_STATE), F32),
                        pltpu.VMEM((per, ts * batch, 2 * S5_BLOCK_STATE), F32)],
        compiler_params=_params("parallel", "arbitrary"), name="s5_core",
    )(u_tm, bw, cw, a_vec, d_skip.reshape(1, width).astype(F32))


def _s5_block_weights(abar_re, abar_im, bbar_re, bbar_im, c_re, c_im):
    g_, p_ = abar_re.shape
    gb = S5_GROUPS_PER_BLOCK
    nblk = g_ // gb
    eye = jnp.eye(gb, dtype=F32)

    def in_blocks(bbar):
        b = bbar.reshape(S5_GROUP, nblk, gb, p_).transpose(1, 2, 0, 3)
        return jnp.einsum("ngcp,gh->ngchp", b, eye).reshape(nblk, gb * S5_GROUP, gb * p_)

    def out_blocks(cmat):
        c = cmat.reshape(nblk, gb, S5_GROUP, p_).transpose(0, 1, 3, 2)
        return jnp.einsum("ngpc,gh->ngphc", c, eye).reshape(nblk, gb * p_, gb * S5_GROUP)

    bw = jnp.concatenate([in_blocks(bbar_re), in_blocks(bbar_im)], axis=2).astype(BF16)
    cw = jnp.concatenate([out_blocks(c_re.astype(F32)), out_blocks(-c_im.astype(F32))], axis=1).astype(BF16)
    a_vec = jnp.concatenate([abar_re.reshape(nblk, 1, gb * p_), abar_im.reshape(nblk, 1, gb * p_)], axis=2)
    return bw, cw, a_vec


SSD_MAIN_COLS = 2 * SSD_GROUPS * SSD_GROUP_W + 2 * SSD_GROUPS * SSD_STATE
MLSTM_MAIN_COLS = 2 * MLSTM_HEADS * (MLSTM_DQK + MLSTM_DV)
TOKEN_TILE = 1024
COL_TILE = 1024
IN_PROJ_COL_TILE = 2048


def _in_proj_col_tile(g):
    return IN_PROJ_COL_TILE if g is None else COL_TILE
MLP_TOKEN_TILE = 1024
MLP_FF_TILE = 512


def _small_head(w_in, idx, main, transposed):
    w = jnp.pad(w_in[idx, :, main:], ((0, 0), (0, LANES - (w_in.shape[2] - main)))).astype(BF16)
    return w.T if transposed else w


def _ssd_layer(h, xin, g, w_in, w_in16, w_out16, idx, conv_w, conv_b, dt_bias, a_log, d_skip, norm_g, *, batch, seq):
    inner = SSD_GROUPS * SSD_GROUP_W
    main = SSD_MAIN_COLS
    proj, dt_t = in_proj(xin, g, w_in16, idx, n=main, w_small=_small_head(w_in, idx, main, True), small="cols",
                         tm=TOKEN_TILE, tn=_in_proj_col_tile(g), name="ssd_in_proj")
    y16, ss = ssd_core(proj.reshape(batch, seq, main), dt_t, conv_w, conv_b, dt_bias, a_log, d_skip, norm_g,
                       batch=batch, seq=seq)
    return rms_matmul_residual(y16.reshape(batch * seq, inner), ss.reshape(batch * seq, -1), w_out16, idx, h,
                               tm=TOKEN_TILE // 2, tn=w_out16.shape[2], name="ssd_out_proj")


def _mlstm_layer(h, xin, g, w_in, w_in16, w_out16, idx, gate_b, head_g, *, batch, seq):
    main = MLSTM_MAIN_COLS
    proj, gates = in_proj(xin, g, w_in16, idx, n=main, w_small=_small_head(w_in, idx, main, False), small="rows",
                          tm=TOKEN_TILE, tn=_in_proj_col_tile(g), name="mlstm_in_proj")
    hs = mlstm_core(proj.reshape(batch, seq, main),
                    gates[:, :2 * MLSTM_HEADS].reshape(batch, seq, 2 * MLSTM_HEADS),
                    gate_b, head_g, batch=batch, seq=seq)
    return matmul_residual(hs.reshape(batch * seq, -1), w_out16, idx, h,
                           tm=TOKEN_TILE // 2, tn=w_out16.shape[2], name="mlstm_out_proj")


def _s5_layer(h, xin, g, w_in16, w_out16, idx, b_re, b_im, c_re, c_im, d_skip, log_dt, a_re, a_im, *, batch, seq):
    width = w_in16.shape[2]
    tm = min(TOKEN_TILE, seq)
    nt = seq // tm
    tn = _in_proj_col_tile(g)
    ncol = width // tn
    u_tm = in_proj(xin, g, w_in16, idx, n=width, tm=tm, tn=tn,
                   out_index=lambda i, j: (i % nt, (i // nt) * ncol + j),
                   out_shape=(seq, batch * width), name="s5_in_proj")
    abar_re, abar_im, bbar_re, bbar_im = s5_discretize(log_dt, a_re, a_im, b_re, b_im)
    bw, cw, a_vec = _s5_block_weights(abar_re, abar_im, bbar_re, bbar_im, c_re, c_im)
    y_tm = s5_core(u_tm.reshape(seq, batch, width), bw, cw, a_vec, d_skip, batch=batch, seq=seq)
    return glu_matmul_residual(y_tm.reshape(seq, batch * width), w_out16, idx, h,
                               batch=batch, tm=tm // 2, tn=w_out16.shape[2] // 2, name="s5_out_proj")


def kernel(x, norm_mix_g, norm_mlp_g, ssd_w_in, ssd_conv_w, ssd_conv_b, ssd_dt_bias, ssd_a_log, ssd_d, ssd_norm_g, ssd_w_out, mlstm_w_in, mlstm_gate_b, mlstm_head_g, mlstm_w_out, s5_w_in, s5_b_re, s5_b_im, s5_c_re, s5_c_im, s5_d, s5_log_dt, s5_a_re, s5_a_im, s5_w_out, mlp_w1, mlp_w2, final_norm_g):
    batch, seq, d = x.shape
    depth = norm_mix_g.shape[0]
    ssd_in16, ssd_out16 = ssd_w_in.astype(BF16), ssd_w_out.astype(BF16)
    mlstm_in16, mlstm_out16 = mlstm_w_in.astype(BF16), mlstm_w_out.astype(BF16)
    s5_in16, s5_out16 = s5_w_in.astype(BF16), s5_w_out.astype(BF16)
    mlp_w1_16, mlp_w2_16 = mlp_w1.astype(BF16), mlp_w2.astype(BF16)
    h = x.reshape(batch * seq, d)
    xin, g = h, norm_mix_g[0]
    for layer in range(depth):
        kind, idx = layer % 3, layer // 3
        if kind == 0:
            h = _ssd_layer(h, xin, g, ssd_w_in, ssd_in16, ssd_out16, idx, ssd_conv_w[idx], ssd_conv_b[idx],
                           ssd_dt_bias[idx], ssd_a_log[idx], ssd_d[idx], ssd_norm_g[idx], batch=batch, seq=seq)
        elif kind == 1:
            h = _mlstm_layer(h, xin, g, mlstm_w_in, mlstm_in16, mlstm_out16, idx, mlstm_gate_b[idx],
                             mlstm_head_g[idx], batch=batch, seq=seq)
        else:
            h = _s5_layer(h, xin, g, s5_in16, s5_out16, idx, s5_b_re[idx], s5_b_im[idx], s5_c_re[idx], s5_c_im[idx],
                          s5_d[idx], s5_log_dt[idx], s5_a_re[idx], s5_a_im[idx], batch=batch, seq=seq)
        final = layer == depth - 1
        out = mlp_block(h, norm_mlp_g[layer], mlp_w1_16, mlp_w2_16, layer,
                        final_norm_g if final else norm_mix_g[layer + 1], final=final,
                        tm=min(MLP_TOKEN_TILE, batch * seq), tf=MLP_FF_TILE, name=f"mlp_{layer}")
        if final:
            h = out
        else:
            h, xin = out
            g = None
    return h.reshape(batch, seq, d)
```
